```python
import math
import jax, jax.numpy as jnp
from jax import lax
import numpy as np

D_MODEL = 1024
BATCH = 4
SEQ = 8192
DEPTH = 4
DEC_BATCH = 16
DEC_SEQ = 2048
PAST_LEN = 128

GRID_W = 64
HEAD_DIM = 64
N_HEADS_A = 4
W_A = N_HEADS_A * HEAD_DIM
NA_ROWS = 8
NA_COLS = 16
N_HEADS_B = 8
W_B = N_HEADS_B * HEAD_DIM
DILATIONS = ((128, 1), (512, 4), (2048, 16))
SSM_GROUP = 16
W_C = 256
N_GROUPS_C = W_C // SSM_GROUP
SSM_STATE = 64
MIX_W = W_A + W_B + W_C
IN_W = 3 * W_A + 3 * W_B + W_C
NUM_BUCKETS = 32
MAX_DISTANCE = 1024
PEER_HEADS = 8
PEER_KEYS = 128
PEER_EXPERTS = PEER_KEYS * PEER_KEYS
PEER_TOPK = 16
PEER_DKEY = 256
PEER_CHUNK = 128
EPS = 1e-6
NEG = -1e30

kernel_name = "hybrid_natten_dilated_s5_peer_encoder"


def rmsnorm(x, g):
    xf = x.astype(jnp.float32)
    y = xf * lax.rsqrt(jnp.mean(xf * xf, axis=-1, keepdims=True) + EPS)
    return (y * g.astype(jnp.float32)).astype(x.dtype)


def t5_bucket(rel):
    nb = NUM_BUCKETS // 2
    ret = jnp.where(rel > 0, nb, 0)
    n = jnp.abs(rel)
    max_exact = nb // 2
    nf = jnp.maximum(n, 1).astype(jnp.float32)
    large = max_exact + (jnp.log(nf / max_exact) / math.log(MAX_DISTANCE / max_exact)
                         * (nb - max_exact)).astype(jnp.int32)
    large = jnp.minimum(large, nb - 1)
    return ret + jnp.where(n < max_exact, n, large)


def neighborhood_attention(q, k, v, rpb):
    n, l, h, dh = q.shape
    rows = l // GRID_W
    kh = min(NA_ROWS, rows)
    qg = q.reshape(n, rows, GRID_W, h, dh)
    kg = k.reshape(n, rows, GRID_W, h, dh)
    vg = v.reshape(n, rows, GRID_W, h, dh)
    r = jnp.arange(rows)
    row_start = jnp.clip(r - kh // 2, 0, rows - kh)
    row_idx = row_start[:, None] + jnp.arange(kh)[None, :]
    k_nb = kg[:, row_idx]
    v_nb = vg[:, row_idx]
    c = jnp.arange(GRID_W)
    col_start = jnp.clip(c - NA_COLS // 2, 0, GRID_W - NA_COLS)
    col_ok = (c[None, :] >= col_start[:, None]) & (c[None, :] < col_start[:, None] + NA_COLS)
    row_off = row_idx - r[:, None] + (NA_ROWS - 1)
    col_off = jnp.clip(c[None, :] - c[:, None], -(NA_COLS - 1), NA_COLS - 1) + (NA_COLS - 1)
    bias = rpb[:, row_off][:, :, :, col_off]
    bias = jnp.transpose(bias, (1, 0, 3, 2, 4)).astype(jnp.float32)
    logits = jnp.einsum('nrchd,nrjwhd->nrhcjw', qg, k_nb).astype(jnp.float32) + bias[None]
    logits = jnp.where(col_ok[:, None, :], logits, NEG)
    p = jax.nn.softmax(logits.reshape(n, rows, h, GRID_W, kh * GRID_W), axis=-1)
    p = p.reshape(n, rows, h, GRID_W, kh, GRID_W).astype(v.dtype)
    out = jnp.einsum('nrhcjw,nrjwhd->nrchd', p, v_nb)
    return out.reshape(n, l, h * dh)


def dilated_band_attention(q, k, v, radius, dilation, rel_bias):
    n, l, h, dh = q.shape
    blk = radius
    nb = -(-l // blk)
    lp = nb * blk
    pad = lp - l
    qb = jnp.pad(q, ((0, 0), (0, pad), (0, 0), (0, 0))).reshape(n, nb, blk, h, dh)
    kp = jnp.pad(k, ((0, 0), (blk, pad + blk), (0, 0), (0, 0)))
    vp = jnp.pad(v, ((0, 0), (blk, pad + blk), (0, 0), (0, 0)))

    def band(a):
        return jnp.concatenate([a[:, i * blk:i * blk + lp].reshape(n, nb, blk, h, dh) for i in range(3)], axis=2)

    kb, vb = band(kp), band(vp)
    qi = jnp.arange(blk)
    ki = jnp.arange(3 * blk) - blk
    rel = ki[None, :] - qi[:, None]
    kpos = jnp.arange(nb)[:, None] * blk + ki[None, :]
    ok = (jnp.abs(rel) <= radius)[None] & ((kpos >= 0) & (kpos < l))[:, None, :]
    bias = jnp.transpose(rel_bias[t5_bucket(rel * dilation)], (2, 0, 1)).astype(jnp.float32)
    logits = jnp.einsum('nbqhd,nbkhd->nbhqk', qb, kb).astype(jnp.float32) + bias
    logits = jnp.where(ok[None, :, None], logits, NEG)
    m = jnp.max(logits, axis=-1, keepdims=True)
    p = jnp.exp(logits - m)
    s = jnp.sum(p, axis=-1, keepdims=True)
    o = jnp.einsum('nbhqk,nbkhd->nbqhd', (p / s).astype(v.dtype), vb)
    lse = (m + jnp.log(s))[..., 0]
    o = o.reshape(n, lp, h, dh)[:, :l]
    lse = jnp.transpose(lse, (0, 1, 3, 2)).reshape(n, lp, h)[:, :l]
    return o, lse


def dilated_mixture(q, k, v, rel_bias):
    n, l, h, dh = q.shape
    outs, lses = [], []
    for window, d in DILATIONS:
        radius = window // (2 * d)
        ls = l // d

        def to_sub(a):
            return jnp.transpose(a.reshape(n, ls, d, h, dh), (0, 2, 1, 3, 4)).reshape(n * d, ls, h, dh)

        o, lse = dilated_band_attention(to_sub(q), to_sub(k), to_sub(v), radius, d, rel_bias)
        outs.append(jnp.transpose(o.reshape(n, d, ls, h, dh), (0, 2, 1, 3, 4)).reshape(n, l, h, dh))
        lses.append(jnp.transpose(lse.reshape(n, d, ls, h), (0, 2, 1, 3)).reshape(n, l, h))
    w = jax.nn.softmax(jnp.stack(lses), axis=0)
    out = jnp.einsum('gnlh,gnlhd->nlhd', w, jnp.stack(outs).astype(jnp.float32))
    return out.reshape(n, l, h * dh).astype(q.dtype)


def _complex_lin_combine(e1, e2):
    a1r, a1i, b1r, b1i = e1
    a2r, a2i, b2r, b2i = e2
    return (a2r * a1r - a2i * a1i,
            a2r * a1i + a2i * a1r,
            a2r * b1r - a2i * b1i + b2r,
            a2r * b1i + a2i * b1r + b2i)


def s5_mixer(u, a_re, a_im, log_step, b_re, b_im, c_re, c_im, d_skip, w_glu, b_glu):
    n, l, _ = u.shape
    uf = u.astype(jnp.float32)
    ug = uf.reshape(n, l, N_GROUPS_C, SSM_GROUP)
    y = d_skip.astype(jnp.float32) * uf
    for direction in range(2):
        ar = a_re[direction].astype(jnp.float32)
        ai = a_im[direction].astype(jnp.float32)
        step = jnp.exp(log_step[direction].astype(jnp.float32))[:, None]
        decay = jnp.exp(ar * step)
        lb_re = decay * jnp.cos(ai * step)
        lb_im = decay * jnp.sin(ai * step)
        nr = lb_re - 1.0
        den = ar * ar + ai * ai
        z_re = (nr * ar + lb_im * ai) / den
        z_im = (lb_im * ar - nr * ai) / den
        br = b_re[direction].astype(jnp.float32)
        bi = b_im[direction].astype(jnp.float32)
        bb_re = z_re[..., None] * br - z_im[..., None] * bi
        bb_im = z_re[..., None] * bi + z_im[..., None] * br
        x_re = jnp.einsum('nlgi,gpi->nlgp', ug, bb_re)
        x_im = jnp.einsum('nlgi,gpi->nlgp', ug, bb_im)
        a_r = jnp.broadcast_to(lb_re, x_re.shape)
        a_i = jnp.broadcast_to(lb_im, x_im.shape)
        _, _, h_re, h_im = lax.associative_scan(_complex_lin_combine, (a_r, a_i, x_re, x_im),
                                                axis=1, reverse=(direction == 1))
        yd = (jnp.einsum('nlgp,gip->nlgi', h_re, c_re[direction].astype(jnp.float32))
              - jnp.einsum('nlgp,gip->nlgi', h_im, c_im[direction].astype(jnp.float32)))
        y = y + yd.reshape(n, l, W_C)
    g = jax.nn.gelu(y, approximate=False)
    out = g * jax.nn.sigmoid(g @ w_glu.astype(jnp.float32) + b_glu.astype(jnp.float32))
    return out.astype(u.dtype)


def peer_ffn(x, wq, sub_keys, u_tab, v_tab):
    n, l, dm = x.shape
    xt = x.reshape((n * l) // PEER_CHUNK, PEER_CHUNK, dm)

    def chunk_fn(xc):
        q = (xc @ wq).reshape(PEER_CHUNK, PEER_HEADS, 2, PEER_DKEY // 2)
        scores = jnp.einsum('thsd,hskd->thsk', q, sub_keys).astype(jnp.float32)
        s_top, i_top = lax.top_k(scores, PEER_TOPK)
        cand = (s_top[:, :, 0, :, None] + s_top[:, :, 1, None, :]).reshape(PEER_CHUNK, PEER_HEADS, PEER_TOPK * PEER_TOPK)
        cand_idx = (i_top[:, :, 0, :, None] * PEER_KEYS + i_top[:, :, 1, None, :]).reshape(PEER_CHUNK, PEER_HEADS, PEER_TOPK * PEER_TOPK)
        best, pos = lax.top_k(cand, PEER_TOPK)
        experts = jnp.take_along_axis(cand_idx, pos, axis=-1)
        gates = jax.nn.softmax(best, axis=-1)
        u_sel = u_tab[experts]
        act = jax.nn.gelu(jnp.einsum('td,thkd->thk', xc, u_sel).astype(jnp.float32), approximate=False)
        v_sel = v_tab[experts]
        return jnp.einsum('thk,thkd->td', (gates * act).astype(xc.dtype), v_sel)

    return lax.map(chunk_fn, xt).reshape(n, l, dm)


def encoder_trunk(x, norm1_g, w_in, rpb_a, rel_bias, ssm_a_re, ssm_a_im, ssm_log_step, ssm_b_re, ssm_b_im,
                  ssm_c_re, ssm_c_im, ssm_d, w_glu, b_glu, out_norm_a, out_norm_b, out_norm_c, w_out,
                  norm2_g, peer_wq, peer_keys, peer_u, peer_v, final_g):
    n, l, _ = x.shape
    splits = [W_A, 2 * W_A, 3 * W_A, 3 * W_A + W_B, 3 * W_A + 2 * W_B, 3 * W_A + 3 * W_B]
    scale = HEAD_DIM ** -0.5
    for layer in range(DEPTH):
        h = rmsnorm(x, norm1_g[layer])
        proj = h @ w_in[layer]
        qa, ka, va, qb, kb, vb, uc = jnp.split(proj, splits, axis=-1)
        ha = lambda a: a.reshape(n, l, N_HEADS_A, HEAD_DIM)
        hb = lambda a: a.reshape(n, l, N_HEADS_B, HEAD_DIM)
        oa = neighborhood_attention(ha(qa) * scale, ha(ka), ha(va), rpb_a[layer])
        ob = dilated_mixture(hb(qb) * scale, hb(kb), hb(vb), rel_bias)
        oc = s5_mixer(uc, ssm_a_re[layer], ssm_a_im[layer], ssm_log_step[layer], ssm_b_re[layer], ssm_b_im[layer],
                      ssm_c_re[layer], ssm_c_im[layer], ssm_d[layer], w_glu[layer], b_glu[layer])
        mix = jnp.concatenate([rmsnorm(oa, out_norm_a[layer]), rmsnorm(ob, out_norm_b[layer]),
                               rmsnorm(oc, out_norm_c[layer])], axis=-1)
        x = x + mix @ w_out[layer]
        x = x + peer_ffn(rmsnorm(x, norm2_g[layer]), peer_wq[layer], peer_keys[layer], peer_u[layer], peer_v[layer])
    return rmsnorm(x, final_g)


def setup_inputs(seed: int = 0) -> dict:
    key = jax.random.key(seed)
    ks = jax.random.split(key, 32)
    f32 = jnp.float32

    def nrm(k, shape, s):
        return jax.random.normal(k, shape, f32) * s

    G, P = N_GROUPS_C, SSM_STATE
    n_idx = jnp.arange(P, dtype=f32)
    return {
        "x_prompt": nrm(ks[0], (BATCH, SEQ, D_MODEL), 1.0),
        "x_sample": nrm(ks[1], (DEC_BATCH, DEC_SEQ, D_MODEL), 1.0),
        "norm1_g": 1.0 + nrm(ks[2], (DEPTH, D_MODEL), 0.01),
        "w_in": nrm(ks[3], (DEPTH, D_MODEL, IN_W), D_MODEL ** -0.5),
        "rpb_a": nrm(ks[4], (DEPTH, N_HEADS_A, 2 * NA_ROWS - 1, 2 * NA_COLS - 1), 0.02),
        "rel_bias": nrm(ks[5], (NUM_BUCKETS, N_HEADS_B), 0.1),
        "ssm_a_re": -0.5 + nrm(ks[6], (DEPTH, 2, G, P), 0.01),
        "ssm_a_im": jnp.pi * n_idx + nrm(ks[7], (DEPTH, 2, G, P), 0.01),
        "ssm_log_step": jax.random.uniform(ks[8], (DEPTH, 2, G), f32, math.log(0.001), math.log(0.1)),
        "ssm_b_re": nrm(ks[9], (DEPTH, 2, G, P, SSM_GROUP), (2 * SSM_GROUP) ** -0.5),
        "ssm_b_im": nrm(ks[10], (DEPTH, 2, G, P, SSM_GROUP), (2 * SSM_GROUP) ** -0.5),
        "ssm_c_re": nrm(ks[11], (DEPTH, 2, G, SSM_GROUP, P), (2 * P) ** -0.5),
        "ssm_c_im": nrm(ks[12], (DEPTH, 2, G, SSM_GROUP, P), (2 * P) ** -0.5),
        "ssm_d": nrm(ks[13], (DEPTH, W_C), 1.0),
        "w_glu": nrm(ks[14], (DEPTH, W_C, W_C), W_C ** -0.5),
        "b_glu": nrm(ks[15], (DEPTH, W_C), 0.01),
        "out_norm_a": 1.0 + nrm(ks[16], (DEPTH, W_A), 0.01),
        "out_norm_b": 1.0 + nrm(ks[17], (DEPTH, W_B), 0.01),
        "out_norm_c": 1.0 + nrm(ks[18], (DEPTH, W_C), 0.01),
        "w_out": nrm(ks[19], (DEPTH, MIX_W, D_MODEL), MIX_W ** -0.5),
        "norm2_g": 1.0 + nrm(ks[20], (DEPTH, D_MODEL), 0.01),
        "peer_wq": nrm(ks[21], (DEPTH, D_MODEL, PEER_HEADS * PEER_DKEY), D_MODEL ** -0.5),
        "peer_keys": nrm(ks[22], (DEPTH, PEER_HEADS, 2, PEER_KEYS, PEER_DKEY // 2), (PEER_DKEY // 2) ** -0.5),
        "peer_u": nrm(ks[23], (DEPTH, PEER_EXPERTS, D_MODEL), D_MODEL ** -0.5),
        "peer_v": nrm(ks[24], (DEPTH, PEER_EXPERTS, D_MODEL), PEER_HEADS ** -0.5),
        "final_g": 1.0 + nrm(ks[25], (D_MODEL,), 0.01),
    }


def reference(x_prompt, x_sample, norm1_g, w_in, rpb_a, rel_bias, ssm_a_re, ssm_a_im, ssm_log_step, ssm_b_re,
              ssm_b_im, ssm_c_re, ssm_c_im, ssm_d, w_glu, b_glu, out_norm_a, out_norm_b, out_norm_c, w_out,
              norm2_g, peer_wq, peer_keys, peer_u, peer_v, final_g):
    y_prompt = encoder_trunk(x_prompt, norm1_g, w_in, rpb_a, rel_bias, ssm_a_re, ssm_a_im, ssm_log_step, ssm_b_re,
                             ssm_b_im, ssm_c_re, ssm_c_im, ssm_d, w_glu, b_glu, out_norm_a, out_norm_b, out_norm_c,
                             w_out, norm2_g, peer_wq, peer_keys, peer_u, peer_v, final_g)
    y_sample = encoder_trunk(x_sample, norm1_g, w_in, rpb_a, rel_bias, ssm_a_re, ssm_a_im, ssm_log_step, ssm_b_re,
                             ssm_b_im, ssm_c_re, ssm_c_im, ssm_d, w_glu, b_glu, out_norm_a, out_norm_b, out_norm_c,
                             w_out, norm2_g, peer_wq, peer_keys, peer_u, peer_v, final_g)
    return (y_prompt, y_sample)
```

```python
import functools
import math

import jax
import jax.numpy as jnp
from jax import lax
from jax.experimental import pallas as pl
from jax.experimental.pallas import tpu as pltpu

F32 = jnp.float32
BF16 = jnp.bfloat16
I32 = jnp.int32

D_MODEL = 1024
DEPTH = 4
GRID_W = 64
HEAD_DIM = 64
N_HEADS_A = 4
W_A = N_HEADS_A * HEAD_DIM
NA_ROWS = 8
NA_COLS = 16
N_HEADS_B = 8
W_B = N_HEADS_B * HEAD_DIM
DILATIONS = ((128, 1), (512, 4), (2048, 16))
SSM_GROUP = 16
W_C = 256
N_GROUPS_C = W_C // SSM_GROUP
SSM_STATE = 64
NUM_BUCKETS = 32
MAX_DISTANCE = 1024
PEER_HEADS = 8
PEER_KEYS = 128
PEER_EXPERTS = PEER_KEYS * PEER_KEYS
PEER_TOPK = 16
PEER_DKEY = 256
PEER_SEL = PEER_HEADS * PEER_TOPK
EPS = 1e-6
NEG = -1e30

BAND = 64
S5_CHUNK = 64
S5_PAD = 128
HALF = D_MODEL // 2
WORD_ROWS = HALF // 128

MIB = 1024 * 1024


def _params(vmem_mib, n_axes=1):
    return pltpu.CompilerParams(
        vmem_limit_bytes=vmem_mib * MIB,
        dimension_semantics=("arbitrary",) * n_axes,
    )


def _rms(x, g):
    return x * lax.rsqrt(jnp.mean(x * x, axis=-1, keepdims=True) + EPS) * g


def _inproj_kernel(x_ref, g_ref, w_ref, a_ref, b_ref, c_ref):
    h = _rms(x_ref[...], g_ref[...]).astype(BF16)
    p = jnp.dot(h, w_ref[...], preferred_element_type=F32)
    a_ref[...] = p[:, : 3 * W_A].astype(BF16)
    b_ref[...] = p[:, 3 * W_A : 3 * W_A + 3 * W_B].astype(BF16)
    c_ref[...] = p[:, 3 * W_A + 3 * W_B :]


def _inproj(x, g, w):
    ntok = x.shape[0]
    tm = 512
    wcols = w.shape[1]
    return pl.pallas_call(
        _inproj_kernel,
        grid=(ntok // tm,),
        in_specs=[
            pl.BlockSpec((tm, D_MODEL), lambda i: (i, 0)),
            pl.BlockSpec((1, D_MODEL), lambda i: (0, 0)),
            pl.BlockSpec((D_MODEL, wcols), lambda i: (0, 0)),
        ],
        out_specs=[
            pl.BlockSpec((tm, 3 * W_A), lambda i: (i, 0)),
            pl.BlockSpec((tm, 3 * W_B), lambda i: (i, 0)),
            pl.BlockSpec((tm, W_C), lambda i: (i, 0)),
        ],
        out_shape=[
            jax.ShapeDtypeStruct((ntok, 3 * W_A), BF16),
            jax.ShapeDtypeStruct((ntok, 3 * W_B), BF16),
            jax.ShapeDtypeStruct((ntok, W_C), F32),
        ],
        compiler_params=_params(48),
        name="inproj",
    )(x, g, w)


def _na_kernel(q_ref, k_ref, v_ref, b_ref, g_ref, o_ref, *, rows, rows_per_tile):
    i = pl.program_id(1)
    win = NA_ROWS * GRID_W
    for j in range(rows_per_tile):
        r = i * rows_per_tile + j
        rs = jnp.clip(r - NA_ROWS // 2, 0, rows - NA_ROWS)
        base = rs - r + (NA_ROWS - 1)
        start = pl.multiple_of(rs * GRID_W, GRID_W)
        outs = []
        for h in range(N_HEADS_A):
            cols = slice(h * HEAD_DIM, (h + 1) * HEAD_DIM)
            q = q_ref[0, j * GRID_W : (j + 1) * GRID_W, cols]
            k = k_ref[0, pl.ds(start, win), cols]
            v = v_ref[0, pl.ds(start, win), cols]
            s = lax.dot_general(q, k, (((1,), (1,)), ((), ())), preferred_element_type=F32)
            s = s + b_ref[h, base]
            m = jnp.max(s, axis=-1, keepdims=True)
            p = jnp.exp(s - m)
            l = jnp.sum(p, axis=-1, keepdims=True)
            outs.append(jnp.dot(p.astype(BF16), v, preferred_element_type=F32) / l)
        o = _rms(jnp.concatenate(outs, axis=-1), g_ref[...])
        o_ref[0, j * GRID_W : (j + 1) * GRID_W, :] = o.astype(BF16)


def _na_bias_table(rpb):
    c = jnp.arange(GRID_W)
    col_start = jnp.clip(c - NA_COLS // 2, 0, GRID_W - NA_COLS)
    col_ok = (c[None, :] >= col_start[:, None]) & (c[None, :] < col_start[:, None] + NA_COLS)
    col_off = jnp.clip(c[None, :] - c[:, None], -(NA_COLS - 1), NA_COLS - 1) + (NA_COLS - 1)
    t = rpb[:, :, col_off]
    t = jnp.where(col_ok[None, None], t, NEG)
    ro = jnp.arange(NA_ROWS)[:, None] + jnp.arange(NA_ROWS)[None, :]
    t = t[:, ro]
    t = jnp.transpose(t, (0, 1, 3, 2, 4))
    return t.reshape(N_HEADS_A, NA_ROWS, GRID_W, NA_ROWS * GRID_W).astype(F32)


def _na_attention(a3, bias, g):
    n, l, _ = a3.shape
    rows = l // GRID_W
    assert rows >= NA_ROWS and l % GRID_W == 0
    rows_per_tile = 8
    tq = rows_per_tile * GRID_W
    kern = functools.partial(_na_kernel, rows=rows, rows_per_tile=rows_per_tile)
    return pl.pallas_call(
        kern,
        grid=(n, l // tq),
        in_specs=[
            pl.BlockSpec((1, tq, W_A), lambda b, i: (b, i, 0)),
            pl.BlockSpec((1, l, W_A), lambda b, i: (b, 0, 1)),
            pl.BlockSpec((1, l, W_A), lambda b, i: (b, 0, 2)),
            pl.BlockSpec(bias.shape, lambda b, i: (0, 0, 0, 0)),
            pl.BlockSpec((1, W_A), lambda b, i: (0, 0)),
        ],
        out_specs=pl.BlockSpec((1, tq, W_A), lambda b, i: (b, i, 0)),
        out_shape=jax.ShapeDtypeStruct((n, l, W_A), BF16),
        compiler_params=_params(48, 2),
        name="na_attention",
    )(a3, a3, a3, bias, g)


def _band_kernel(q_ref, k_ref, v_ref, kp_ref, vp_ref, kn_ref, vn_ref, b_ref, o_ref, lse_ref,
                 *, blocks_per_tile, n_blocks):
    i = pl.program_id(1)
    col = lax.broadcasted_iota(I32, (BAND, 3 * BAND), 1)
    for j in range(blocks_per_tile):
        gb = i * blocks_per_tile + j
        rows = slice(j * BAND, (j + 1) * BAND)
        outs, lses = [], []
        for h in range(N_HEADS_B):
            cols = slice(h * HEAD_DIM, (h + 1) * HEAD_DIM)
            q = q_ref[0, rows, cols]
            if j == 0:
                kprev, vprev = kp_ref[0, :, cols], vp_ref[0, :, cols]
            else:
                prow = slice((j - 1) * BAND, j * BAND)
                kprev, vprev = k_ref[0, prow, cols], v_ref[0, prow, cols]
            if j == blocks_per_tile - 1:
                knext, vnext = kn_ref[0, :, cols], vn_ref[0, :, cols]
            else:
                nrow = slice((j + 1) * BAND, (j + 2) * BAND)
                knext, vnext = k_ref[0, nrow, cols], v_ref[0, nrow, cols]
            kw = jnp.concatenate([kprev, k_ref[0, rows, cols], knext], axis=0)
            vw = jnp.concatenate([vprev, v_ref[0, rows, cols], vnext], axis=0)
            s = lax.dot_general(q, kw, (((1,), (1,)), ((), ())), preferred_element_type=F32)
            s = s + b_ref[h]
            if j == 0 or j == blocks_per_tile - 1:
                lo = jnp.where(gb == 0, BAND, 0)
                hi = jnp.where(gb == n_blocks - 1, 2 * BAND, 3 * BAND)
                s = jnp.where((col >= lo) & (col < hi), s, NEG)
            m = jnp.max(s, axis=-1, keepdims=True)
            p = jnp.exp(s - m)
            l = jnp.sum(p, axis=-1, keepdims=True)
            outs.append(jnp.dot(p.astype(BF16), vw, preferred_element_type=F32) / l)
            lses.append(jnp.broadcast_to(m + jnp.log(l), (BAND, HEAD_DIM)))
        o_ref[0, rows, :] = jnp.concatenate(outs, axis=-1)
        lse_ref[0, rows, :] = jnp.concatenate(lses, axis=-1)


def _t5_bucket(rel):
    nb = NUM_BUCKETS // 2
    ret = jnp.where(rel > 0, nb, 0)
    n = jnp.abs(rel)
    max_exact = nb // 2
    nf = jnp.maximum(n, 1).astype(F32)
    large = max_exact + (jnp.log(nf / max_exact) / math.log(MAX_DISTANCE / max_exact)
                         * (nb - max_exact)).astype(I32)
    large = jnp.minimum(large, nb - 1)
    return ret + jnp.where(n < max_exact, n, large)


def _band_bias_table(rel_bias, dilation):
    qi = jnp.arange(BAND)
    ki = jnp.arange(3 * BAND) - BAND
    rel = ki[None, :] - qi[:, None]
    bias = jnp.transpose(rel_bias[_t5_bucket(rel * dilation)], (2, 0, 1)).astype(F32)
    return jnp.where((jnp.abs(rel) <= BAND)[None], bias, NEG)


def _band_attention(q, k, v, bias):
    n, ls, _ = q.shape
    assert ls % BAND == 0
    n_blocks = ls // BAND
    blocks_per_tile = min(8, n_blocks)
    tq = blocks_per_tile * BAND
    kern = functools.partial(_band_kernel, blocks_per_tile=blocks_per_tile, n_blocks=n_blocks)
    cur = pl.BlockSpec((1, tq, W_B), lambda b, i: (b, i, 0))
    prev = pl.BlockSpec((1, BAND, W_B), lambda b, i: (b, jnp.maximum(i * blocks_per_tile - 1, 0), 0))
    nxt = pl.BlockSpec((1, BAND, W_B), lambda b, i: (b, jnp.minimum((i + 1) * blocks_per_tile, n_blocks - 1), 0))
    return pl.pallas_call(
        kern,
        grid=(n, ls // tq),
        in_specs=[cur, cur, cur, prev, prev, nxt, nxt, pl.BlockSpec(bias.shape, lambda b, i: (0, 0, 0))],
        out_specs=[cur, cur],
        out_shape=[jax.ShapeDtypeStruct((n, ls, W_B), F32), jax.ShapeDtypeStruct((n, ls, W_B), F32)],
        compiler_params=_params(48, 2),
        name="band_attention",
    )(q, k, v, k, v, k, v, bias)


def _merge_kernel(o1, o2, o3, l1, l2, l3, g_ref, out_ref):
    a, b, c = l1[...], l2[...], l3[...]
    m = jnp.maximum(jnp.maximum(a, b), c)
    ea, eb, ec = jnp.exp(a - m), jnp.exp(b - m), jnp.exp(c - m)
    den = ea + eb + ec
    o = (ea / den) * o1[...] + (eb / den) * o2[...] + (ec / den) * o3[...]
    out_ref[...] = _rms(o, g_ref[...]).astype(BF16)


def _merge(os_, ls_, g):
    ntok = os_[0].shape[0]
    tm = 512
    blk = pl.BlockSpec((tm, W_B), lambda i: (i, 0))
    return pl.pallas_call(
        _merge_kernel,
        grid=(ntok // tm,),
        in_specs=[blk] * 6 + [pl.BlockSpec((1, W_B), lambda i: (0, 0))],
        out_specs=blk,
        out_shape=jax.ShapeDtypeStruct((ntok, W_B), BF16),
        compiler_params=_params(48),
        name="dilated_merge",
    )(*os_, *ls_, g)


def _dilated_mixture(b3, biases, g):
    n, l, _ = b3.shape
    q, k, v = b3[..., :W_B], b3[..., W_B : 2 * W_B], b3[..., 2 * W_B :]
    outs, lses = [], []
    for (window, d), bias in zip(DILATIONS, biases):
        assert window // (2 * d) == BAND and l % d == 0
        ls = l // d

        def to_sub(a):
            return jnp.transpose(a.reshape(n, ls, d, W_B), (0, 2, 1, 3)).reshape(n * d, ls, W_B)

        def from_sub(a):
            return jnp.transpose(a.reshape(n, d, ls, W_B), (0, 2, 1, 3)).reshape(n * l, W_B)

        o, lse = _band_attention(to_sub(q), to_sub(k), to_sub(v), bias)
        outs.append(from_sub(o))
        lses.append(from_sub(lse))
    return _merge(outs, lses, g)


def _s5_kernel(u_ref, wt_ref, e_ref, f_ref, lam_ref, d_ref, y_ref, s_scr, h_scr, *, nseq, cps):
    u = u_ref[0]
    ub = u.astype(BF16)
    s_scr[...] = jnp.dot(ub, e_ref[0], preferred_element_type=F32)
    lam = lam_ref[0]
    lfr, lfi = lam[:, 0:S5_PAD], lam[:, S5_PAD : 2 * S5_PAD]
    lbr, lbi = lam[:, 2 * S5_PAD : 3 * S5_PAD], lam[:, 3 * S5_PAD :]
    zero = jnp.zeros((nseq, S5_PAD), F32)
    fr, fi, br, bi = zero, zero, zero, zero
    for c in range(cps):
        rf = slice(c * nseq, (c + 1) * nseq)
        h_scr[rf, 0:S5_PAD] = fr
        h_scr[rf, S5_PAD : 2 * S5_PAD] = fi
        sr, si = s_scr[rf, 0:S5_PAD], s_scr[rf, S5_PAD : 2 * S5_PAD]
        fr, fi = lfr * fr - lfi * fi + sr, lfr * fi + lfi * fr + si
        cb = cps - 1 - c
        rb = slice(cb * nseq, (cb + 1) * nseq)
        h_scr[rb, 2 * S5_PAD : 3 * S5_PAD] = br
        h_scr[rb, 3 * S5_PAD :] = bi
        sr, si = s_scr[rb, 2 * S5_PAD : 3 * S5_PAD], s_scr[rb, 3 * S5_PAD :]
        br, bi = lbr * br - lbi * bi + sr, lbr * bi + lbi * br + si
    y = jnp.dot(ub, wt_ref[0], preferred_element_type=F32)
    y = y + jnp.dot(h_scr[...].astype(BF16), f_ref[0], preferred_element_type=F32)
    y_ref[0] = y + d_ref[0] * u


def _s5_matrices(a_re, a_im, log_step, b_re, b_im, c_re, c_im, d_skip):
    hp = lax.Precision.HIGHEST
    t, g16, p = S5_CHUNK, SSM_GROUP, SSM_STATE
    step = jnp.exp(log_step.astype(F32))[..., None]
    ar, ai = a_re.astype(F32), a_im.astype(F32)
    decay = jnp.exp(ar * step)
    lb_re = decay * jnp.cos(ai * step)
    lb_im = decay * jnp.sin(ai * step)
    nr = lb_re - 1.0
    den = ar * ar + ai * ai
    z_re = (nr * ar + lb_im * ai) / den
    z_im = (lb_im * ar - nr * ai) / den
    br, bi = b_re.astype(F32), b_im.astype(F32)
    bb_re = z_re[..., None] * br - z_im[..., None] * bi
    bb_im = z_re[..., None] * bi + z_im[..., None] * br
    cr, ci = c_re.astype(F32), c_im.astype(F32)
    def pw_step(carry, _):
        pr, pi = carry
        return (pr * lb_re - pi * lb_im, pr * lb_im + pi * lb_re), (pr, pi)
    _, (pw_re, pw_im) = lax.scan(pw_step, (jnp.ones_like(lb_re), jnp.zeros_like(lb_im)), None, length=t + 1)
    pw_re = jnp.moveaxis(pw_re, 0, 2)
    pw_im = jnp.moveaxis(pw_im, 0, 2)
    lbb_re = pw_re[..., None] * bb_re[:, :, None] - pw_im[..., None] * bb_im[:, :, None]
    lbb_im = pw_re[..., None] * bb_im[:, :, None] + pw_im[..., None] * bb_re[:, :, None]
    taps = (jnp.einsum("dgop,dgkpi->dgkoi", cr, lbb_re[:, :, :t], precision=hp)
            - jnp.einsum("dgop,dgkpi->dgkoi", ci, lbb_im[:, :, :t], precision=hp))
    r_idx = jnp.arange(t)[:, None]
    s_idx = jnp.arange(t)[None, :]
    lag_f = s_idx - r_idx
    kf = jnp.where((lag_f >= 0)[None, :, :, None, None], taps[0][:, jnp.clip(lag_f, 0, t - 1)], 0.0)
    kb = jnp.where((lag_f <= 0)[None, :, :, None, None], taps[1][:, jnp.clip(-lag_f, 0, t - 1)], 0.0)
    wt = jnp.transpose(kf + kb, (0, 1, 4, 2, 3)).reshape(N_GROUPS_C, t * g16, t * g16)
    pad = S5_PAD - p

    def e_part(x):
        x = jnp.transpose(x, (0, 1, 3, 2)).reshape(N_GROUPS_C, t * g16, p)
        return jnp.pad(x, ((0, 0), (0, 0), (0, pad)))

    rev = jnp.arange(t - 1, -1, -1)
    e = jnp.concatenate([e_part(lbb_re[0][:, rev]), e_part(lbb_im[0][:, rev]),
                         e_part(lbb_re[1][:, :t]), e_part(lbb_im[1][:, :t])], axis=-1)
    def f_parts(d, pows):
        gr = cr[d][:, None] * pw_re[d][:, pows][:, :, None] - ci[d][:, None] * pw_im[d][:, pows][:, :, None]
        gi = cr[d][:, None] * pw_im[d][:, pows][:, :, None] + ci[d][:, None] * pw_re[d][:, pows][:, :, None]
        def shp(x):
            x = jnp.transpose(x, (0, 3, 1, 2)).reshape(N_GROUPS_C, p, t * g16)
            return jnp.pad(x, ((0, 0), (0, pad), (0, 0)))
        return shp(gr), shp(-gi)

    f_fr, f_fi = f_parts(0, jnp.arange(1, t + 1))
    f_br, f_bi = f_parts(1, jnp.arange(t, 0, -1))
    f = jnp.concatenate([f_fr, f_fi, f_br, f_bi], axis=1)

    def lam_part(x):
        return jnp.pad(x, ((0, 0), (0, pad)))[:, None, :]

    lam = jnp.concatenate([lam_part(pw_re[0][:, t]), lam_part(pw_im[0][:, t]),
                           lam_part(pw_re[1][:, t]), lam_part(pw_im[1][:, t])], axis=-1)
    dvec = jnp.tile(d_skip.astype(F32).reshape(N_GROUPS_C, 1, g16), (1, t, 1)).reshape(N_GROUPS_C, 1, t * g16)
    return wt.astype(BF16), e.astype(BF16), f.astype(BF16), lam, dvec


def _s5(uc, n, l, mats):
    wt, e, f, lam, dvec = mats
    t, g16 = S5_CHUNK, SSM_GROUP
    assert l % t == 0
    cps = l // t
    nc = cps * n
    width = t * g16
    u2 = jnp.transpose(uc.reshape(n, cps, t, N_GROUPS_C, g16), (3, 1, 0, 2, 4)).reshape(N_GROUPS_C, nc, width)
    kern = functools.partial(_s5_kernel, nseq=n, cps=cps)
    per_g = lambda shape: pl.BlockSpec((1,) + shape, lambda g: (g, 0, 0))
    y2 = pl.pallas_call(
        kern,
        grid=(N_GROUPS_C,),
        in_specs=[per_g((nc, width)), per_g((width, width)), per_g((width, 4 * S5_PAD)),
                  per_g((4 * S5_PAD, width)), per_g((1, 4 * S5_PAD)), per_g((1, width))],
        out_specs=per_g((nc, width)),
        out_shape=jax.ShapeDtypeStruct((N_GROUPS_C, nc, width), F32),
        scratch_shapes=[pltpu.VMEM((nc, 4 * S5_PAD), F32), pltpu.VMEM((nc, 4 * S5_PAD), F32)],
        compiler_params=_params(48),
        name="s5_chunked",
    )(u2, wt, e, f, lam, dvec)
    return jnp.transpose(y2.reshape(N_GROUPS_C, cps, n, t, g16), (2, 1, 3, 0, 4)).reshape(n * l, W_C)


def _outproj_kernel(x_ref, oa_ref, ob_ref, yc_ref, w_ref, wg_ref, bg_ref, gc_ref, o_ref):
    y = yc_ref[...]
    g = 0.5 * y * (1.0 + lax.erf(y * (1.0 / math.sqrt(2.0))))
    z = jnp.dot(g.astype(BF16), wg_ref[...], preferred_element_type=F32) + bg_ref[...]
    oc = g * (1.0 / (1.0 + jnp.exp(-z)))
    ocn = _rms(oc, gc_ref[...]).astype(BF16)
    acc = jnp.dot(oa_ref[...], w_ref[0:W_A, :], preferred_element_type=F32)
    acc = acc + jnp.dot(ob_ref[...], w_ref[W_A : W_A + W_B, :], preferred_element_type=F32)
    acc = acc + jnp.dot(ocn, w_ref[W_A + W_B :, :], preferred_element_type=F32)
    o_ref[...] = x_ref[...] + acc


def _outproj(x, oa, ob, yc, w, wg, bg, gc):
    ntok = x.shape[0]
    tm = 512
    row = lambda width: pl.BlockSpec((tm, width), lambda i: (i, 0))
    full = lambda a: pl.BlockSpec(a.shape, lambda i: (0, 0))
    return pl.pallas_call(
        _outproj_kernel,
        grid=(ntok // tm,),
        in_specs=[row(D_MODEL), row(W_A), row(W_B), row(W_C), full(w), full(wg), full(bg), full(gc)],
        out_specs=row(D_MODEL),
        out_shape=jax.ShapeDtypeStruct((ntok, D_MODEL), F32),
        compiler_params=_params(48),
        name="outproj",
    )(x, oa, ob, yc, w, wg, bg, gc)


def _topk_rows(vals, k, iota, sentinel):
    out_v, out_i = [], []
    for _ in range(k):
        m = jnp.max(vals, axis=0, keepdims=True)
        am = jnp.min(jnp.where(vals == m, iota, sentinel), axis=0, keepdims=True)
        out_v.append(m)
        out_i.append(am)
        vals = jnp.where(iota == am, -jnp.inf, vals)
    return jnp.concatenate(out_v, axis=0), jnp.concatenate(out_i, axis=0)


def _route_kernel(x_ref, g_ref, wq_ref, keys_ref, xn_ref, idx_ref, gate_ref):
    tm = x_ref.shape[0]
    xn = _rms(x_ref[...], g_ref[...])
    xn_ref[...] = xn
    q = jnp.dot(xn.astype(BF16), wq_ref[...], preferred_element_type=F32).astype(BF16)
    half = PEER_DKEY // 2
    iota_k = lax.broadcasted_iota(I32, (PEER_KEYS, tm), 0)
    ncand = PEER_TOPK * PEER_TOPK
    iota_c = lax.broadcasted_iota(I32, (ncand, tm), 0)
    for h in range(PEER_HEADS):
        tops = []
        for s in range(2):
            qs = q[:, (2 * h + s) * half : (2 * h + s + 1) * half]
            sc = lax.dot_general(keys_ref[2 * h + s], qs, (((1,), (1,)), ((), ())),
                                 preferred_element_type=F32)
            tops.append(_topk_rows(sc, PEER_TOPK, iota_k, PEER_KEYS))
        (s0, i0), (s1, i1) = tops
        cand = (s0[:, None, :] + s1[None, :, :]).reshape(ncand, tm)
        cidx = (i0[:, None, :] * PEER_KEYS + i1[None, :, :]).reshape(ncand, tm)
        best, pos = _topk_rows(cand, PEER_TOPK, iota_c, ncand)
        experts = [jnp.sum(jnp.where(iota_c == pos[r : r + 1], cidx, 0), axis=0, keepdims=True)
                   for r in range(PEER_TOPK)]
        ex = jnp.exp(best - best[0:1])
        rows = slice(h * PEER_TOPK, (h + 1) * PEER_TOPK)
        idx_ref[rows, :] = jnp.concatenate(experts, axis=0)
        gate_ref[rows, :] = ex / jnp.sum(ex, axis=0, keepdims=True)


def _route(x, g, wq, keys):
    ntok = x.shape[0]
    tm = 256
    return pl.pallas_call(
        _route_kernel,
        grid=(ntok // tm,),
        in_specs=[
            pl.BlockSpec((tm, D_MODEL), lambda i: (i, 0)),
            pl.BlockSpec((1, D_MODEL), lambda i: (0, 0)),
            pl.BlockSpec(wq.shape, lambda i: (0, 0)),
            pl.BlockSpec(keys.shape, lambda i: (0, 0, 0)),
        ],
        out_specs=[
            pl.BlockSpec((tm, D_MODEL), lambda i: (i, 0)),
            pl.BlockSpec((PEER_SEL, tm), lambda i: (0, i)),
            pl.BlockSpec((PEER_SEL, tm), lambda i: (0, i)),
        ],
        out_shape=[
            jax.ShapeDtypeStruct((ntok, D_MODEL), F32),
            jax.ShapeDtypeStruct((PEER_SEL, ntok), I32),
            jax.ShapeDtypeStruct((PEER_SEL, ntok), F32),
        ],
        compiler_params=_params(48),
        name="peer_route",
    )(x, g, wq, keys)


def _pack_table(tab):
    tb = tab.astype(BF16)
    hi = lax.bitcast_convert_type(tb[:, :HALF], jnp.uint16).astype(jnp.uint32)
    lo = lax.bitcast_convert_type(tb[:, HALF:], jnp.uint16).astype(jnp.uint32)
    words = lax.bitcast_convert_type((hi << 16) | lo, I32)
    return words.reshape(tab.shape[0], WORD_ROWS, 128)


def _gather_rows(idx_ref, tbl_ref, s_ref, t):
    for k in range(PEER_SEL):
        e = idx_ref[t * PEER_SEL + k]
        s_ref[k * WORD_ROWS : (k + 1) * WORD_ROWS, :] = tbl_ref[e]


def _unpack_bf16(words):
    hi = pltpu.bitcast(words & jnp.int32(-65536), F32).astype(BF16)
    lo = pltpu.bitcast(words << 16, F32).astype(BF16)
    return hi, lo


def _diag_masks():
    j = lax.broadcasted_iota(I32, (2 * WORD_ROWS, PEER_SEL * WORD_ROWS), 0)
    m = lax.broadcasted_iota(I32, (2 * WORD_ROWS, PEER_SEL * WORD_ROWS), 1)
    r = m % WORD_ROWS
    return (j < WORD_ROWS) & (r == j), (j >= WORD_ROWS) & (r == j - WORD_ROWS)


def _peer_u_kernel(idx_ref, xn_ref, gate_ref, tbl_ref, fold_ref, w_ref, s_ref, act_ref):
    tt = xn_ref.shape[0]
    mask_hi, mask_lo = _diag_masks()

    def token(t, carry):
        _gather_rows(idx_ref, tbl_ref, s_ref, t)
        hi, lo = _unpack_bf16(s_ref[...])
        x8 = xn_ref[t].astype(BF16)
        nt = (((1,), (1,)), ((), ()))
        z = (jnp.where(mask_hi, lax.dot_general(x8, hi, nt, preferred_element_type=F32), 0.0)
             + jnp.where(mask_lo, lax.dot_general(x8, lo, nt, preferred_element_type=F32), 0.0))
        zh = z.astype(BF16)
        zl = (z - zh.astype(F32)).astype(BF16)
        a = (jnp.dot(zh, fold_ref[...], preferred_element_type=F32)
             + jnp.dot(zl, fold_ref[...], preferred_element_type=F32))
        act_ref[pl.ds(t, 1), :] = jnp.sum(a, axis=0, keepdims=True)
        return carry

    lax.fori_loop(0, tt, token, 0)
    a = act_ref[...]
    w_ref[...] = gate_ref[...] * (0.5 * a * (1.0 + lax.erf(a * (1.0 / math.sqrt(2.0)))))


def _peer_v_kernel(idx_ref, w_ref, x_ref, tbl_ref, spread_ref, o_ref, s_ref, wrep_ref):
    tt = w_ref.shape[0]
    mask_hi, mask_lo = _diag_masks()
    wrep_ref[...] = jnp.dot(w_ref[...].astype(BF16), spread_ref[...], preferred_element_type=F32)

    def token(t, carry):
        _gather_rows(idx_ref, tbl_ref, s_ref, t)
        hi, lo = _unpack_bf16(s_ref[...])
        wr = wrep_ref[pl.ds(t, 1), :]
        w_hi = jnp.where(mask_hi, wr, 0.0).astype(BF16)
        w_lo = jnp.where(mask_lo, wr, 0.0).astype(BF16)
        o8 = (jnp.dot(w_hi, hi, preferred_element_type=F32)
              + jnp.dot(w_lo, lo, preferred_element_type=F32))
        o_ref[t] = x_ref[t] + o8
        return carry

    lax.fori_loop(0, tt, token, 0)


PEER_TILE = 64


def _peer_tables_call(kernel, idx_flat, row_inputs, table, const, out_row_shape, scratch, name):
    ntok = idx_flat.shape[0] // PEER_SEL
    tt = PEER_TILE

    def row_spec(a):
        blk = (tt,) + a.shape[1:]
        nd = len(a.shape)
        return pl.BlockSpec(blk, lambda i: (i,) + (0,) * (nd - 1))

    return pl.pallas_call(
        kernel,
        grid=(ntok // tt,),
        in_specs=[pl.BlockSpec((tt * PEER_SEL,), lambda i: (i,), memory_space=pltpu.SMEM)]
        + [row_spec(a) for a in row_inputs]
        + [pl.BlockSpec(table.shape, lambda i: (0, 0, 0), pipeline_mode=pl.Buffered(1)),
           pl.BlockSpec(const.shape, lambda i: (0, 0))],
        out_specs=pl.BlockSpec((tt,) + out_row_shape, lambda i: (i,) + (0,) * len(out_row_shape)),
        out_shape=jax.ShapeDtypeStruct((ntok,) + out_row_shape, F32),
        scratch_shapes=scratch,
        compiler_params=_params(56),
        name=name,
    )(idx_flat, *row_inputs, table, const)


def _peer(x, g, wq, keys, u_packed, v_packed):
    ntok = x.shape[0]
    xn, idx_t, gate_t = _route(x, g, wq, keys)
    idx_flat = jnp.transpose(idx_t).reshape(ntok * PEER_SEL)
    gates = jnp.transpose(gate_t)
    nrows = PEER_SEL * WORD_ROWS
    m = jnp.arange(nrows)
    fold = (m[:, None] // WORD_ROWS == jnp.arange(PEER_SEL)[None, :]).astype(BF16)
    s_scr = pltpu.VMEM((nrows, 128), I32)
    w = _peer_tables_call(
        _peer_u_kernel, idx_flat, [xn.reshape(ntok, 8, 128), gates], u_packed, fold, (PEER_SEL,),
        [s_scr, pltpu.VMEM((PEER_TILE, PEER_SEL), F32)], "peer_gather_u")
    out = _peer_tables_call(
        _peer_v_kernel, idx_flat, [w, x.reshape(ntok, 8, 128)], v_packed, jnp.transpose(fold), (8, 128),
        [s_scr, pltpu.VMEM((PEER_TILE, nrows), F32)], "peer_gather_v")
    return out.reshape(ntok, D_MODEL)


def _final_kernel(x_ref, g_ref, o_ref):
    o_ref[...] = _rms(x_ref[...], g_ref[...])


def _final_norm(x, g):
    ntok = x.shape[0]
    tm = 1024
    return pl.pallas_call(
        _final_kernel,
        grid=(ntok // tm,),
        in_specs=[pl.BlockSpec((tm, D_MODEL), lambda i: (i, 0)), pl.BlockSpec((1, D_MODEL), lambda i: (0, 0))],
        out_specs=pl.BlockSpec((tm, D_MODEL), lambda i: (i, 0)),
        out_shape=jax.ShapeDtypeStruct((ntok, D_MODEL), F32),
        compiler_params=_params(48),
        name="final_norm",
    )(x, g)


def _prepare_layer(p, layer):
    scale = HEAD_DIM ** -0.5
    w_in = p["w_in"][layer]
    qcols = jnp.concatenate([
        jnp.full((W_A,), scale, F32), jnp.ones((2 * W_A,), F32),
        jnp.full((W_B,), scale, F32), jnp.ones((2 * W_B + W_C,), F32)])
    row = lambda a: a.astype(F32).reshape(1, -1)
    return dict(
        norm1=row(p["norm1_g"][layer]),
        w_in=(w_in * qcols[None, :]).astype(BF16),
        na_bias=_na_bias_table(p["rpb_a"][layer]),
        gn_a=row(p["out_norm_a"][layer]),
        gn_b=row(p["out_norm_b"][layer]),
        gn_c=row(p["out_norm_c"][layer]),
        s5=_s5_matrices(p["ssm_a_re"][layer], p["ssm_a_im"][layer], p["ssm_log_step"][layer],
                        p["ssm_b_re"][layer], p["ssm_b_im"][layer], p["ssm_c_re"][layer],
                        p["ssm_c_im"][layer], p["ssm_d"][layer]),
        w_glu=p["w_glu"][layer].astype(BF16),
        b_glu=row(p["b_glu"][layer]),
        w_out=p["w_out"][layer].astype(BF16),
        norm2=row(p["norm2_g"][layer]),
        wq=p["peer_wq"][layer].astype(BF16),
        keys=p["peer_keys"][layer].astype(BF16).reshape(PEER_HEADS * 2, PEER_KEYS, PEER_DKEY // 2),
        u=_pack_table(p["peer_u"][layer]),
        v=_pack_table(p["peer_v"][layer]),
    )


def _trunk(x3, layers, band_biases, final_g):
    n, l, _ = x3.shape
    x = x3.reshape(n * l, D_MODEL)
    for lp in layers:
        a3, b3, uc = _inproj(x, lp["norm1"], lp["w_in"])
        oa = _na_attention(a3.reshape(n, l, 3 * W_A), lp["na_bias"], lp["gn_a"]).reshape(n * l, W_A)
        ob = _dilated_mixture(b3.reshape(n, l, 3 * W_B), band_biases, lp["gn_b"])
        yc = _s5(uc, n, l, lp["s5"])
        x = _outproj(x, oa, ob, yc, lp["w_out"], lp["w_glu"], lp["b_glu"], lp["gn_c"])
        x = _peer(x, lp["norm2"], lp["wq"], lp["keys"], lp["u"], lp["v"])
    return _final_norm(x, final_g).reshape(n, l, D_MODEL)


def kernel(x_prompt, x_sample, norm1_g, w_in, rpb_a, rel_bias, ssm_a_re, ssm_a_im, ssm_log_step, ssm_b_re, ssm_b_im, ssm_c_re, ssm_c_im, ssm_d, w_glu, b_glu, out_norm_a, out_norm_b, out_norm_c, w_out, norm2_g, peer_wq, peer_keys, peer_u, peer_v, final_g):
    p = dict(norm1_g=norm1_g, w_in=w_in, rpb_a=rpb_a, ssm_a_re=ssm_a_re, ssm_a_im=ssm_a_im,
             ssm_log_step=ssm_log_step, ssm_b_re=ssm_b_re, ssm_b_im=ssm_b_im, ssm_c_re=ssm_c_re,
             ssm_c_im=ssm_c_im, ssm_d=ssm_d, w_glu=w_glu, b_glu=b_glu, out_norm_a=out_norm_a,
             out_norm_b=out_norm_b, out_norm_c=out_norm_c, w_out=w_out, norm2_g=norm2_g,
             peer_wq=peer_wq, peer_keys=peer_keys, peer_u=peer_u, peer_v=peer_v)
    layers = [_prepare_layer(p, layer) for layer in range(DEPTH)]
    band_biases = [_band_bias_table(rel_bias, d) for _, d in DILATIONS]
    fg = final_g.astype(F32).reshape(1, D_MODEL)
    return (_trunk(x_prompt, layers, band_biases, fg), _trunk(x_sample, layers, band_biases, fg))
```

```python
import functools
import math

import jax
import jax.numpy as jnp
from jax import lax
from jax.experimental import pallas as pl
from jax.experimental.pallas import tpu as pltpu

F32 = jnp.float32
BF16 = jnp.bfloat16
I32 = jnp.int32

D_MODEL = 1024
DEPTH = 4
GRID_W = 64
HEAD_DIM = 64
N_HEADS_A = 4
W_A = N_HEADS_A * HEAD_DIM
NA_ROWS = 8
NA_COLS = 16
N_HEADS_B = 8
W_B = N_HEADS_B * HEAD_DIM
DILATIONS = ((128, 1), (512, 4), (2048, 16))
SSM_GROUP = 16
W_C = 256
N_GROUPS_C = W_C // SSM_GROUP
SSM_STATE = 64
NUM_BUCKETS = 32
MAX_DISTANCE = 1024
PEER_HEADS = 8
PEER_KEYS = 128
PEER_EXPERTS = PEER_KEYS * PEER_KEYS
PEER_TOPK = 16
PEER_DKEY = 256
PEER_SEL = PEER_HEADS * PEER_TOPK
EPS = 1e-6
NEG = -1e30

BAND = 64
S5_CHUNK = 64
S5_PAD = 128
HALF = D_MODEL // 2
WORD_ROWS = HALF // 128

MIB = 1024 * 1024


def _params(vmem_mib, n_axes=1):
    return pltpu.CompilerParams(
        vmem_limit_bytes=vmem_mib * MIB,
        dimension_semantics=("arbitrary",) * n_axes,
    )


def _rms(x, g):
    return x * lax.rsqrt(jnp.mean(x * x, axis=-1, keepdims=True) + EPS) * g


def _inproj_kernel(x_ref, g_ref, w_ref, a_ref, b_ref, c_ref):
    h = _rms(x_ref[...], g_ref[...]).astype(BF16)
    p = jnp.dot(h, w_ref[...], preferred_element_type=F32)
    a_ref[...] = p[:, : 3 * W_A].astype(BF16)
    b_ref[...] = p[:, 3 * W_A : 3 * W_A + 3 * W_B].astype(BF16)
    c_ref[...] = p[:, 3 * W_A + 3 * W_B :]


def _inproj(x, g, w):
    ntok = x.shape[0]
    tm = 512
    wcols = w.shape[1]
    return pl.pallas_call(
        _inproj_kernel,
        grid=(ntok // tm,),
        in_specs=[
            pl.BlockSpec((tm, D_MODEL), lambda i: (i, 0)),
            pl.BlockSpec((1, D_MODEL), lambda i: (0, 0)),
            pl.BlockSpec((D_MODEL, wcols), lambda i: (0, 0)),
        ],
        out_specs=[
            pl.BlockSpec((tm, 3 * W_A), lambda i: (i, 0)),
            pl.BlockSpec((tm, 3 * W_B), lambda i: (i, 0)),
            pl.BlockSpec((tm, W_C), lambda i: (i, 0)),
        ],
        out_shape=[
            jax.ShapeDtypeStruct((ntok, 3 * W_A), BF16),
            jax.ShapeDtypeStruct((ntok, 3 * W_B), BF16),
            jax.ShapeDtypeStruct((ntok, W_C), F32),
        ],
        compiler_params=_params(48),
        name="inproj",
    )(x, g, w)


def _na_kernel(q_ref, k_ref, v_ref, b_ref, g_ref, o_ref, *, rows, rows_per_tile):
    i = pl.program_id(1)
    win = NA_ROWS * GRID_W
    for j in range(rows_per_tile):
        r = i * rows_per_tile + j
        rs = jnp.clip(r - NA_ROWS // 2, 0, rows - NA_ROWS)
        base = rs - r + (NA_ROWS - 1)
        start = pl.multiple_of(rs * GRID_W, GRID_W)
        outs = []
        for h in range(N_HEADS_A):
            cols = slice(h * HEAD_DIM, (h + 1) * HEAD_DIM)
            q = q_ref[0, j * GRID_W : (j + 1) * GRID_W, cols]
            k = k_ref[0, pl.ds(start, win), cols]
            v = v_ref[0, pl.ds(start, win), cols]
            s = lax.dot_general(q, k, (((1,), (1,)), ((), ())), preferred_element_type=F32)
            s = s + b_ref[h, base]
            m = jnp.max(s, axis=-1, keepdims=True)
            p = jnp.exp(s - m)
            l = jnp.sum(p, axis=-1, keepdims=True)
            outs.append(jnp.dot(p.astype(BF16), v, preferred_element_type=F32) / l)
        o = _rms(jnp.concatenate(outs, axis=-1), g_ref[...])
        o_ref[0, j * GRID_W : (j + 1) * GRID_W, :] = o.astype(BF16)


def _na_bias_table(rpb):
    c = jnp.arange(GRID_W)
    col_start = jnp.clip(c - NA_COLS // 2, 0, GRID_W - NA_COLS)
    col_ok = (c[None, :] >= col_start[:, None]) & (c[None, :] < col_start[:, None] + NA_COLS)
    col_off = jnp.clip(c[None, :] - c[:, None], -(NA_COLS - 1), NA_COLS - 1) + (NA_COLS - 1)
    t = rpb[:, :, col_off]
    t = jnp.where(col_ok[None, None], t, NEG)
    ro = jnp.arange(NA_ROWS)[:, None] + jnp.arange(NA_ROWS)[None, :]
    t = t[:, ro]
    t = jnp.transpose(t, (0, 1, 3, 2, 4))
    return t.reshape(N_HEADS_A, NA_ROWS, GRID_W, NA_ROWS * GRID_W).astype(F32)


def _na_attention(a3, bias, g):
    n, l, _ = a3.shape
    rows = l // GRID_W
    assert rows >= NA_ROWS and l % GRID_W == 0
    rows_per_tile = 8
    tq = rows_per_tile * GRID_W
    kern = functools.partial(_na_kernel, rows=rows, rows_per_tile=rows_per_tile)
    return pl.pallas_call(
        kern,
        grid=(n, l // tq),
        in_specs=[
            pl.BlockSpec((1, tq, W_A), lambda b, i: (b, i, 0)),
            pl.BlockSpec((1, l, W_A), lambda b, i: (b, 0, 1)),
            pl.BlockSpec((1, l, W_A), lambda b, i: (b, 0, 2)),
            pl.BlockSpec(bias.shape, lambda b, i: (0, 0, 0, 0)),
            pl.BlockSpec((1, W_A), lambda b, i: (0, 0)),
        ],
        out_specs=pl.BlockSpec((1, tq, W_A), lambda b, i: (b, i, 0)),
        out_shape=jax.ShapeDtypeStruct((n, l, W_A), BF16),
        compiler_params=_params(48, 2),
        name="na_attention",
    )(a3, a3, a3, bias, g)


def _band_kernel(q_ref, k_ref, v_ref, kp_ref, vp_ref, kn_ref, vn_ref, b_ref, o_ref, lse_ref, *, n_tiles):
    i = pl.program_id(1)
    tq = q_ref.shape[1]
    kwid = tq + 2 * BAND
    col = lax.broadcasted_iota(I32, (1, kwid), 1)
    lo = jnp.where(i == 0, BAND, 0)
    hi = jnp.where(i == n_tiles - 1, tq + BAND, kwid)
    edge = jnp.where((col >= lo) & (col < hi), 0.0, NEG)
    pair = 2 * HEAD_DIM
    first = lax.broadcasted_iota(I32, (1, pair), 1) < HEAD_DIM
    nt = (((1,), (1,)), ((), ()))
    for p in range(N_HEADS_B // 2):
        cols = slice(p * pair, (p + 1) * pair)
        q2 = q_ref[0, :, cols]
        kw = jnp.concatenate([kp_ref[0, :, cols], k_ref[0, :, cols], kn_ref[0, :, cols]], axis=0)
        vw = jnp.concatenate([vp_ref[0, :, cols], v_ref[0, :, cols], vn_ref[0, :, cols]], axis=0)
        outs, lses = [], []
        for half in range(2):
            qm = jnp.where(first if half == 0 else jnp.logical_not(first), q2, jnp.zeros_like(q2))
            s = lax.dot_general(qm, kw, nt, preferred_element_type=F32) + b_ref[2 * p + half] + edge
            m = jnp.max(s, axis=-1, keepdims=True)
            e = jnp.exp(s - m)
            l = jnp.sum(e, axis=-1, keepdims=True)
            outs.append(jnp.dot(e.astype(BF16), vw, preferred_element_type=F32) / l)
            lses.append(m + jnp.log(l))
        o_ref[0, :, cols] = jnp.where(first, outs[0], outs[1])
        lse_ref[0, :, cols] = jnp.where(first, lses[0], lses[1])


def _t5_bucket(rel):
    nb = NUM_BUCKETS // 2
    ret = jnp.where(rel > 0, nb, 0)
    n = jnp.abs(rel)
    max_exact = nb // 2
    nf = jnp.maximum(n, 1).astype(F32)
    large = max_exact + (jnp.log(nf / max_exact) / math.log(MAX_DISTANCE / max_exact)
                         * (nb - max_exact)).astype(I32)
    large = jnp.minimum(large, nb - 1)
    return ret + jnp.where(n < max_exact, n, large)


BAND_BLOCKS_PER_TILE = 4


def _band_tile(ls):
    assert ls % BAND == 0
    return min(BAND_BLOCKS_PER_TILE, ls // BAND) * BAND


def _band_bias_table(rel_bias, dilation, tq):
    qi = jnp.arange(tq)
    ki = jnp.arange(tq + 2 * BAND) - BAND
    rel = ki[None, :] - qi[:, None]
    inside = jnp.abs(rel) <= BAND
    bias = jnp.transpose(rel_bias[_t5_bucket(jnp.where(inside, rel, 0) * dilation)], (2, 0, 1)).astype(F32)
    return jnp.where(inside[None], bias, NEG)


def _band_attention(q, k, v, bias):
    n, ls, _ = q.shape
    tq = _band_tile(ls)
    blocks_per_tile = tq // BAND
    n_blocks = ls // BAND
    assert ls % tq == 0 and bias.shape == (N_HEADS_B, tq, tq + 2 * BAND)
    kern = functools.partial(_band_kernel, n_tiles=ls // tq)
    cur = pl.BlockSpec((1, tq, W_B), lambda b, i: (b, i, 0))
    prev = pl.BlockSpec((1, BAND, W_B), lambda b, i: (b, jnp.maximum(i * blocks_per_tile - 1, 0), 0))
    nxt = pl.BlockSpec((1, BAND, W_B), lambda b, i: (b, jnp.minimum((i + 1) * blocks_per_tile, n_blocks - 1), 0))
    return pl.pallas_call(
        kern,
        grid=(n, ls // tq),
        in_specs=[cur, cur, cur, prev, prev, nxt, nxt, pl.BlockSpec(bias.shape, lambda b, i: (0, 0, 0))],
        out_specs=[cur, cur],
        out_shape=[jax.ShapeDtypeStruct((n, ls, W_B), F32), jax.ShapeDtypeStruct((n, ls, W_B), F32)],
        compiler_params=_params(48, 2),
        name="band_attention",
    )(q, k, v, k, v, k, v, bias)


def _merge_kernel(o1, o2, o3, l1, l2, l3, g_ref, out_ref):
    a, b, c = l1[...], l2[...], l3[...]
    m = jnp.maximum(jnp.maximum(a, b), c)
    ea, eb, ec = jnp.exp(a - m), jnp.exp(b - m), jnp.exp(c - m)
    den = ea + eb + ec
    o = (ea / den) * o1[...] + (eb / den) * o2[...] + (ec / den) * o3[...]
    out_ref[...] = _rms(o, g_ref[...]).astype(BF16)


def _merge(os_, ls_, g):
    ntok = os_[0].shape[0]
    tm = 512
    blk = pl.BlockSpec((tm, W_B), lambda i: (i, 0))
    return pl.pallas_call(
        _merge_kernel,
        grid=(ntok // tm,),
        in_specs=[blk] * 6 + [pl.BlockSpec((1, W_B), lambda i: (0, 0))],
        out_specs=blk,
        out_shape=jax.ShapeDtypeStruct((ntok, W_B), BF16),
        compiler_params=_params(48),
        name="dilated_merge",
    )(*os_, *ls_, g)


def _dilated_mixture(b3, rel_bias, g):
    n, l, _ = b3.shape
    q, k, v = b3[..., :W_B], b3[..., W_B : 2 * W_B], b3[..., 2 * W_B :]
    outs, lses = [], []
    for window, d in DILATIONS:
        assert window // (2 * d) == BAND and l % d == 0
        ls = l // d
        bias = _band_bias_table(rel_bias, d, _band_tile(ls))

        def to_sub(a):
            return jnp.transpose(a.reshape(n, ls, d, W_B), (0, 2, 1, 3)).reshape(n * d, ls, W_B)

        def from_sub(a):
            return jnp.transpose(a.reshape(n, d, ls, W_B), (0, 2, 1, 3)).reshape(n * l, W_B)

        o, lse = _band_attention(to_sub(q), to_sub(k), to_sub(v), bias)
        outs.append(from_sub(o))
        lses.append(from_sub(lse))
    return _merge(outs, lses, g)


def _s5_kernel(u_ref, wt_ref, e_ref, f_ref, lam_ref, d_ref, y_ref, s_scr, h_scr, *, nseq, cps):
    u = u_ref[0]
    ub = u.astype(BF16)
    s_scr[...] = jnp.dot(ub, e_ref[0], preferred_element_type=F32)
    lam = lam_ref[0]
    lfr, lfi = lam[:, 0:S5_PAD], lam[:, S5_PAD : 2 * S5_PAD]
    lbr, lbi = lam[:, 2 * S5_PAD : 3 * S5_PAD], lam[:, 3 * S5_PAD :]
    zero = jnp.zeros((nseq, S5_PAD), F32)
    fr, fi, br, bi = zero, zero, zero, zero
    for c in range(cps):
        rf = slice(c * nseq, (c + 1) * nseq)
        h_scr[rf, 0:S5_PAD] = fr
        h_scr[rf, S5_PAD : 2 * S5_PAD] = fi
        sr, si = s_scr[rf, 0:S5_PAD], s_scr[rf, S5_PAD : 2 * S5_PAD]
        fr, fi = lfr * fr - lfi * fi + sr, lfr * fi + lfi * fr + si
        cb = cps - 1 - c
        rb = slice(cb * nseq, (cb + 1) * nseq)
        h_scr[rb, 2 * S5_PAD : 3 * S5_PAD] = br
        h_scr[rb, 3 * S5_PAD :] = bi
        sr, si = s_scr[rb, 2 * S5_PAD : 3 * S5_PAD], s_scr[rb, 3 * S5_PAD :]
        br, bi = lbr * br - lbi * bi + sr, lbr * bi + lbi * br + si
    y = jnp.dot(ub, wt_ref[0], preferred_element_type=F32)
    y = y + jnp.dot(h_scr[...].astype(BF16), f_ref[0], preferred_element_type=F32)
    y_ref[0] = y + d_ref[0] * u


def _s5_matrices(a_re, a_im, log_step, b_re, b_im, c_re, c_im, d_skip):
    hp = lax.Precision.HIGHEST
    t, g16, p = S5_CHUNK, SSM_GROUP, SSM_STATE
    step = jnp.exp(log_step.astype(F32))[..., None]
    ar, ai = a_re.astype(F32), a_im.astype(F32)
    decay = jnp.exp(ar * step)
    lb_re = decay * jnp.cos(ai * step)
    lb_im = decay * jnp.sin(ai * step)
    nr = lb_re - 1.0
    den = ar * ar + ai * ai
    z_re = (nr * ar + lb_im * ai) / den
    z_im = (lb_im * ar - nr * ai) / den
    br, bi = b_re.astype(F32), b_im.astype(F32)
    bb_re = z_re[..., None] * br - z_im[..., None] * bi
    bb_im = z_re[..., None] * bi + z_im[..., None] * br
    cr, ci = c_re.astype(F32), c_im.astype(F32)
    def pw_step(carry, _):
        pr, pi = carry
        return (pr * lb_re - pi * lb_im, pr * lb_im + pi * lb_re), (pr, pi)
    _, (pw_re, pw_im) = lax.scan(pw_step, (jnp.ones_like(lb_re), jnp.zeros_like(lb_im)), None, length=t + 1)
    pw_re = jnp.moveaxis(pw_re, 0, 2)
    pw_im = jnp.moveaxis(pw_im, 0, 2)
    lbb_re = pw_re[..., None] * bb_re[:, :, None] - pw_im[..., None] * bb_im[:, :, None]
    lbb_im = pw_re[..., None] * bb_im[:, :, None] + pw_im[..., None] * bb_re[:, :, None]
    taps = (jnp.einsum("dgop,dgkpi->dgkoi", cr, lbb_re[:, :, :t], precision=hp)
            - jnp.einsum("dgop,dgkpi->dgkoi", ci, lbb_im[:, :, :t], precision=hp))
    r_idx = jnp.arange(t)[:, None]
    s_idx = jnp.arange(t)[None, :]
    lag_f = s_idx - r_idx
    kf = jnp.where((lag_f >= 0)[None, :, :, None, None], taps[0][:, jnp.clip(lag_f, 0, t - 1)], 0.0)
    kb = jnp.where((lag_f <= 0)[None, :, :, None, None], taps[1][:, jnp.clip(-lag_f, 0, t - 1)], 0.0)
    wt = jnp.transpose(kf + kb, (0, 1, 4, 2, 3)).reshape(N_GROUPS_C, t * g16, t * g16)
    pad = S5_PAD - p

    def e_part(x):
        x = jnp.transpose(x, (0, 1, 3, 2)).reshape(N_GROUPS_C, t * g16, p)
        return jnp.pad(x, ((0, 0), (0, 0), (0, pad)))

    rev = jnp.arange(t - 1, -1, -1)
    e = jnp.concatenate([e_part(lbb_re[0][:, rev]), e_part(lbb_im[0][:, rev]),
                         e_part(lbb_re[1][:, :t]), e_part(lbb_im[1][:, :t])], axis=-1)
    def f_parts(d, pows):
        gr = cr[d][:, None] * pw_re[d][:, pows][:, :, None] - ci[d][:, None] * pw_im[d][:, pows][:, :, None]
        gi = cr[d][:, None] * pw_im[d][:, pows][:, :, None] + ci[d][:, None] * pw_re[d][:, pows][:, :, None]
        def shp(x):
            x = jnp.transpose(x, (0, 3, 1, 2)).reshape(N_GROUPS_C, p, t * g16)
            return jnp.pad(x, ((0, 0), (0, pad), (0, 0)))
        return shp(gr), shp(-gi)

    f_fr, f_fi = f_parts(0, jnp.arange(1, t + 1))
    f_br, f_bi = f_parts(1, jnp.arange(t, 0, -1))
    f = jnp.concatenate([f_fr, f_fi, f_br, f_bi], axis=1)

    def lam_part(x):
        return jnp.pad(x, ((0, 0), (0, pad)))[:, None, :]

    lam = jnp.concatenate([lam_part(pw_re[0][:, t]), lam_part(pw_im[0][:, t]),
                           lam_part(pw_re[1][:, t]), lam_part(pw_im[1][:, t])], axis=-1)
    dvec = jnp.tile(d_skip.astype(F32).reshape(N_GROUPS_C, 1, g16), (1, t, 1)).reshape(N_GROUPS_C, 1, t * g16)
    return wt.astype(BF16), e.astype(BF16), f.astype(BF16), lam, dvec


def _s5(uc, n, l, mats):
    wt, e, f, lam, dvec = mats
    t, g16 = S5_CHUNK, SSM_GROUP
    assert l % t == 0
    cps = l // t
    nc = cps * n
    width = t * g16
    u2 = jnp.transpose(uc.reshape(n, cps, t, N_GROUPS_C, g16), (3, 1, 0, 2, 4)).reshape(N_GROUPS_C, nc, width)
    kern = functools.partial(_s5_kernel, nseq=n, cps=cps)
    per_g = lambda shape: pl.BlockSpec((1,) + shape, lambda g: (g, 0, 0))
    y2 = pl.pallas_call(
        kern,
        grid=(N_GROUPS_C,),
        in_specs=[per_g((nc, width)), per_g((width, width)), per_g((width, 4 * S5_PAD)),
                  per_g((4 * S5_PAD, width)), per_g((1, 4 * S5_PAD)), per_g((1, width))],
        out_specs=per_g((nc, width)),
        out_shape=jax.ShapeDtypeStruct((N_GROUPS_C, nc, width), F32),
        scratch_shapes=[pltpu.VMEM((nc, 4 * S5_PAD), F32), pltpu.VMEM((nc, 4 * S5_PAD), F32)],
        compiler_params=_params(48),
        name="s5_chunked",
    )(u2, wt, e, f, lam, dvec)
    return jnp.transpose(y2.reshape(N_GROUPS_C, cps, n, t, g16), (2, 1, 3, 0, 4)).reshape(n * l, W_C)


def _outproj_kernel(x_ref, oa_ref, ob_ref, yc_ref, w_ref, wg_ref, bg_ref, gc_ref, o_ref):
    y = yc_ref[...]
    g = 0.5 * y * (1.0 + lax.erf(y * (1.0 / math.sqrt(2.0))))
    z = jnp.dot(g.astype(BF16), wg_ref[...], preferred_element_type=F32) + bg_ref[...]
    oc = g * (1.0 / (1.0 + jnp.exp(-z)))
    ocn = _rms(oc, gc_ref[...]).astype(BF16)
    acc = jnp.dot(oa_ref[...], w_ref[0:W_A, :], preferred_element_type=F32)
    acc = acc + jnp.dot(ob_ref[...], w_ref[W_A : W_A + W_B, :], preferred_element_type=F32)
    acc = acc + jnp.dot(ocn, w_ref[W_A + W_B :, :], preferred_element_type=F32)
    o_ref[...] = x_ref[...] + acc


def _outproj(x, oa, ob, yc, w, wg, bg, gc):
    ntok = x.shape[0]
    tm = 512
    row = lambda width: pl.BlockSpec((tm, width), lambda i: (i, 0))
    full = lambda a: pl.BlockSpec(a.shape, lambda i: (0, 0))
    return pl.pallas_call(
        _outproj_kernel,
        grid=(ntok // tm,),
        in_specs=[row(D_MODEL), row(W_A), row(W_B), row(W_C), full(w), full(wg), full(bg), full(gc)],
        out_specs=row(D_MODEL),
        out_shape=jax.ShapeDtypeStruct((ntok, D_MODEL), F32),
        compiler_params=_params(48),
        name="outproj",
    )(x, oa, ob, yc, w, wg, bg, gc)


def _topk_rows(vals, k, iota, sentinel):
    out_v, out_i = [], []
    for _ in range(k):
        m = jnp.max(vals, axis=0, keepdims=True)
        am = jnp.min(jnp.where(vals == m, iota, sentinel), axis=0, keepdims=True)
        out_v.append(m)
        out_i.append(am)
        vals = jnp.where(iota == am, -jnp.inf, vals)
    return jnp.concatenate(out_v, axis=0), jnp.concatenate(out_i, axis=0)


_CAND_BLOCKS = ([(0, 1, 0, PEER_TOPK)] + [(a, a + 1, 0, 8) for a in range(1, 8)] + [(8, PEER_TOPK, 0, 1)])


def _candidate_positions(tm):
    cols = []
    for a0, a1, b0, b1 in _CAND_BLOCKS:
        cols += [a * PEER_TOPK + b for a in range(a0, a1) for b in range(b0, b1)]
    covered = set(cols)
    assert all(a * PEER_TOPK + b in covered for a in range(PEER_TOPK) for b in range(PEER_TOPK)
               if (a + 1) * (b + 1) <= PEER_TOPK)
    return jnp.broadcast_to(jnp.asarray(cols, I32)[:, None], (len(cols), tm))


def _candidate_rows(first, second, combine):
    return jnp.concatenate([combine(first[a0:a1], second[b0:b1]) for a0, a1, b0, b1 in _CAND_BLOCKS], axis=0)


def _route_kernel(x_ref, g_ref, wq_ref, keys_ref, pos_ref, xn_ref, idx_ref, gate_ref):
    tm = x_ref.shape[0]
    xn = _rms(x_ref[...], g_ref[...])
    xn_ref[...] = xn
    q = jnp.dot(xn.astype(BF16), wq_ref[...], preferred_element_type=F32).astype(BF16)
    half = PEER_DKEY // 2
    iota_k = lax.broadcasted_iota(I32, (PEER_KEYS, tm), 0)
    cpos = pos_ref[...]
    for h in range(PEER_HEADS):
        tops = []
        for s in range(2):
            qs = q[:, (2 * h + s) * half : (2 * h + s + 1) * half]
            sc = lax.dot_general(keys_ref[2 * h + s], qs, (((1,), (1,)), ((), ())),
                                 preferred_element_type=F32)
            tops.append(_topk_rows(sc, PEER_TOPK, iota_k, PEER_KEYS))
        (s0, i0), (s1, i1) = tops
        cand = _candidate_rows(s0, s1, lambda a, b: a + b)
        cidx = _candidate_rows(i0, i1, lambda a, b: a * PEER_KEYS + b)
        best, pos = _topk_rows(cand, PEER_TOPK, cpos, PEER_TOPK * PEER_TOPK)
        experts = [jnp.sum(jnp.where(cpos == pos[r : r + 1], cidx, 0), axis=0, keepdims=True)
                   for r in range(PEER_TOPK)]
        ex = jnp.exp(best - best[0:1])
        rows = slice(h * PEER_TOPK, (h + 1) * PEER_TOPK)
        idx_ref[rows, :] = jnp.concatenate(experts, axis=0)
        gate_ref[rows, :] = ex / jnp.sum(ex, axis=0, keepdims=True)


def _route(x, g, wq, keys):
    ntok = x.shape[0]
    tm = 256
    cpos = _candidate_positions(tm)
    return pl.pallas_call(
        _route_kernel,
        grid=(ntok // tm,),
        in_specs=[
            pl.BlockSpec((tm, D_MODEL), lambda i: (i, 0)),
            pl.BlockSpec((1, D_MODEL), lambda i: (0, 0)),
            pl.BlockSpec(wq.shape, lambda i: (0, 0)),
            pl.BlockSpec(keys.shape, lambda i: (0, 0, 0)),
            pl.BlockSpec(cpos.shape, lambda i: (0, 0)),
        ],
        out_specs=[
            pl.BlockSpec((tm, D_MODEL), lambda i: (i, 0)),
            pl.BlockSpec((PEER_SEL, tm), lambda i: (0, i)),
            pl.BlockSpec((PEER_SEL, tm), lambda i: (0, i)),
        ],
        out_shape=[
            jax.ShapeDtypeStruct((ntok, D_MODEL), F32),
            jax.ShapeDtypeStruct((PEER_SEL, ntok), I32),
            jax.ShapeDtypeStruct((PEER_SEL, ntok), F32),
        ],
        compiler_params=_params(48),
        name="peer_route",
    )(x, g, wq, keys, cpos)


def _pack_table(tab):
    tb = tab.astype(BF16)
    hi = lax.bitcast_convert_type(tb[:, :HALF], jnp.uint16).astype(jnp.uint32)
    lo = lax.bitcast_convert_type(tb[:, HALF:], jnp.uint16).astype(jnp.uint32)
    words = lax.bitcast_convert_type((hi << 16) | lo, I32)
    return words.reshape(tab.shape[0], WORD_ROWS, 128)


def _gather_rows(idx_ref, tbl_ref, s_ref, t):
    for k in range(PEER_SEL):
        e = idx_ref[t * PEER_SEL + k]
        s_ref[k * WORD_ROWS : (k + 1) * WORD_ROWS, :] = tbl_ref[e]


def _for_each_token(idx_ref, tbl_ref, bufs, tt, compute):
    nb = len(bufs)
    for b in range(nb - 1):
        _gather_rows(idx_ref, tbl_ref, bufs[b], b)

    def trip(p, carry):
        t = nb * p
        for b in range(nb):
            ahead = jnp.minimum(t + b + nb - 1, tt - 1)
            _gather_rows(idx_ref, tbl_ref, bufs[(b + nb - 1) % nb], ahead)
            compute(bufs[b], t + b)
        return carry

    lax.fori_loop(0, tt // nb, trip, 0)


def _unpack_bf16(words):
    hi = pltpu.bitcast(words & jnp.int32(-65536), F32).astype(BF16)
    lo = pltpu.bitcast(words << 16, F32).astype(BF16)
    return hi, lo


def _diag_masks():
    j = lax.broadcasted_iota(I32, (2 * WORD_ROWS, PEER_SEL * WORD_ROWS), 0)
    m = lax.broadcasted_iota(I32, (2 * WORD_ROWS, PEER_SEL * WORD_ROWS), 1)
    r = m % WORD_ROWS
    return (j < WORD_ROWS) & (r == j), (j >= WORD_ROWS) & (r == j - WORD_ROWS)


PEER_TILE = 128
GATHER_BUFFERS = 4
GATHER_SPLIT = 2


def _peer_u_kernel(idx_ref, xn_ref, gate_ref, tbl_ref, fold_ref, w_ref, *scratch):
    bufs, part_ref = scratch[:-1], scratch[-1]
    tt = xn_ref.shape[0]
    mask_hi, mask_lo = _diag_masks()
    nt = (((1,), (1,)), ((), ()))
    chunk = PEER_SEL * WORD_ROWS // GATHER_SPLIT

    def compute(s_ref, t):
        x8 = xn_ref[t].astype(BF16)
        parts = []
        for c in range(GATHER_SPLIT):
            rows = slice(c * chunk, (c + 1) * chunk)
            hi, lo = _unpack_bf16(s_ref[rows, :])
            z = (jnp.where(mask_hi[:, rows], lax.dot_general(x8, hi, nt, preferred_element_type=F32), 0.0)
                 + jnp.where(mask_lo[:, rows], lax.dot_general(x8, lo, nt, preferred_element_type=F32), 0.0))
            parts.append(jnp.sum(z, axis=0, keepdims=True))
        part_ref[pl.ds(t, 1), :] = jnp.concatenate(parts, axis=-1)

    _for_each_token(idx_ref, tbl_ref, bufs, tt, compute)
    v = part_ref[...]
    vh = v.astype(BF16)
    vl = (v - vh.astype(F32)).astype(BF16)
    a = (jnp.dot(vh, fold_ref[...], preferred_element_type=F32)
         + jnp.dot(vl, fold_ref[...], preferred_element_type=F32))
    w_ref[...] = gate_ref[...] * (0.5 * a * (1.0 + lax.erf(a * (1.0 / math.sqrt(2.0)))))


def _peer_v_kernel(idx_ref, w_ref, x_ref, tbl_ref, spread_ref, o_ref, *scratch):
    bufs, wrep_ref = scratch[:-1], scratch[-1]
    tt = w_ref.shape[0]
    mask_hi, mask_lo = _diag_masks()
    chunk = PEER_SEL * WORD_ROWS // GATHER_SPLIT
    wrep_ref[...] = jnp.dot(w_ref[...].astype(BF16), spread_ref[...], preferred_element_type=F32)

    def compute(s_ref, t):
        wr = wrep_ref[pl.ds(t, 1), :]
        w_hi = jnp.where(mask_hi, wr, 0.0).astype(BF16)
        w_lo = jnp.where(mask_lo, wr, 0.0).astype(BF16)
        o8 = x_ref[t]
        for c in range(GATHER_SPLIT):
            rows = slice(c * chunk, (c + 1) * chunk)
            hi, lo = _unpack_bf16(s_ref[rows, :])
            o8 = (o8 + jnp.dot(w_hi[:, rows], hi, preferred_element_type=F32)
                  + jnp.dot(w_lo[:, rows], lo, preferred_element_type=F32))
        o_ref[t] = o8

    _for_each_token(idx_ref, tbl_ref, bufs, tt, compute)


def _peer_tables_call(kernel, idx_flat, row_inputs, table, const, out_row_shape, scratch, name):
    ntok = idx_flat.shape[0] // PEER_SEL
    tt = PEER_TILE

    def row_spec(a):
        blk = (tt,) + a.shape[1:]
        nd = len(a.shape)
        return pl.BlockSpec(blk, lambda i: (i,) + (0,) * (nd - 1))

    return pl.pallas_call(
        kernel,
        grid=(ntok // tt,),
        in_specs=[pl.BlockSpec((tt * PEER_SEL,), lambda i: (i,), memory_space=pltpu.SMEM)]
        + [row_spec(a) for a in row_inputs]
        + [pl.BlockSpec(table.shape, lambda i: (0, 0, 0), pipeline_mode=pl.Buffered(1)),
           pl.BlockSpec(const.shape, lambda i: (0, 0))],
        out_specs=pl.BlockSpec((tt,) + out_row_shape, lambda i: (i,) + (0,) * len(out_row_shape)),
        out_shape=jax.ShapeDtypeStruct((ntok,) + out_row_shape, F32),
        scratch_shapes=scratch,
        compiler_params=_params(56),
        name=name,
    )(idx_flat, *row_inputs, table, const)


def _peer(x, g, wq, keys, u_packed, v_packed):
    ntok = x.shape[0]
    xn, idx_t, gate_t = _route(x, g, wq, keys)
    idx_flat = jnp.transpose(idx_t).reshape(ntok * PEER_SEL)
    gates = jnp.transpose(gate_t)
    nrows = PEER_SEL * WORD_ROWS
    m = jnp.arange(nrows)
    fold = (m[:, None] // WORD_ROWS == jnp.arange(PEER_SEL)[None, :]).astype(BF16)
    scratch = [pltpu.VMEM((nrows, 128), I32)] * GATHER_BUFFERS + [pltpu.VMEM((PEER_TILE, nrows), F32)]
    w = _peer_tables_call(
        _peer_u_kernel, idx_flat, [xn.reshape(ntok, 8, 128), gates], u_packed, fold, (PEER_SEL,),
        scratch, "peer_gather_u")
    out = _peer_tables_call(
        _peer_v_kernel, idx_flat, [w, x.reshape(ntok, 8, 128)], v_packed, jnp.transpose(fold), (8, 128),
        scratch, "peer_gather_v")
    return out.reshape(ntok, D_MODEL)


def _final_kernel(x_ref, g_ref, o_ref):
    o_ref[...] = _rms(x_ref[...], g_ref[...])


def _final_norm(x, g):
    ntok = x.shape[0]
    tm = 1024
    return pl.pallas_call(
        _final_kernel,
        grid=(ntok // tm,),
        in_specs=[pl.BlockSpec((tm, D_MODEL), lambda i: (i, 0)), pl.BlockSpec((1, D_MODEL), lambda i: (0, 0))],
        out_specs=pl.BlockSpec((tm, D_MODEL), lambda i: (i, 0)),
        out_shape=jax.ShapeDtypeStruct((ntok, D_MODEL), F32),
        compiler_params=_params(48),
        name="final_norm",
    )(x, g)


def _prepare_layer(p, layer):
    scale = HEAD_DIM ** -0.5
    w_in = p["w_in"][layer]
    qcols = jnp.concatenate([
        jnp.full((W_A,), scale, F32), jnp.ones((2 * W_A,), F32),
        jnp.full((W_B,), scale, F32), jnp.ones((2 * W_B + W_C,), F32)])
    row = lambda a: a.astype(F32).reshape(1, -1)
    return dict(
        norm1=row(p["norm1_g"][layer]),
        w_in=(w_in * qcols[None, :]).astype(BF16),
        na_bias=_na_bias_table(p["rpb_a"][layer]),
        gn_a=row(p["out_norm_a"][layer]),
        gn_b=row(p["out_norm_b"][layer]),
        gn_c=row(p["out_norm_c"][layer]),
        s5=_s5_matrices(p["ssm_a_re"][layer], p["ssm_a_im"][layer], p["ssm_log_step"][layer],
                        p["ssm_b_re"][layer], p["ssm_b_im"][layer], p["ssm_c_re"][layer],
                        p["ssm_c_im"][layer], p["ssm_d"][layer]),
        w_glu=p["w_glu"][layer].astype(BF16),
        b_glu=row(p["b_glu"][layer]),
        w_out=p["w_out"][layer].astype(BF16),
        norm2=row(p["norm2_g"][layer]),
        wq=p["peer_wq"][layer].astype(BF16),
        keys=p["peer_keys"][layer].astype(BF16).reshape(PEER_HEADS * 2, PEER_KEYS, PEER_DKEY // 2),
        u=_pack_table(p["peer_u"][layer]),
        v=_pack_table(p["peer_v"][layer]),
    )


def _trunk(x3, layers, rel_bias, final_g):
    n, l, _ = x3.shape
    x = x3.reshape(n * l, D_MODEL)
    for lp in layers:
        a3, b3, uc = _inproj(x, lp["norm1"], lp["w_in"])
        oa = _na_attention(a3.reshape(n, l, 3 * W_A), lp["na_bias"], lp["gn_a"]).reshape(n * l, W_A)
        ob = _dilated_mixture(b3.reshape(n, l, 3 * W_B), rel_bias, lp["gn_b"])
        yc = _s5(uc, n, l, lp["s5"])
        x = _outproj(x, oa, ob, yc, lp["w_out"], lp["w_glu"], lp["b_glu"], lp["gn_c"])
        x = _peer(x, lp["norm2"], lp["wq"], lp["keys"], lp["u"], lp["v"])
    return _final_norm(x, final_g).reshape(n, l, D_MODEL)


def kernel(x_prompt, x_sample, norm1_g, w_in, rpb_a, rel_bias, ssm_a_re, ssm_a_im, ssm_log_step, ssm_b_re, ssm_b_im, ssm_c_re, ssm_c_im, ssm_d, w_glu, b_glu, out_norm_a, out_norm_b, out_norm_c, w_out, norm2_g, peer_wq, peer_keys, peer_u, peer_v, final_g):
    p = dict(norm1_g=norm1_g, w_in=w_in, rpb_a=rpb_a, ssm_a_re=ssm_a_re, ssm_a_im=ssm_a_im,
             ssm_log_step=ssm_log_step, ssm_b_re=ssm_b_re, ssm_b_im=ssm_b_im, ssm_c_re=ssm_c_re,
             ssm_c_im=ssm_c_im, ssm_d=ssm_d, w_glu=w_glu, b_glu=b_glu, out_norm_a=out_norm_a,
             out_norm_b=out_norm_b, out_norm_c=out_norm_c, w_out=w_out, norm2_g=norm2_g,
             peer_wq=peer_wq, peer_keys=peer_keys, peer_u=peer_u, peer_v=peer_v)
    layers = [_prepare_layer(p, layer) for layer in range(DEPTH)]
    fg = final_g.astype(F32).reshape(1, D_MODEL)
    return (_trunk(x_prompt, layers, rel_bias, fg), _trunk(x_sample, layers, rel_bias, fg))
```

```python
import functools
import math

import jax
import jax.numpy as jnp
from jax import lax
from jax.experimental import pallas as pl
from jax.experimental.pallas import tpu as pltpu

F32 = jnp.float32
BF16 = jnp.bfloat16
I32 = jnp.int32

D_MODEL = 1024
DEPTH = 4
GRID_W = 64
HEAD_DIM = 64
N_HEADS_A = 4
W_A = N_HEADS_A * HEAD_DIM
NA_ROWS = 8
NA_COLS = 16
N_HEADS_B = 8
W_B = N_HEADS_B * HEAD_DIM
DILATIONS = ((128, 1), (512, 4), (2048, 16))
SSM_GROUP = 16
W_C = 256
N_GROUPS_C = W_C // SSM_GROUP
SSM_STATE = 64
NUM_BUCKETS = 32
MAX_DISTANCE = 1024
PEER_HEADS = 8
PEER_KEYS = 128
PEER_EXPERTS = PEER_KEYS * PEER_KEYS
PEER_TOPK = 16
PEER_DKEY = 256
PEER_SEL = PEER_HEADS * PEER_TOPK
EPS = 1e-6
NEG = -1e30

BAND = 64
S5_CHUNK = 64
S5_PAD = 128
HALF = D_MODEL // 2
WORD_ROWS = HALF // 128

MIB = 1024 * 1024


def _params(vmem_mib, n_axes=1):
    return pltpu.CompilerParams(
        vmem_limit_bytes=vmem_mib * MIB,
        dimension_semantics=("arbitrary",) * n_axes,
    )


def _rms(x, g):
    return x * lax.rsqrt(jnp.mean(x * x, axis=-1, keepdims=True) + EPS) * g


def _inproj_kernel(x_ref, g_ref, w_ref, a_ref, b_ref, c_ref):
    h = _rms(x_ref[...], g_ref[...]).astype(BF16)
    p = jnp.dot(h, w_ref[...], preferred_element_type=F32)
    a_ref[...] = p[:, : 3 * W_A].astype(BF16)
    b_ref[...] = p[:, 3 * W_A : 3 * W_A + 3 * W_B].astype(BF16)
    c_ref[...] = p[:, 3 * W_A + 3 * W_B :]


def _inproj(x, g, w):
    ntok = x.shape[0]
    tm = 512
    wcols = w.shape[1]
    return pl.pallas_call(
        _inproj_kernel,
        grid=(ntok // tm,),
        in_specs=[
            pl.BlockSpec((tm, D_MODEL), lambda i: (i, 0)),
            pl.BlockSpec((1, D_MODEL), lambda i: (0, 0)),
            pl.BlockSpec((D_MODEL, wcols), lambda i: (0, 0)),
        ],
        out_specs=[
            pl.BlockSpec((tm, 3 * W_A), lambda i: (i, 0)),
            pl.BlockSpec((tm, 3 * W_B), lambda i: (i, 0)),
            pl.BlockSpec((tm, W_C), lambda i: (i, 0)),
        ],
        out_shape=[
            jax.ShapeDtypeStruct((ntok, 3 * W_A), BF16),
            jax.ShapeDtypeStruct((ntok, 3 * W_B), BF16),
            jax.ShapeDtypeStruct((ntok, W_C), F32),
        ],
        compiler_params=_params(48),
        name="inproj",
    )(x, g, w)


def _na_kernel(q_ref, k_ref, v_ref, b_ref, g_ref, o_ref, *, rows, rows_per_tile):
    i = pl.program_id(1)
    win = NA_ROWS * GRID_W
    for j in range(rows_per_tile):
        r = i * rows_per_tile + j
        rs = jnp.clip(r - NA_ROWS // 2, 0, rows - NA_ROWS)
        base = rs - r + (NA_ROWS - 1)
        start = pl.multiple_of(rs * GRID_W, GRID_W)
        outs = []
        for h in range(N_HEADS_A):
            cols = slice(h * HEAD_DIM, (h + 1) * HEAD_DIM)
            q = q_ref[0, j * GRID_W : (j + 1) * GRID_W, cols]
            k = k_ref[0, pl.ds(start, win), cols]
            v = v_ref[0, pl.ds(start, win), cols]
            s = lax.dot_general(q, k, (((1,), (1,)), ((), ())), preferred_element_type=F32)
            s = s + b_ref[h, base]
            m = jnp.max(s, axis=-1, keepdims=True)
            p = jnp.exp(s - m)
            l = jnp.sum(p, axis=-1, keepdims=True)
            outs.append(jnp.dot(p.astype(BF16), v, preferred_element_type=F32) / l)
        o = _rms(jnp.concatenate(outs, axis=-1), g_ref[...])
        o_ref[0, j * GRID_W : (j + 1) * GRID_W, :] = o.astype(BF16)


def _na_bias_table(rpb):
    c = jnp.arange(GRID_W)
    col_start = jnp.clip(c - NA_COLS // 2, 0, GRID_W - NA_COLS)
    col_ok = (c[None, :] >= col_start[:, None]) & (c[None, :] < col_start[:, None] + NA_COLS)
    col_off = jnp.clip(c[None, :] - c[:, None], -(NA_COLS - 1), NA_COLS - 1) + (NA_COLS - 1)
    t = rpb[:, :, col_off]
    t = jnp.where(col_ok[None, None], t, NEG)
    ro = jnp.arange(NA_ROWS)[:, None] + jnp.arange(NA_ROWS)[None, :]
    t = t[:, ro]
    t = jnp.transpose(t, (0, 1, 3, 2, 4))
    return t.reshape(N_HEADS_A, NA_ROWS, GRID_W, NA_ROWS * GRID_W).astype(F32)


def _na_attention(a3, bias, g):
    n, l, _ = a3.shape
    rows = l // GRID_W
    assert rows >= NA_ROWS and l % GRID_W == 0
    rows_per_tile = 8
    tq = rows_per_tile * GRID_W
    kern = functools.partial(_na_kernel, rows=rows, rows_per_tile=rows_per_tile)
    return pl.pallas_call(
        kern,
        grid=(n, l // tq),
        in_specs=[
            pl.BlockSpec((1, tq, W_A), lambda b, i: (b, i, 0)),
            pl.BlockSpec((1, l, W_A), lambda b, i: (b, 0, 1)),
            pl.BlockSpec((1, l, W_A), lambda b, i: (b, 0, 2)),
            pl.BlockSpec(bias.shape, lambda b, i: (0, 0, 0, 0)),
            pl.BlockSpec((1, W_A), lambda b, i: (0, 0)),
        ],
        out_specs=pl.BlockSpec((1, tq, W_A), lambda b, i: (b, i, 0)),
        out_shape=jax.ShapeDtypeStruct((n, l, W_A), BF16),
        compiler_params=_params(48, 2),
        name="na_attention",
    )(a3, a3, a3, bias, g)


def _band_kernel(q_ref, k_ref, v_ref, kp_ref, vp_ref, kn_ref, vn_ref, b_ref, o_ref, lse_ref, *, n_tiles):
    i = pl.program_id(1)
    tq = q_ref.shape[1]
    kwid = tq + 2 * BAND
    col = lax.broadcasted_iota(I32, (1, kwid), 1)
    lo = jnp.where(i == 0, BAND, 0)
    hi = jnp.where(i == n_tiles - 1, tq + BAND, kwid)
    edge = jnp.where((col >= lo) & (col < hi), 0.0, NEG)
    pair = 2 * HEAD_DIM
    first = lax.broadcasted_iota(I32, (1, pair), 1) < HEAD_DIM
    nt = (((1,), (1,)), ((), ()))
    for p in range(N_HEADS_B // 2):
        cols = slice(p * pair, (p + 1) * pair)
        q2 = q_ref[0, :, cols]
        kw = jnp.concatenate([kp_ref[0, :, cols], k_ref[0, :, cols], kn_ref[0, :, cols]], axis=0)
        vw = jnp.concatenate([vp_ref[0, :, cols], v_ref[0, :, cols], vn_ref[0, :, cols]], axis=0)
        outs, lses = [], []
        for half in range(2):
            qm = jnp.where(first if half == 0 else jnp.logical_not(first), q2, jnp.zeros_like(q2))
            s = lax.dot_general(qm, kw, nt, preferred_element_type=F32) + b_ref[2 * p + half] + edge
            m = jnp.max(s, axis=-1, keepdims=True)
            e = jnp.exp(s - m)
            l = jnp.sum(e, axis=-1, keepdims=True)
            outs.append(jnp.dot(e.astype(BF16), vw, preferred_element_type=F32) / l)
            lses.append(m + jnp.log(l))
        o_ref[0, :, cols] = jnp.where(first, outs[0], outs[1])
        lse_ref[0, :, cols] = jnp.where(first, lses[0], lses[1])


def _t5_bucket(rel):
    nb = NUM_BUCKETS // 2
    ret = jnp.where(rel > 0, nb, 0)
    n = jnp.abs(rel)
    max_exact = nb // 2
    nf = jnp.maximum(n, 1).astype(F32)
    large = max_exact + (jnp.log(nf / max_exact) / math.log(MAX_DISTANCE / max_exact)
                         * (nb - max_exact)).astype(I32)
    large = jnp.minimum(large, nb - 1)
    return ret + jnp.where(n < max_exact, n, large)


BAND_BLOCKS_PER_TILE = 4


def _band_tile(ls):
    assert ls % BAND == 0
    return min(BAND_BLOCKS_PER_TILE, ls // BAND) * BAND


def _band_bias_table(rel_bias, dilation, tq):
    qi = jnp.arange(tq)
    ki = jnp.arange(tq + 2 * BAND) - BAND
    rel = ki[None, :] - qi[:, None]
    inside = jnp.abs(rel) <= BAND
    bias = jnp.transpose(rel_bias[_t5_bucket(jnp.where(inside, rel, 0) * dilation)], (2, 0, 1)).astype(F32)
    return jnp.where(inside[None], bias, NEG)


def _band_attention(qkv, bias):
    n, ls, _ = qkv.shape
    tq = _band_tile(ls)
    blocks_per_tile = tq // BAND
    n_blocks = ls // BAND
    assert ls % tq == 0 and bias.shape == (N_HEADS_B, tq, tq + 2 * BAND)
    kern = functools.partial(_band_kernel, n_tiles=ls // tq)
    cur = lambda part: pl.BlockSpec((1, tq, W_B), lambda b, i: (b, i, part))
    prev = lambda part: pl.BlockSpec(
        (1, BAND, W_B), lambda b, i: (b, jnp.maximum(i * blocks_per_tile - 1, 0), part))
    nxt = lambda part: pl.BlockSpec(
        (1, BAND, W_B), lambda b, i: (b, jnp.minimum((i + 1) * blocks_per_tile, n_blocks - 1), part))
    return pl.pallas_call(
        kern,
        grid=(n, ls // tq),
        in_specs=[cur(0), cur(1), cur(2), prev(1), prev(2), nxt(1), nxt(2),
                  pl.BlockSpec(bias.shape, lambda b, i: (0, 0, 0))],
        out_specs=[cur(0), cur(0)],
        out_shape=[jax.ShapeDtypeStruct((n, ls, W_B), F32), jax.ShapeDtypeStruct((n, ls, W_B), F32)],
        compiler_params=_params(48, 2),
        name="band_attention",
    )(qkv, qkv, qkv, qkv, qkv, qkv, qkv, bias)


def _merge_kernel(o1, o2, o3, l1, l2, l3, g_ref, out_ref):
    a, b, c = l1[...], l2[...], l3[...]
    m = jnp.maximum(jnp.maximum(a, b), c)
    ea, eb, ec = jnp.exp(a - m), jnp.exp(b - m), jnp.exp(c - m)
    den = ea + eb + ec
    o = (ea / den) * o1[...] + (eb / den) * o2[...] + (ec / den) * o3[...]
    out_ref[...] = _rms(o, g_ref[...]).astype(BF16)


def _merge(os_, ls_, g):
    ntok = os_[0].shape[0]
    tm = 512
    blk = pl.BlockSpec((tm, W_B), lambda i: (i, 0))
    return pl.pallas_call(
        _merge_kernel,
        grid=(ntok // tm,),
        in_specs=[blk] * 6 + [pl.BlockSpec((1, W_B), lambda i: (0, 0))],
        out_specs=blk,
        out_shape=jax.ShapeDtypeStruct((ntok, W_B), BF16),
        compiler_params=_params(48),
        name="dilated_merge",
    )(*os_, *ls_, g)


def _dilated_mixture(b3, rel_bias, g):
    n, l, width = b3.shape
    outs, lses = [], []
    for window, d in DILATIONS:
        assert window // (2 * d) == BAND and l % d == 0
        ls = l // d
        bias = _band_bias_table(rel_bias, d, _band_tile(ls))

        def to_sub(a):
            return jnp.transpose(a.reshape(n, ls, d, width), (0, 2, 1, 3)).reshape(n * d, ls, width)

        def from_sub(a):
            return jnp.transpose(a.reshape(n, d, ls, W_B), (0, 2, 1, 3)).reshape(n * l, W_B)

        o, lse = _band_attention(b3 if d == 1 else to_sub(b3), bias)
        outs.append(from_sub(o))
        lses.append(from_sub(lse))
    return _merge(outs, lses, g)


def _s5_kernel(u_ref, wt_ref, e_ref, f_ref, lam_ref, d_ref, y_ref, s_scr, h_scr, *, nseq, cps):
    u = u_ref[0]
    ub = u.astype(BF16)
    s_scr[...] = jnp.dot(ub, e_ref[0], preferred_element_type=F32)
    lam = lam_ref[0]
    lfr, lfi = lam[:, 0:S5_PAD], lam[:, S5_PAD : 2 * S5_PAD]
    lbr, lbi = lam[:, 2 * S5_PAD : 3 * S5_PAD], lam[:, 3 * S5_PAD :]
    zero = jnp.zeros((nseq, S5_PAD), F32)
    fr, fi, br, bi = zero, zero, zero, zero
    for c in range(cps):
        rf = slice(c * nseq, (c + 1) * nseq)
        h_scr[rf, 0:S5_PAD] = fr
        h_scr[rf, S5_PAD : 2 * S5_PAD] = fi
        sr, si = s_scr[rf, 0:S5_PAD], s_scr[rf, S5_PAD : 2 * S5_PAD]
        fr, fi = lfr * fr - lfi * fi + sr, lfr * fi + lfi * fr + si
        cb = cps - 1 - c
        rb = slice(cb * nseq, (cb + 1) * nseq)
        h_scr[rb, 2 * S5_PAD : 3 * S5_PAD] = br
        h_scr[rb, 3 * S5_PAD :] = bi
        sr, si = s_scr[rb, 2 * S5_PAD : 3 * S5_PAD], s_scr[rb, 3 * S5_PAD :]
        br, bi = lbr * br - lbi * bi + sr, lbr * bi + lbi * br + si
    y = jnp.dot(ub, wt_ref[0], preferred_element_type=F32)
    y = y + jnp.dot(h_scr[...].astype(BF16), f_ref[0], preferred_element_type=F32)
    y_ref[0] = y + d_ref[0] * u


def _s5_matrices(a_re, a_im, log_step, b_re, b_im, c_re, c_im, d_skip):
    hp = lax.Precision.HIGHEST
    t, g16, p = S5_CHUNK, SSM_GROUP, SSM_STATE
    step = jnp.exp(log_step.astype(F32))[..., None]
    ar, ai = a_re.astype(F32), a_im.astype(F32)
    decay = jnp.exp(ar * step)
    lb_re = decay * jnp.cos(ai * step)
    lb_im = decay * jnp.sin(ai * step)
    nr = lb_re - 1.0
    den = ar * ar + ai * ai
    z_re = (nr * ar + lb_im * ai) / den
    z_im = (lb_im * ar - nr * ai) / den
    br, bi = b_re.astype(F32), b_im.astype(F32)
    bb_re = z_re[..., None] * br - z_im[..., None] * bi
    bb_im = z_re[..., None] * bi + z_im[..., None] * br
    cr, ci = c_re.astype(F32), c_im.astype(F32)
    def pw_step(carry, _):
        pr, pi = carry
        return (pr * lb_re - pi * lb_im, pr * lb_im + pi * lb_re), (pr, pi)
    _, (pw_re, pw_im) = lax.scan(pw_step, (jnp.ones_like(lb_re), jnp.zeros_like(lb_im)), None, length=t + 1)
    pw_re = jnp.moveaxis(pw_re, 0, 2)
    pw_im = jnp.moveaxis(pw_im, 0, 2)
    lbb_re = pw_re[..., None] * bb_re[:, :, None] - pw_im[..., None] * bb_im[:, :, None]
    lbb_im = pw_re[..., None] * bb_im[:, :, None] + pw_im[..., None] * bb_re[:, :, None]
    taps = (jnp.einsum("dgop,dgkpi->dgkoi", cr, lbb_re[:, :, :t], precision=hp)
            - jnp.einsum("dgop,dgkpi->dgkoi", ci, lbb_im[:, :, :t], precision=hp))
    r_idx = jnp.arange(t)[:, None]
    s_idx = jnp.arange(t)[None, :]
    lag = s_idx - r_idx
    by_lag = jnp.concatenate([jnp.flip(taps[1][:, 1:], axis=1), taps[0][:, :1] + taps[1][:, :1],
                              taps[0][:, 1:]], axis=1)
    wt = jnp.transpose(by_lag[:, lag + (t - 1)], (0, 1, 4, 2, 3)).reshape(N_GROUPS_C, t * g16, t * g16)
    pad = S5_PAD - p

    def e_part(x):
        x = jnp.transpose(x, (0, 1, 3, 2)).reshape(N_GROUPS_C, t * g16, p)
        return jnp.pad(x, ((0, 0), (0, 0), (0, pad)))

    rev = jnp.arange(t - 1, -1, -1)
    e = jnp.concatenate([e_part(lbb_re[0][:, rev]), e_part(lbb_im[0][:, rev]),
                         e_part(lbb_re[1][:, :t]), e_part(lbb_im[1][:, :t])], axis=-1)
    def f_parts(d, pows):
        gr = cr[d][:, None] * pw_re[d][:, pows][:, :, None] - ci[d][:, None] * pw_im[d][:, pows][:, :, None]
        gi = cr[d][:, None] * pw_im[d][:, pows][:, :, None] + ci[d][:, None] * pw_re[d][:, pows][:, :, None]
        def shp(x):
            x = jnp.transpose(x, (0, 3, 1, 2)).reshape(N_GROUPS_C, p, t * g16)
            return jnp.pad(x, ((0, 0), (0, pad), (0, 0)))
        return shp(gr), shp(-gi)

    f_fr, f_fi = f_parts(0, jnp.arange(1, t + 1))
    f_br, f_bi = f_parts(1, jnp.arange(t, 0, -1))
    f = jnp.concatenate([f_fr, f_fi, f_br, f_bi], axis=1)

    def lam_part(x):
        return jnp.pad(x, ((0, 0), (0, pad)))[:, None, :]

    lam = jnp.concatenate([lam_part(pw_re[0][:, t]), lam_part(pw_im[0][:, t]),
                           lam_part(pw_re[1][:, t]), lam_part(pw_im[1][:, t])], axis=-1)
    dvec = jnp.tile(d_skip.astype(F32).reshape(N_GROUPS_C, 1, g16), (1, t, 1)).reshape(N_GROUPS_C, 1, t * g16)
    return wt.astype(BF16), e.astype(BF16), f.astype(BF16), lam, dvec


def _s5(uc, n, l, mats):
    wt, e, f, lam, dvec = mats
    t, g16 = S5_CHUNK, SSM_GROUP
    assert l % t == 0
    cps = l // t
    nc = cps * n
    width = t * g16
    u2 = jnp.transpose(uc.reshape(n, cps, t, N_GROUPS_C, g16), (3, 1, 0, 2, 4)).reshape(N_GROUPS_C, nc, width)
    kern = functools.partial(_s5_kernel, nseq=n, cps=cps)
    per_g = lambda shape: pl.BlockSpec((1,) + shape, lambda g: (g, 0, 0))
    y2 = pl.pallas_call(
        kern,
        grid=(N_GROUPS_C,),
        in_specs=[per_g((nc, width)), per_g((width, width)), per_g((width, 4 * S5_PAD)),
                  per_g((4 * S5_PAD, width)), per_g((1, 4 * S5_PAD)), per_g((1, width))],
        out_specs=per_g((nc, width)),
        out_shape=jax.ShapeDtypeStruct((N_GROUPS_C, nc, width), F32),
        scratch_shapes=[pltpu.VMEM((nc, 4 * S5_PAD), F32), pltpu.VMEM((nc, 4 * S5_PAD), F32)],
        compiler_params=_params(48),
        name="s5_chunked",
    )(u2, wt, e, f, lam, dvec)
    return jnp.transpose(y2.reshape(N_GROUPS_C, cps, n, t, g16), (2, 1, 3, 0, 4)).reshape(n * l, W_C)


def _outproj_kernel(x_ref, oa_ref, ob_ref, yc_ref, w_ref, wg_ref, bg_ref, gc_ref, o_ref):
    y = yc_ref[...]
    g = 0.5 * y * (1.0 + lax.erf(y * (1.0 / math.sqrt(2.0))))
    z = jnp.dot(g.astype(BF16), wg_ref[...], preferred_element_type=F32) + bg_ref[...]
    oc = g * (1.0 / (1.0 + jnp.exp(-z)))
    ocn = _rms(oc, gc_ref[...]).astype(BF16)
    acc = jnp.dot(oa_ref[...], w_ref[0:W_A, :], preferred_element_type=F32)
    acc = acc + jnp.dot(ob_ref[...], w_ref[W_A : W_A + W_B, :], preferred_element_type=F32)
    acc = acc + jnp.dot(ocn, w_ref[W_A + W_B :, :], preferred_element_type=F32)
    o_ref[...] = x_ref[...] + acc


def _outproj(x, oa, ob, yc, w, wg, bg, gc):
    ntok = x.shape[0]
    tm = 512
    row = lambda width: pl.BlockSpec((tm, width), lambda i: (i, 0))
    full = lambda a: pl.BlockSpec(a.shape, lambda i: (0, 0))
    return pl.pallas_call(
        _outproj_kernel,
        grid=(ntok // tm,),
        in_specs=[row(D_MODEL), row(W_A), row(W_B), row(W_C), full(w), full(wg), full(bg), full(gc)],
        out_specs=row(D_MODEL),
        out_shape=jax.ShapeDtypeStruct((ntok, D_MODEL), F32),
        compiler_params=_params(48),
        name="outproj",
    )(x, oa, ob, yc, w, wg, bg, gc)


def _topk_rows(vals, k, iota, sentinel):
    out_v, out_i = [], []
    for _ in range(k):
        m = jnp.max(vals, axis=0, keepdims=True)
        am = jnp.min(jnp.where(vals == m, iota, sentinel), axis=0, keepdims=True)
        out_v.append(m)
        out_i.append(am)
        vals = jnp.where(iota == am, -jnp.inf, vals)
    return jnp.concatenate(out_v, axis=0), jnp.concatenate(out_i, axis=0)


_CAND_BLOCKS = ([(0, 1, 0, PEER_TOPK)] + [(a, a + 1, 0, 8) for a in range(1, 8)] + [(8, PEER_TOPK, 0, 1)])


def _candidate_positions(tm):
    cols = []
    for a0, a1, b0, b1 in _CAND_BLOCKS:
        cols += [a * PEER_TOPK + b for a in range(a0, a1) for b in range(b0, b1)]
    covered = set(cols)
    assert all(a * PEER_TOPK + b in covered for a in range(PEER_TOPK) for b in range(PEER_TOPK)
               if (a + 1) * (b + 1) <= PEER_TOPK)
    return jnp.broadcast_to(jnp.asarray(cols, F32)[:, None], (len(cols), tm))


def _candidate_rows(first, second, combine):
    return jnp.concatenate([combine(first[a0:a1], second[b0:b1]) for a0, a1, b0, b1 in _CAND_BLOCKS], axis=0)


def _route_kernel(x_ref, g_ref, wq_ref, keys_ref, pos_ref, xn_ref, idx_ref, gate_ref):
    tm = x_ref.shape[0]
    xn = _rms(x_ref[...], g_ref[...])
    xn_ref[...] = xn
    q = jnp.dot(xn.astype(BF16), wq_ref[...], preferred_element_type=F32).astype(BF16)
    half = PEER_DKEY // 2
    iota_k = lax.broadcasted_iota(I32, (PEER_KEYS, tm), 0).astype(F32)
    cpos = pos_ref[...]
    for h in range(PEER_HEADS):
        tops = []
        for s in range(2):
            qs = q[:, (2 * h + s) * half : (2 * h + s + 1) * half]
            sc = lax.dot_general(keys_ref[2 * h + s], qs, (((1,), (1,)), ((), ())),
                                 preferred_element_type=F32)
            tops.append(_topk_rows(sc, PEER_TOPK, iota_k, float(PEER_KEYS)))
        (s0, i0), (s1, i1) = tops
        cand = _candidate_rows(s0, s1, lambda a, b: a + b)
        cidx = _candidate_rows(i0.astype(I32), i1.astype(I32), lambda a, b: a * PEER_KEYS + b)
        best, pos = _topk_rows(cand, PEER_TOPK, cpos, float(PEER_TOPK * PEER_TOPK))
        experts = [jnp.sum(jnp.where(cpos == pos[r : r + 1], cidx, 0), axis=0, keepdims=True)
                   for r in range(PEER_TOPK)]
        ex = jnp.exp(best - best[0:1])
        rows = slice(h * PEER_TOPK, (h + 1) * PEER_TOPK)
        idx_ref[rows, :] = jnp.concatenate(experts, axis=0)
        gate_ref[rows, :] = ex / jnp.sum(ex, axis=0, keepdims=True)


def _route(x, g, wq, keys):
    ntok = x.shape[0]
    tm = 256
    cpos = _candidate_positions(tm)
    return pl.pallas_call(
        _route_kernel,
        grid=(ntok // tm,),
        in_specs=[
            pl.BlockSpec((tm, D_MODEL), lambda i: (i, 0)),
            pl.BlockSpec((1, D_MODEL), lambda i: (0, 0)),
            pl.BlockSpec(wq.shape, lambda i: (0, 0)),
            pl.BlockSpec(keys.shape, lambda i: (0, 0, 0)),
            pl.BlockSpec(cpos.shape, lambda i: (0, 0)),
        ],
        out_specs=[
            pl.BlockSpec((tm, D_MODEL), lambda i: (i, 0)),
            pl.BlockSpec((PEER_SEL, tm), lambda i: (0, i)),
            pl.BlockSpec((PEER_SEL, tm), lambda i: (0, i)),
        ],
        out_shape=[
            jax.ShapeDtypeStruct((ntok, D_MODEL), F32),
            jax.ShapeDtypeStruct((PEER_SEL, ntok), I32),
            jax.ShapeDtypeStruct((PEER_SEL, ntok), F32),
        ],
        compiler_params=_params(48),
        name="peer_route",
    )(x, g, wq, keys, cpos)


def _pack_table(tab):
    tb = tab.astype(BF16)
    hi = lax.bitcast_convert_type(tb[:, :HALF], jnp.uint16).astype(jnp.uint32)
    lo = lax.bitcast_convert_type(tb[:, HALF:], jnp.uint16).astype(jnp.uint32)
    words = lax.bitcast_convert_type((hi << 16) | lo, I32)
    return words.reshape(tab.shape[0], WORD_ROWS, 128)


def _gather_rows(idx_ref, tbl_ref, s_ref, t):
    for k in range(PEER_SEL):
        e = idx_ref[t * PEER_SEL + k]
        s_ref[k * WORD_ROWS : (k + 1) * WORD_ROWS, :] = tbl_ref[e]


def _for_each_token(idx_ref, tbl_ref, bufs, tt, compute):
    nb = len(bufs)
    for b in range(nb - 1):
        _gather_rows(idx_ref, tbl_ref, bufs[b], b)

    def trip(p, carry):
        t = nb * p
        for b in range(nb):
            ahead = jnp.minimum(t + b + nb - 1, tt - 1)
            _gather_rows(idx_ref, tbl_ref, bufs[(b + nb - 1) % nb], ahead)
            compute(bufs[b], t + b)
        return carry

    lax.fori_loop(0, tt // nb, trip, 0)


def _unpack_bf16(words):
    hi = pltpu.bitcast(words & jnp.int32(-65536), F32).astype(BF16)
    lo = pltpu.bitcast(words << 16, F32).astype(BF16)
    return hi, lo


def _diag_masks():
    j = lax.broadcasted_iota(I32, (2 * WORD_ROWS, PEER_SEL * WORD_ROWS), 0)
    m = lax.broadcasted_iota(I32, (2 * WORD_ROWS, PEER_SEL * WORD_ROWS), 1)
    r = m % WORD_ROWS
    return (j < WORD_ROWS) & (r == j), (j >= WORD_ROWS) & (r == j - WORD_ROWS)


PEER_TILE = 256
GATHER_BUFFERS = 8
GATHER_SPLIT = 2


def _peer_u_kernel(idx_ref, xn_ref, gate_ref, tbl_ref, fold_ref, w_ref, *scratch):
    bufs, part_ref = scratch[:-1], scratch[-1]
    tt = xn_ref.shape[0]
    mask_hi, mask_lo = _diag_masks()
    nt = (((1,), (1,)), ((), ()))
    chunk = PEER_SEL * WORD_ROWS // GATHER_SPLIT

    def compute(s_ref, t):
        x8 = xn_ref[t].astype(BF16)
        parts = []
        for c in range(GATHER_SPLIT):
            rows = slice(c * chunk, (c + 1) * chunk)
            hi, lo = _unpack_bf16(s_ref[rows, :])
            z = (jnp.where(mask_hi[:, rows], lax.dot_general(x8, hi, nt, preferred_element_type=F32), 0.0)
                 + jnp.where(mask_lo[:, rows], lax.dot_general(x8, lo, nt, preferred_element_type=F32), 0.0))
            parts.append(jnp.sum(z, axis=0, keepdims=True))
        part_ref[pl.ds(t, 1), :] = jnp.concatenate(parts, axis=-1)

    _for_each_token(idx_ref, tbl_ref, bufs, tt, compute)
    v = part_ref[...]
    vh = v.astype(BF16)
    vl = (v - vh.astype(F32)).astype(BF16)
    a = (jnp.dot(vh, fold_ref[...], preferred_element_type=F32)
         + jnp.dot(vl, fold_ref[...], preferred_element_type=F32))
    w_ref[...] = gate_ref[...] * (0.5 * a * (1.0 + lax.erf(a * (1.0 / math.sqrt(2.0)))))


def _peer_v_kernel(idx_ref, w_ref, x_ref, tbl_ref, spread_ref, o_ref, *scratch):
    bufs, wrep_ref = scratch[:-1], scratch[-1]
    tt = w_ref.shape[0]
    mask_hi, mask_lo = _diag_masks()
    chunk = PEER_SEL * WORD_ROWS // GATHER_SPLIT
    wrep_ref[...] = jnp.dot(w_ref[...].astype(BF16), spread_ref[...], preferred_element_type=F32)

    def compute(s_ref, t):
        wr = wrep_ref[pl.ds(t, 1), :]
        w_hi = jnp.where(mask_hi, wr, 0.0).astype(BF16)
        w_lo = jnp.where(mask_lo, wr, 0.0).astype(BF16)
        o8 = x_ref[t]
        for c in range(GATHER_SPLIT):
            rows = slice(c * chunk, (c + 1) * chunk)
            hi, lo = _unpack_bf16(s_ref[rows, :])
            o8 = (o8 + jnp.dot(w_hi[:, rows], hi, preferred_element_type=F32)
                  + jnp.dot(w_lo[:, rows], lo, preferred_element_type=F32))
        o_ref[t] = o8

    _for_each_token(idx_ref, tbl_ref, bufs, tt, compute)


def _peer_tables_call(kernel, idx_flat, row_inputs, table, const, out_row_shape, scratch, name):
    ntok = idx_flat.shape[0] // PEER_SEL
    tt = PEER_TILE

    def row_spec(a):
        blk = (tt,) + a.shape[1:]
        nd = len(a.shape)
        return pl.BlockSpec(blk, lambda i: (i,) + (0,) * (nd - 1))

    return pl.pallas_call(
        kernel,
        grid=(ntok // tt,),
        in_specs=[pl.BlockSpec((tt * PEER_SEL,), lambda i: (i,), memory_space=pltpu.SMEM)]
        + [row_spec(a) for a in row_inputs]
        + [pl.BlockSpec(table.shape, lambda i: (0, 0, 0), pipeline_mode=pl.Buffered(1)),
           pl.BlockSpec(const.shape, lambda i: (0, 0))],
        out_specs=pl.BlockSpec((tt,) + out_row_shape, lambda i: (i,) + (0,) * len(out_row_shape)),
        out_shape=jax.ShapeDtypeStruct((ntok,) + out_row_shape, F32),
        scratch_shapes=scratch,
        compiler_params=_params(56),
        name=name,
    )(idx_flat, *row_inputs, table, const)


def _peer(x, g, wq, keys, u_packed, v_packed):
    ntok = x.shape[0]
    xn, idx_t, gate_t = _route(x, g, wq, keys)
    idx_flat = jnp.transpose(idx_t).reshape(ntok * PEER_SEL)
    gates = jnp.transpose(gate_t)
    nrows = PEER_SEL * WORD_ROWS
    m = jnp.arange(nrows)
    fold = (m[:, None] // WORD_ROWS == jnp.arange(PEER_SEL)[None, :]).astype(BF16)
    scratch = [pltpu.VMEM((nrows, 128), I32)] * GATHER_BUFFERS + [pltpu.VMEM((PEER_TILE, nrows), F32)]
    w = _peer_tables_call(
        _peer_u_kernel, idx_flat, [xn.reshape(ntok, 8, 128), gates], u_packed, fold, (PEER_SEL,),
        scratch, "peer_gather_u")
    out = _peer_tables_call(
        _peer_v_kernel, idx_flat, [w, x.reshape(ntok, 8, 128)], v_packed, jnp.transpose(fold), (8, 128),
        scratch, "peer_gather_v")
    return out.reshape(ntok, D_MODEL)


def _final_kernel(x_ref, g_ref, o_ref):
    o_ref[...] = _rms(x_ref[...], g_ref[...])


def _final_norm(x, g):
    ntok = x.shape[0]
    tm = 1024
    return pl.pallas_call(
        _final_kernel,
        grid=(ntok // tm,),
        in_specs=[pl.BlockSpec((tm, D_MODEL), lambda i: (i, 0)), pl.BlockSpec((1, D_MODEL), lambda i: (0, 0))],
        out_specs=pl.BlockSpec((tm, D_MODEL), lambda i: (i, 0)),
        out_shape=jax.ShapeDtypeStruct((ntok, D_MODEL), F32),
        compiler_params=_params(48),
        name="final_norm",
    )(x, g)


def _prepare_layer(p, layer):
    scale = HEAD_DIM ** -0.5
    w_in = p["w_in"][layer]
    qcols = jnp.concatenate([
        jnp.full((W_A,), scale, F32), jnp.ones((2 * W_A,), F32),
        jnp.full((W_B,), scale, F32), jnp.ones((2 * W_B + W_C,), F32)])
    row = lambda a: a.astype(F32).reshape(1, -1)
    return dict(
        norm1=row(p["norm1_g"][layer]),
        w_in=(w_in * qcols[None, :]).astype(BF16),
        na_bias=_na_bias_table(p["rpb_a"][layer]),
        gn_a=row(p["out_norm_a"][layer]),
        gn_b=row(p["out_norm_b"][layer]),
        gn_c=row(p["out_norm_c"][layer]),
        s5=_s5_matrices(p["ssm_a_re"][layer], p["ssm_a_im"][layer], p["ssm_log_step"][layer],
                        p["ssm_b_re"][layer], p["ssm_b_im"][layer], p["ssm_c_re"][layer],
                        p["ssm_c_im"][layer], p["ssm_d"][layer]),
        w_glu=p["w_glu"][layer].astype(BF16),
        b_glu=row(p["b_glu"][layer]),
        w_out=p["w_out"][layer].astype(BF16),
        norm2=row(p["norm2_g"][layer]),
        wq=p["peer_wq"][layer].astype(BF16),
        keys=p["peer_keys"][layer].astype(BF16).reshape(PEER_HEADS * 2, PEER_KEYS, PEER_DKEY // 2),
        u=_pack_table(p["peer_u"][layer]),
        v=_pack_table(p["peer_v"][layer]),
    )


def _trunk(x3, layers, rel_bias, final_g):
    n, l, _ = x3.shape
    x = x3.reshape(n * l, D_MODEL)
    for lp in layers:
        a3, b3, uc = _inproj(x, lp["norm1"], lp["w_in"])
        oa = _na_attention(a3.reshape(n, l, 3 * W_A), lp["na_bias"], lp["gn_a"]).reshape(n * l, W_A)
        ob = _dilated_mixture(b3.reshape(n, l, 3 * W_B), rel_bias, lp["gn_b"])
        yc = _s5(uc, n, l, lp["s5"])
        x = _outproj(x, oa, ob, yc, lp["w_out"], lp["w_glu"], lp["b_glu"], lp["gn_c"])
        x = _peer(x, lp["norm2"], lp["wq"], lp["keys"], lp["u"], lp["v"])
    return _final_norm(x, final_g).reshape(n, l, D_MODEL)


def kernel(x_prompt, x_sample, norm1_g, w_in, rpb_a, rel_bias, ssm_a_re, ssm_a_im, ssm_log_step, ssm_b_re, ssm_b_im, ssm_c_re, ssm_c_im, ssm_d, w_glu, b_glu, out_norm_a, out_norm_b, out_norm_c, w_out, norm2_g, peer_wq, peer_keys, peer_u, peer_v, final_g):
    p = dict(norm1_g=norm1_g, w_in=w_in, rpb_a=rpb_a, ssm_a_re=ssm_a_re, ssm_a_im=ssm_a_im,
             ssm_log_step=ssm_log_step, ssm_b_re=ssm_b_re, ssm_b_im=ssm_b_im, ssm_c_re=ssm_c_re,
             ssm_c_im=ssm_c_im, ssm_d=ssm_d, w_glu=w_glu, b_glu=b_glu, out_norm_a=out_norm_a,
             out_norm_b=out_norm_b, out_norm_c=out_norm_c, w_out=w_out, norm2_g=norm2_g,
             peer_wq=peer_wq, peer_keys=peer_keys, peer_u=peer_u, peer_v=peer_v)
    layers = [_prepare_layer(p, layer) for layer in range(DEPTH)]
    fg = final_g.astype(F32).reshape(1, D_MODEL)
    return (_trunk(x_prompt, layers, rel_bias, fg), _trunk(x_sample, layers, rel_bias, fg))
```

```python
import functools
import math

import jax
import jax.numpy as jnp
from jax import lax
from jax.experimental import pallas as pl
from jax.experimental.pallas import tpu as pltpu

F32 = jnp.float32
BF16 = jnp.bfloat16
I32 = jnp.int32

D_MODEL = 1024
DEPTH = 4
GRID_W = 64
HEAD_DIM = 64
N_HEADS_A = 4
W_A = N_HEADS_A * HEAD_DIM
NA_ROWS = 8
NA_COLS = 16
N_HEADS_B = 8
W_B = N_HEADS_B * HEAD_DIM
DILATIONS = ((128, 1), (512, 4), (2048, 16))
SSM_GROUP = 16
W_C = 256
N_GROUPS_C = W_C // SSM_GROUP
SSM_STATE = 64
NUM_BUCKETS = 32
MAX_DISTANCE = 1024
PEER_HEADS = 8
PEER_KEYS = 128
PEER_EXPERTS = PEER_KEYS * PEER_KEYS
PEER_TOPK = 16
PEER_DKEY = 256
PEER_SEL = PEER_HEADS * PEER_TOPK
EPS = 1e-6
NEG = -1e30

BAND = 64
S5_CHUNK = 64
S5_PAD = 128
HALF = D_MODEL // 2
WORD_ROWS = HALF // 128

MIB = 1024 * 1024


def _params(vmem_mib, n_axes=1):
    return pltpu.CompilerParams(
        vmem_limit_bytes=vmem_mib * MIB,
        dimension_semantics=("arbitrary",) * n_axes,
    )


def _rms(x, g):
    return x * lax.rsqrt(jnp.mean(x * x, axis=-1, keepdims=True) + EPS) * g


def _inproj_kernel(x_ref, g_ref, w_ref, a_ref, b_ref, c_ref):
    h = _rms(x_ref[...], g_ref[...]).astype(BF16)
    p = jnp.dot(h, w_ref[...], preferred_element_type=F32)
    a_ref[...] = p[:, : 3 * W_A].astype(BF16)
    b_ref[...] = p[:, 3 * W_A : 3 * W_A + 3 * W_B].astype(BF16)
    c_ref[...] = p[:, 3 * W_A + 3 * W_B :]


def _inproj(x, g, w):
    ntok = x.shape[0]
    tm = 512
    wcols = w.shape[1]
    return pl.pallas_call(
        _inproj_kernel,
        grid=(ntok // tm,),
        in_specs=[
            pl.BlockSpec((tm, D_MODEL), lambda i: (i, 0)),
            pl.BlockSpec((1, D_MODEL), lambda i: (0, 0)),
            pl.BlockSpec((D_MODEL, wcols), lambda i: (0, 0)),
        ],
        out_specs=[
            pl.BlockSpec((tm, 3 * W_A), lambda i: (i, 0)),
            pl.BlockSpec((tm, 3 * W_B), lambda i: (i, 0)),
            pl.BlockSpec((tm, W_C), lambda i: (i, 0)),
        ],
        out_shape=[
            jax.ShapeDtypeStruct((ntok, 3 * W_A), BF16),
            jax.ShapeDtypeStruct((ntok, 3 * W_B), BF16),
            jax.ShapeDtypeStruct((ntok, W_C), F32),
        ],
        compiler_params=_params(48),
        name="inproj",
    )(x, g, w)


def _na_kernel(q_ref, k_ref, v_ref, b_ref, g_ref, o_ref, *, rows, rows_per_tile):
    i = pl.program_id(1)
    win = NA_ROWS * GRID_W
    for j in range(rows_per_tile):
        r = i * rows_per_tile + j
        rs = jnp.clip(r - NA_ROWS // 2, 0, rows - NA_ROWS)
        base = rs - r + (NA_ROWS - 1)
        start = pl.multiple_of(rs * GRID_W, GRID_W)
        outs = []
        for h in range(N_HEADS_A):
            cols = slice(h * HEAD_DIM, (h + 1) * HEAD_DIM)
            q = q_ref[0, j * GRID_W : (j + 1) * GRID_W, cols]
            k = k_ref[0, pl.ds(start, win), cols]
            v = v_ref[0, pl.ds(start, win), cols]
            s = lax.dot_general(q, k, (((1,), (1,)), ((), ())), preferred_element_type=F32)
            s = s + b_ref[h, base]
            m = jnp.max(s, axis=-1, keepdims=True)
            p = jnp.exp(s - m)
            l = jnp.sum(p, axis=-1, keepdims=True)
            outs.append(jnp.dot(p.astype(BF16), v, preferred_element_type=F32) / l)
        o = _rms(jnp.concatenate(outs, axis=-1), g_ref[...])
        o_ref[0, j * GRID_W : (j + 1) * GRID_W, :] = o.astype(BF16)


def _na_bias_table(rpb):
    c = jnp.arange(GRID_W)
    col_start = jnp.clip(c - NA_COLS // 2, 0, GRID_W - NA_COLS)
    col_ok = (c[None, :] >= col_start[:, None]) & (c[None, :] < col_start[:, None] + NA_COLS)
    col_off = jnp.clip(c[None, :] - c[:, None], -(NA_COLS - 1), NA_COLS - 1) + (NA_COLS - 1)
    t = rpb[:, :, col_off]
    t = jnp.where(col_ok[None, None], t, NEG)
    ro = jnp.arange(NA_ROWS)[:, None] + jnp.arange(NA_ROWS)[None, :]
    t = t[:, ro]
    t = jnp.transpose(t, (0, 1, 3, 2, 4))
    return t.reshape(N_HEADS_A, NA_ROWS, GRID_W, NA_ROWS * GRID_W).astype(F32)


def _na_attention(a3, bias, g):
    n, l, _ = a3.shape
    rows = l // GRID_W
    assert rows >= NA_ROWS and l % GRID_W == 0
    rows_per_tile = 8
    tq = rows_per_tile * GRID_W
    kern = functools.partial(_na_kernel, rows=rows, rows_per_tile=rows_per_tile)
    return pl.pallas_call(
        kern,
        grid=(n, l // tq),
        in_specs=[
            pl.BlockSpec((1, tq, W_A), lambda b, i: (b, i, 0)),
            pl.BlockSpec((1, l, W_A), lambda b, i: (b, 0, 1)),
            pl.BlockSpec((1, l, W_A), lambda b, i: (b, 0, 2)),
            pl.BlockSpec(bias.shape, lambda b, i: (0, 0, 0, 0)),
            pl.BlockSpec((1, W_A), lambda b, i: (0, 0)),
        ],
        out_specs=pl.BlockSpec((1, tq, W_A), lambda b, i: (b, i, 0)),
        out_shape=jax.ShapeDtypeStruct((n, l, W_A), BF16),
        compiler_params=_params(48, 2),
        name="na_attention",
    )(a3, a3, a3, bias, g)


def _band_kernel(q_ref, k_ref, v_ref, kp_ref, vp_ref, kn_ref, vn_ref, b_ref, o_ref, lse_ref, *, n_tiles):
    i = pl.program_id(1)
    tq = q_ref.shape[1]
    kwid = tq + 2 * BAND
    col = lax.broadcasted_iota(I32, (1, kwid), 1)
    lo = jnp.where(i == 0, BAND, 0)
    hi = jnp.where(i == n_tiles - 1, tq + BAND, kwid)
    edge = jnp.where((col >= lo) & (col < hi), 0.0, NEG)
    pair = 2 * HEAD_DIM
    first = lax.broadcasted_iota(I32, (1, pair), 1) < HEAD_DIM
    nt = (((1,), (1,)), ((), ()))
    for p in range(N_HEADS_B // 2):
        cols = slice(p * pair, (p + 1) * pair)
        q2 = q_ref[0, :, cols]
        kw = jnp.concatenate([kp_ref[0, :, cols], k_ref[0, :, cols], kn_ref[0, :, cols]], axis=0)
        vw = jnp.concatenate([vp_ref[0, :, cols], v_ref[0, :, cols], vn_ref[0, :, cols]], axis=0)
        outs, lses = [], []
        for half in range(2):
            qm = jnp.where(first if half == 0 else jnp.logical_not(first), q2, jnp.zeros_like(q2))
            s = lax.dot_general(qm, kw, nt, preferred_element_type=F32) + b_ref[2 * p + half] + edge
            m = jnp.max(s, axis=-1, keepdims=True)
            e = jnp.exp(s - m)
            l = jnp.sum(e, axis=-1, keepdims=True)
            outs.append(jnp.dot(e.astype(BF16), vw, preferred_element_type=F32) / l)
            lses.append(m + jnp.log(l))
        o_ref[0, :, cols] = jnp.where(first, outs[0], outs[1])
        lse_ref[0, :, cols] = jnp.where(first, lses[0], lses[1])


def _t5_bucket(rel):
    nb = NUM_BUCKETS // 2
    ret = jnp.where(rel > 0, nb, 0)
    n = jnp.abs(rel)
    max_exact = nb // 2
    nf = jnp.maximum(n, 1).astype(F32)
    large = max_exact + (jnp.log(nf / max_exact) / math.log(MAX_DISTANCE / max_exact)
                         * (nb - max_exact)).astype(I32)
    large = jnp.minimum(large, nb - 1)
    return ret + jnp.where(n < max_exact, n, large)


BAND_BLOCKS_PER_TILE = 4


def _band_tile(ls):
    assert ls % BAND == 0
    return min(BAND_BLOCKS_PER_TILE, ls // BAND) * BAND


def _band_bias_table(rel_bias, dilation, tq):
    qi = jnp.arange(tq)
    ki = jnp.arange(tq + 2 * BAND) - BAND
    rel = ki[None, :] - qi[:, None]
    inside = jnp.abs(rel) <= BAND
    bias = jnp.transpose(rel_bias[_t5_bucket(jnp.where(inside, rel, 0) * dilation)], (2, 0, 1)).astype(F32)
    return jnp.where(inside[None], bias, NEG)


def _band_attention(qkv, bias):
    n, ls, _ = qkv.shape
    tq = _band_tile(ls)
    blocks_per_tile = tq // BAND
    n_blocks = ls // BAND
    assert ls % tq == 0 and bias.shape == (N_HEADS_B, tq, tq + 2 * BAND)
    kern = functools.partial(_band_kernel, n_tiles=ls // tq)
    cur = lambda part: pl.BlockSpec((1, tq, W_B), lambda b, i: (b, i, part))
    prev = lambda part: pl.BlockSpec(
        (1, BAND, W_B), lambda b, i: (b, jnp.maximum(i * blocks_per_tile - 1, 0), part))
    nxt = lambda part: pl.BlockSpec(
        (1, BAND, W_B), lambda b, i: (b, jnp.minimum((i + 1) * blocks_per_tile, n_blocks - 1), part))
    return pl.pallas_call(
        kern,
        grid=(n, ls // tq),
        in_specs=[cur(0), cur(1), cur(2), prev(1), prev(2), nxt(1), nxt(2),
                  pl.BlockSpec(bias.shape, lambda b, i: (0, 0, 0))],
        out_specs=[cur(0), cur(0)],
        out_shape=[jax.ShapeDtypeStruct((n, ls, W_B), F32), jax.ShapeDtypeStruct((n, ls, W_B), F32)],
        compiler_params=_params(48, 2),
        name="band_attention",
    )(qkv, qkv, qkv, qkv, qkv, qkv, qkv, bias)


def _merge_kernel(o1, o2, o3, l1, l2, l3, g_ref, out_ref):
    a, b, c = l1[...], l2[...], l3[...]
    m = jnp.maximum(jnp.maximum(a, b), c)
    ea, eb, ec = jnp.exp(a - m), jnp.exp(b - m), jnp.exp(c - m)
    den = ea + eb + ec
    o = (ea / den) * o1[...] + (eb / den) * o2[...] + (ec / den) * o3[...]
    out_ref[...] = _rms(o, g_ref[...]).astype(BF16)


def _merge(os_, ls_, g):
    ntok = os_[0].shape[0]
    tm = 512
    blk = pl.BlockSpec((tm, W_B), lambda i: (i, 0))
    return pl.pallas_call(
        _merge_kernel,
        grid=(ntok // tm,),
        in_specs=[blk] * 6 + [pl.BlockSpec((1, W_B), lambda i: (0, 0))],
        out_specs=blk,
        out_shape=jax.ShapeDtypeStruct((ntok, W_B), BF16),
        compiler_params=_params(48),
        name="dilated_merge",
    )(*os_, *ls_, g)


def _dilated_mixture(b3, rel_bias, g):
    n, l, width = b3.shape
    outs, lses = [], []
    for window, d in DILATIONS:
        assert window // (2 * d) == BAND and l % d == 0
        ls = l // d
        bias = _band_bias_table(rel_bias, d, _band_tile(ls))

        def to_sub(a):
            return jnp.transpose(a.reshape(n, ls, d, width), (0, 2, 1, 3)).reshape(n * d, ls, width)

        def from_sub(a):
            return jnp.transpose(a.reshape(n, d, ls, W_B), (0, 2, 1, 3)).reshape(n * l, W_B)

        o, lse = _band_attention(b3 if d == 1 else to_sub(b3), bias)
        outs.append(from_sub(o))
        lses.append(from_sub(lse))
    return _merge(outs, lses, g)


def _s5_kernel(u_ref, wt_ref, e_ref, f_ref, lam_ref, d_ref, y_ref, s_scr, h_scr, *, nseq, cps):
    u = u_ref[0]
    ub = u.astype(BF16)
    s_scr[...] = jnp.dot(ub, e_ref[0], preferred_element_type=F32)
    lam = lam_ref[0]
    lfr, lfi = lam[:, 0:S5_PAD], lam[:, S5_PAD : 2 * S5_PAD]
    lbr, lbi = lam[:, 2 * S5_PAD : 3 * S5_PAD], lam[:, 3 * S5_PAD :]
    zero = jnp.zeros((nseq, S5_PAD), F32)
    fr, fi, br, bi = zero, zero, zero, zero
    for c in range(cps):
        rf = slice(c * nseq, (c + 1) * nseq)
        h_scr[rf, 0:S5_PAD] = fr
        h_scr[rf, S5_PAD : 2 * S5_PAD] = fi
        sr, si = s_scr[rf, 0:S5_PAD], s_scr[rf, S5_PAD : 2 * S5_PAD]
        fr, fi = lfr * fr - lfi * fi + sr, lfr * fi + lfi * fr + si
        cb = cps - 1 - c
        rb = slice(cb * nseq, (cb + 1) * nseq)
        h_scr[rb, 2 * S5_PAD : 3 * S5_PAD] = br
        h_scr[rb, 3 * S5_PAD :] = bi
        sr, si = s_scr[rb, 2 * S5_PAD : 3 * S5_PAD], s_scr[rb, 3 * S5_PAD :]
        br, bi = lbr * br - lbi * bi + sr, lbr * bi + lbi * br + si
    y = jnp.dot(ub, wt_ref[0], preferred_element_type=F32)
    y = y + jnp.dot(h_scr[...].astype(BF16), f_ref[0], preferred_element_type=F32)
    y_ref[0] = y + d_ref[0] * u


def _s5_matrices(a_re, a_im, log_step, b_re, b_im, c_re, c_im, d_skip):
    hp = lax.Precision.HIGHEST
    t, g16, p = S5_CHUNK, SSM_GROUP, SSM_STATE
    step = jnp.exp(log_step.astype(F32))[..., None]
    ar, ai = a_re.astype(F32), a_im.astype(F32)
    decay = jnp.exp(ar * step)
    lb_re = decay * jnp.cos(ai * step)
    lb_im = decay * jnp.sin(ai * step)
    nr = lb_re - 1.0
    den = ar * ar + ai * ai
    z_re = (nr * ar + lb_im * ai) / den
    z_im = (lb_im * ar - nr * ai) / den
    br, bi = b_re.astype(F32), b_im.astype(F32)
    bb_re = z_re[..., None] * br - z_im[..., None] * bi
    bb_im = z_re[..., None] * bi + z_im[..., None] * br
    cr, ci = c_re.astype(F32), c_im.astype(F32)
    def pw_step(carry, _):
        pr, pi = carry
        return (pr * lb_re - pi * lb_im, pr * lb_im + pi * lb_re), (pr, pi)
    _, (pw_re, pw_im) = lax.scan(pw_step, (jnp.ones_like(lb_re), jnp.zeros_like(lb_im)), None, length=t + 1)
    pw_re = jnp.moveaxis(pw_re, 0, 2)
    pw_im = jnp.moveaxis(pw_im, 0, 2)
    lbb_re = pw_re[..., None] * bb_re[:, :, None] - pw_im[..., None] * bb_im[:, :, None]
    lbb_im = pw_re[..., None] * bb_im[:, :, None] + pw_im[..., None] * bb_re[:, :, None]
    taps = (jnp.einsum("dgop,dgkpi->dgkoi", cr, lbb_re[:, :, :t], precision=hp)
            - jnp.einsum("dgop,dgkpi->dgkoi", ci, lbb_im[:, :, :t], precision=hp))
    r_idx = jnp.arange(t)[:, None]
    s_idx = jnp.arange(t)[None, :]
    lag = s_idx - r_idx
    by_lag = jnp.concatenate([jnp.flip(taps[1][:, 1:], axis=1), taps[0][:, :1] + taps[1][:, :1],
                              taps[0][:, 1:]], axis=1)
    wt = jnp.transpose(by_lag[:, lag + (t - 1)], (0, 1, 4, 2, 3)).reshape(N_GROUPS_C, t * g16, t * g16)
    pad = S5_PAD - p

    def e_part(x):
        x = jnp.transpose(x, (0, 1, 3, 2)).reshape(N_GROUPS_C, t * g16, p)
        return jnp.pad(x, ((0, 0), (0, 0), (0, pad)))

    rev = jnp.arange(t - 1, -1, -1)
    e = jnp.concatenate([e_part(lbb_re[0][:, rev]), e_part(lbb_im[0][:, rev]),
                         e_part(lbb_re[1][:, :t]), e_part(lbb_im[1][:, :t])], axis=-1)
    def f_parts(d, pows):
        gr = cr[d][:, None] * pw_re[d][:, pows][:, :, None] - ci[d][:, None] * pw_im[d][:, pows][:, :, None]
        gi = cr[d][:, None] * pw_im[d][:, pows][:, :, None] + ci[d][:, None] * pw_re[d][:, pows][:, :, None]
        def shp(x):
            x = jnp.transpose(x, (0, 3, 1, 2)).reshape(N_GROUPS_C, p, t * g16)
            return jnp.pad(x, ((0, 0), (0, pad), (0, 0)))
        return shp(gr), shp(-gi)

    f_fr, f_fi = f_parts(0, jnp.arange(1, t + 1))
    f_br, f_bi = f_parts(1, jnp.arange(t, 0, -1))
    f = jnp.concatenate([f_fr, f_fi, f_br, f_bi], axis=1)

    def lam_part(x):
        return jnp.pad(x, ((0, 0), (0, pad)))[:, None, :]

    lam = jnp.concatenate([lam_part(pw_re[0][:, t]), lam_part(pw_im[0][:, t]),
                           lam_part(pw_re[1][:, t]), lam_part(pw_im[1][:, t])], axis=-1)
    dvec = jnp.tile(d_skip.astype(F32).reshape(N_GROUPS_C, 1, g16), (1, t, 1)).reshape(N_GROUPS_C, 1, t * g16)
    return wt.astype(BF16), e.astype(BF16), f.astype(BF16), lam, dvec


def _s5(uc, n, l, mats):
    wt, e, f, lam, dvec = mats
    t, g16 = S5_CHUNK, SSM_GROUP
    assert l % t == 0
    cps = l // t
    nc = cps * n
    width = t * g16
    u2 = jnp.transpose(uc.reshape(n, cps, t, N_GROUPS_C, g16), (3, 1, 0, 2, 4)).reshape(N_GROUPS_C, nc, width)
    kern = functools.partial(_s5_kernel, nseq=n, cps=cps)
    per_g = lambda shape: pl.BlockSpec((1,) + shape, lambda g: (g, 0, 0))
    y2 = pl.pallas_call(
        kern,
        grid=(N_GROUPS_C,),
        in_specs=[per_g((nc, width)), per_g((width, width)), per_g((width, 4 * S5_PAD)),
                  per_g((4 * S5_PAD, width)), per_g((1, 4 * S5_PAD)), per_g((1, width))],
        out_specs=per_g((nc, width)),
        out_shape=jax.ShapeDtypeStruct((N_GROUPS_C, nc, width), F32),
        scratch_shapes=[pltpu.VMEM((nc, 4 * S5_PAD), F32), pltpu.VMEM((nc, 4 * S5_PAD), F32)],
        compiler_params=_params(48),
        name="s5_chunked",
    )(u2, wt, e, f, lam, dvec)
    return jnp.transpose(y2.reshape(N_GROUPS_C, cps, n, t, g16), (2, 1, 3, 0, 4)).reshape(n * l, W_C)


def _outproj_kernel(x_ref, oa_ref, ob_ref, yc_ref, w_ref, wg_ref, bg_ref, gc_ref, o_ref):
    y = yc_ref[...]
    g = 0.5 * y * (1.0 + lax.erf(y * (1.0 / math.sqrt(2.0))))
    z = jnp.dot(g.astype(BF16), wg_ref[...], preferred_element_type=F32) + bg_ref[...]
    oc = g * (1.0 / (1.0 + jnp.exp(-z)))
    ocn = _rms(oc, gc_ref[...]).astype(BF16)
    acc = jnp.dot(oa_ref[...], w_ref[0:W_A, :], preferred_element_type=F32)
    acc = acc + jnp.dot(ob_ref[...], w_ref[W_A : W_A + W_B, :], preferred_element_type=F32)
    acc = acc + jnp.dot(ocn, w_ref[W_A + W_B :, :], preferred_element_type=F32)
    o_ref[...] = x_ref[...] + acc


def _outproj(x, oa, ob, yc, w, wg, bg, gc):
    ntok = x.shape[0]
    tm = 512
    row = lambda width: pl.BlockSpec((tm, width), lambda i: (i, 0))
    full = lambda a: pl.BlockSpec(a.shape, lambda i: (0, 0))
    return pl.pallas_call(
        _outproj_kernel,
        grid=(ntok // tm,),
        in_specs=[row(D_MODEL), row(W_A), row(W_B), row(W_C), full(w), full(wg), full(bg), full(gc)],
        out_specs=row(D_MODEL),
        out_shape=jax.ShapeDtypeStruct((ntok, D_MODEL), F32),
        compiler_params=_params(48),
        name="outproj",
    )(x, oa, ob, yc, w, wg, bg, gc)


def _topk_rows(vals, k, iota, sentinel):
    out_v, out_i = [], []
    for _ in range(k):
        m = jnp.max(vals, axis=0, keepdims=True)
        am = jnp.min(jnp.where(vals == m, iota, sentinel), axis=0, keepdims=True)
        out_v.append(m)
        out_i.append(am)
        vals = jnp.where(iota == am, -jnp.inf, vals)
    return jnp.concatenate(out_v, axis=0), jnp.concatenate(out_i, axis=0)


_CAND_BLOCKS = ([(0, 1, 0, PEER_TOPK)] + [(a, a + 1, 0, 8) for a in range(1, 8)] + [(8, PEER_TOPK, 0, 1)])


def _candidate_positions(tm):
    cols = []
    for a0, a1, b0, b1 in _CAND_BLOCKS:
        cols += [a * PEER_TOPK + b for a in range(a0, a1) for b in range(b0, b1)]
    covered = set(cols)
    assert all(a * PEER_TOPK + b in covered for a in range(PEER_TOPK) for b in range(PEER_TOPK)
               if (a + 1) * (b + 1) <= PEER_TOPK)
    return jnp.broadcast_to(jnp.asarray(cols, F32)[:, None], (len(cols), tm))


def _candidate_rows(first, second, combine):
    return jnp.concatenate([combine(first[a0:a1], second[b0:b1]) for a0, a1, b0, b1 in _CAND_BLOCKS], axis=0)


def _route_kernel(x_ref, g_ref, wq_ref, keys_ref, pos_ref, xn_ref, idx_ref, gate_ref):
    tm = x_ref.shape[0]
    xn = _rms(x_ref[...], g_ref[...])
    xn_ref[...] = xn
    q = jnp.dot(xn.astype(BF16), wq_ref[...], preferred_element_type=F32).astype(BF16)
    half = PEER_DKEY // 2
    iota_k = lax.broadcasted_iota(I32, (PEER_KEYS, tm), 0).astype(F32)
    cpos = pos_ref[...]
    for h in range(PEER_HEADS):
        tops = []
        for s in range(2):
            qs = q[:, (2 * h + s) * half : (2 * h + s + 1) * half]
            sc = lax.dot_general(keys_ref[2 * h + s], qs, (((1,), (1,)), ((), ())),
                                 preferred_element_type=F32)
            tops.append(_topk_rows(sc, PEER_TOPK, iota_k, float(PEER_KEYS)))
        (s0, i0), (s1, i1) = tops
        cand = _candidate_rows(s0, s1, lambda a, b: a + b)
        cidx = _candidate_rows(i0.astype(I32), i1.astype(I32), lambda a, b: a * PEER_KEYS + b)
        best, pos = _topk_rows(cand, PEER_TOPK, cpos, float(PEER_TOPK * PEER_TOPK))
        experts = [jnp.sum(jnp.where(cpos == pos[r : r + 1], cidx, 0), axis=0, keepdims=True)
                   for r in range(PEER_TOPK)]
        ex = jnp.exp(best - best[0:1])
        rows = slice(h * PEER_TOPK, (h + 1) * PEER_TOPK)
        idx_ref[rows, :] = jnp.concatenate(experts, axis=0)
        gate_ref[rows, :] = ex / jnp.sum(ex, axis=0, keepdims=True)


def _route(x, g, wq, keys):
    ntok = x.shape[0]
    tm = 256
    cpos = _candidate_positions(tm)
    return pl.pallas_call(
        _route_kernel,
        grid=(ntok // tm,),
        in_specs=[
            pl.BlockSpec((tm, D_MODEL), lambda i: (i, 0)),
            pl.BlockSpec((1, D_MODEL), lambda i: (0, 0)),
            pl.BlockSpec(wq.shape, lambda i: (0, 0)),
            pl.BlockSpec(keys.shape, lambda i: (0, 0, 0)),
            pl.BlockSpec(cpos.shape, lambda i: (0, 0)),
        ],
        out_specs=[
            pl.BlockSpec((tm, D_MODEL), lambda i: (i, 0)),
            pl.BlockSpec((PEER_SEL, tm), lambda i: (0, i)),
            pl.BlockSpec((PEER_SEL, tm), lambda i: (0, i)),
        ],
        out_shape=[
            jax.ShapeDtypeStruct((ntok, D_MODEL), F32),
            jax.ShapeDtypeStruct((PEER_SEL, ntok), I32),
            jax.ShapeDtypeStruct((PEER_SEL, ntok), F32),
        ],
        compiler_params=_params(48),
        name="peer_route",
    )(x, g, wq, keys, cpos)


PACK_TILE = 512


def _pack_kernel(t_ref, o_ref):
    tb = t_ref[...].astype(BF16).astype(F32)
    hi = pltpu.bitcast(tb[:, :HALF], I32)
    lo = lax.shift_right_logical(pltpu.bitcast(tb[:, HALF:], I32), jnp.int32(16))
    words = hi | lo
    for r in range(WORD_ROWS):
        o_ref[:, r, :] = words[:, r * 128 : (r + 1) * 128]


def _pack_table(tab):
    e = tab.shape[0]
    return pl.pallas_call(
        _pack_kernel,
        grid=(e // PACK_TILE,),
        in_specs=[pl.BlockSpec((PACK_TILE, D_MODEL), lambda i: (i, 0))],
        out_specs=pl.BlockSpec((PACK_TILE, WORD_ROWS, 128), lambda i: (i, 0, 0)),
        out_shape=jax.ShapeDtypeStruct((e, WORD_ROWS, 128), I32),
        compiler_params=_params(48),
        name="pack_table",
    )(tab)


PEER_TILE = 256
GATHER_BUFFERS = 8
GATHER_SPLIT = 2
INDEX_STREAMS = 8
PER_STREAM = PEER_SEL // INDEX_STREAMS
LANE_CHUNKS = D_MODEL // 128


def _gather_rows(idx_refs, tbl_ref, s_ref, t):
    for q in range(PER_STREAM):
        off = t * PER_STREAM + q
        for j in range(INDEX_STREAMS):
            k = q * INDEX_STREAMS + j
            s_ref[k * WORD_ROWS : (k + 1) * WORD_ROWS, :] = tbl_ref[idx_refs[j][off]]


def _for_each_token(idx_refs, tbl_ref, bufs, tt, load_group, compute, store_group):
    nb = len(bufs)
    for b in range(nb - 1):
        _gather_rows(idx_refs, tbl_ref, bufs[b], b)

    def trip(p, carry):
        t0 = pl.multiple_of(nb * p, nb)
        rows = load_group(t0)
        outs = []
        for b in range(nb):
            ahead = jnp.minimum(t0 + b + nb - 1, tt - 1)
            _gather_rows(idx_refs, tbl_ref, bufs[(b + nb - 1) % nb], ahead)
            outs.append(compute(bufs[b], rows, b))
        store_group(t0, outs)
        return carry

    lax.fori_loop(0, tt // nb, trip, 0)


def _unpack_bf16(words):
    hi = pltpu.bitcast(words & jnp.int32(-65536), F32).astype(BF16)
    lo = pltpu.bitcast(words << 16, F32).astype(BF16)
    return hi, lo


def _diag_masks():
    j = lax.broadcasted_iota(I32, (2 * WORD_ROWS, PEER_SEL * WORD_ROWS), 0)
    m = lax.broadcasted_iota(I32, (2 * WORD_ROWS, PEER_SEL * WORD_ROWS), 1)
    r = m % WORD_ROWS
    return (j < WORD_ROWS) & (r == j), (j >= WORD_ROWS) & (r == j - WORD_ROWS)


def _peer_u_kernel(*refs):
    idx_refs = refs[:INDEX_STREAMS]
    xn_ref, gate_ref, tbl_ref, fold_ref, w_ref = refs[INDEX_STREAMS : INDEX_STREAMS + 5]
    scratch = refs[INDEX_STREAMS + 5 :]
    bufs, part_ref = scratch[:-1], scratch[-1]
    tt = xn_ref.shape[0]
    mask_hi, mask_lo = _diag_masks()
    nt = (((1,), (1,)), ((), ()))
    chunk = PEER_SEL * WORD_ROWS // GATHER_SPLIT

    def load_group(t0):
        return xn_ref[pl.ds(t0, len(bufs)), :]

    def compute(s_ref, xrows, b):
        x8 = jnp.concatenate([xrows[b : b + 1, j * 128 : (j + 1) * 128] for j in range(LANE_CHUNKS)],
                             axis=0).astype(BF16)
        parts = []
        for c in range(GATHER_SPLIT):
            rows = slice(c * chunk, (c + 1) * chunk)
            hi, lo = _unpack_bf16(s_ref[rows, :])
            z = (jnp.where(mask_hi[:, rows], lax.dot_general(x8, hi, nt, preferred_element_type=F32), 0.0)
                 + jnp.where(mask_lo[:, rows], lax.dot_general(x8, lo, nt, preferred_element_type=F32), 0.0))
            parts.append(jnp.sum(z, axis=0, keepdims=True))
        return jnp.concatenate(parts, axis=-1)

    def store_group(t0, outs):
        part_ref[pl.ds(t0, len(bufs)), :] = jnp.concatenate(outs, axis=0)

    _for_each_token(idx_refs, tbl_ref, bufs, tt, load_group, compute, store_group)
    v = part_ref[...]
    vh = v.astype(BF16)
    vl = (v - vh.astype(F32)).astype(BF16)
    a = (jnp.dot(vh, fold_ref[...], preferred_element_type=F32)
         + jnp.dot(vl, fold_ref[...], preferred_element_type=F32))
    w_ref[...] = gate_ref[...] * (0.5 * a * (1.0 + lax.erf(a * (1.0 / math.sqrt(2.0)))))


def _peer_v_kernel(*refs):
    idx_refs = refs[:INDEX_STREAMS]
    w_ref, x_ref, tbl_ref, spread_ref, o_ref = refs[INDEX_STREAMS : INDEX_STREAMS + 5]
    scratch = refs[INDEX_STREAMS + 5 :]
    bufs, wrep_ref = scratch[:-1], scratch[-1]
    tt = w_ref.shape[0]
    mask_hi, mask_lo = _diag_masks()
    chunk = PEER_SEL * WORD_ROWS // GATHER_SPLIT
    wrep_ref[...] = jnp.dot(w_ref[...].astype(BF16), spread_ref[...], preferred_element_type=F32)

    def load_group(t0):
        return wrep_ref[pl.ds(t0, len(bufs)), :]

    def compute(s_ref, wrows, b):
        wr = wrows[b : b + 1, :]
        w_hi = jnp.where(mask_hi, wr, 0.0).astype(BF16)
        w_lo = jnp.where(mask_lo, wr, 0.0).astype(BF16)
        o8 = jnp.zeros((LANE_CHUNKS, 128), F32)
        for c in range(GATHER_SPLIT):
            rows = slice(c * chunk, (c + 1) * chunk)
            hi, lo = _unpack_bf16(s_ref[rows, :])
            o8 = (o8 + jnp.dot(w_hi[:, rows], hi, preferred_element_type=F32)
                  + jnp.dot(w_lo[:, rows], lo, preferred_element_type=F32))
        return o8

    def store_group(t0, outs):
        rows = pl.ds(t0, len(bufs))
        for j in range(LANE_CHUNKS):
            cols = slice(j * 128, (j + 1) * 128)
            delta = jnp.concatenate([o8[j : j + 1, :] for o8 in outs], axis=0)
            o_ref[rows, cols] = x_ref[rows, cols] + delta

    _for_each_token(idx_refs, tbl_ref, bufs, tt, load_group, compute, store_group)


def _peer_tables_call(kernel, idx_streams, row_inputs, table, const, out_width, scratch, name):
    ntok = row_inputs[0].shape[0]
    tt = PEER_TILE
    row_spec = lambda a: pl.BlockSpec((tt, a.shape[1]), lambda i: (i, 0))
    return pl.pallas_call(
        kernel,
        grid=(ntok // tt,),
        in_specs=[pl.BlockSpec((tt * PER_STREAM,), lambda i: (i,), memory_space=pltpu.SMEM)] * INDEX_STREAMS
        + [row_spec(a) for a in row_inputs]
        + [pl.BlockSpec(table.shape, lambda i: (0, 0, 0), pipeline_mode=pl.Buffered(1)),
           pl.BlockSpec(const.shape, lambda i: (0, 0))],
        out_specs=pl.BlockSpec((tt, out_width), lambda i: (i, 0)),
        out_shape=jax.ShapeDtypeStruct((ntok, out_width), F32),
        scratch_shapes=scratch,
        compiler_params=_params(56),
        name=name,
    )(*idx_streams, *row_inputs, table, const)


def _peer(x, g, wq, keys, u_packed, v_packed):
    ntok = x.shape[0]
    xn, idx_t, gate_t = _route(x, g, wq, keys)
    idx_streams = [jnp.transpose(idx_t[j::INDEX_STREAMS]).reshape(ntok * PER_STREAM)
                   for j in range(INDEX_STREAMS)]
    gates = jnp.transpose(gate_t)
    nrows = PEER_SEL * WORD_ROWS
    m = jnp.arange(nrows)
    fold = (m[:, None] // WORD_ROWS == jnp.arange(PEER_SEL)[None, :]).astype(BF16)
    scratch = [pltpu.VMEM((nrows, 128), I32)] * GATHER_BUFFERS + [pltpu.VMEM((PEER_TILE, nrows), F32)]
    w = _peer_tables_call(_peer_u_kernel, idx_streams, [xn, gates], u_packed, fold, PEER_SEL,
                          scratch, "peer_gather_u")
    return _peer_tables_call(_peer_v_kernel, idx_streams, [w, x], v_packed, jnp.transpose(fold), D_MODEL,
                             scratch, "peer_gather_v")


def _final_kernel(x_ref, g_ref, o_ref):
    o_ref[...] = _rms(x_ref[...], g_ref[...])


def _final_norm(x, g):
    ntok = x.shape[0]
    tm = 1024
    return pl.pallas_call(
        _final_kernel,
        grid=(ntok // tm,),
        in_specs=[pl.BlockSpec((tm, D_MODEL), lambda i: (i, 0)), pl.BlockSpec((1, D_MODEL), lambda i: (0, 0))],
        out_specs=pl.BlockSpec((tm, D_MODEL), lambda i: (i, 0)),
        out_shape=jax.ShapeDtypeStruct((ntok, D_MODEL), F32),
        compiler_params=_params(48),
        name="final_norm",
    )(x, g)


def _prepare_layer(p, layer):
    scale = HEAD_DIM ** -0.5
    w_in = p["w_in"][layer]
    qcols = jnp.concatenate([
        jnp.full((W_A,), scale, F32), jnp.ones((2 * W_A,), F32),
        jnp.full((W_B,), scale, F32), jnp.ones((2 * W_B + W_C,), F32)])
    row = lambda a: a.astype(F32).reshape(1, -1)
    return dict(
        norm1=row(p["norm1_g"][layer]),
        w_in=(w_in * qcols[None, :]).astype(BF16),
        na_bias=_na_bias_table(p["rpb_a"][layer]),
        gn_a=row(p["out_norm_a"][layer]),
        gn_b=row(p["out_norm_b"][layer]),
        gn_c=row(p["out_norm_c"][layer]),
        s5=_s5_matrices(p["ssm_a_re"][layer], p["ssm_a_im"][layer], p["ssm_log_step"][layer],
                        p["ssm_b_re"][layer], p["ssm_b_im"][layer], p["ssm_c_re"][layer],
                        p["ssm_c_im"][layer], p["ssm_d"][layer]),
        w_glu=p["w_glu"][layer].astype(BF16),
        b_glu=row(p["b_glu"][layer]),
        w_out=p["w_out"][layer].astype(BF16),
        norm2=row(p["norm2_g"][layer]),
        wq=p["peer_wq"][layer].astype(BF16),
        keys=p["peer_keys"][layer].astype(BF16).reshape(PEER_HEADS * 2, PEER_KEYS, PEER_DKEY // 2),
        u=_pack_table(p["peer_u"][layer]),
        v=_pack_table(p["peer_v"][layer]),
    )


def _trunk(x3, layers, rel_bias, final_g):
    n, l, _ = x3.shape
    x = x3.reshape(n * l, D_MODEL)
    for lp in layers:
        a3, b3, uc = _inproj(x, lp["norm1"], lp["w_in"])
        oa = _na_attention(a3.reshape(n, l, 3 * W_A), lp["na_bias"], lp["gn_a"]).reshape(n * l, W_A)
        ob = _dilated_mixture(b3.reshape(n, l, 3 * W_B), rel_bias, lp["gn_b"])
        yc = _s5(uc, n, l, lp["s5"])
        x = _outproj(x, oa, ob, yc, lp["w_out"], lp["w_glu"], lp["b_glu"], lp["gn_c"])
        x = _peer(x, lp["norm2"], lp["wq"], lp["keys"], lp["u"], lp["v"])
    return _final_norm(x, final_g).reshape(n, l, D_MODEL)


def kernel(x_prompt, x_sample, norm1_g, w_in, rpb_a, rel_bias, ssm_a_re, ssm_a_im, ssm_log_step, ssm_b_re, ssm_b_im, ssm_c_re, ssm_c_im, ssm_d, w_glu, b_glu, out_norm_a, out_norm_b, out_norm_c, w_out, norm2_g, peer_wq, peer_keys, peer_u, peer_v, final_g):
    p = dict(norm1_g=norm1_g, w_in=w_in, rpb_a=rpb_a, ssm_a_re=ssm_a_re, ssm_a_im=ssm_a_im,
             ssm_log_step=ssm_log_step, ssm_b_re=ssm_b_re, ssm_b_im=ssm_b_im, ssm_c_re=ssm_c_re,
             ssm_c_im=ssm_c_im, ssm_d=ssm_d, w_glu=w_glu, b_glu=b_glu, out_norm_a=out_norm_a,
             out_norm_b=out_norm_b, out_norm_c=out_norm_c, w_out=w_out, norm2_g=norm2_g,
             peer_wq=peer_wq, peer_keys=peer_keys, peer_u=peer_u, peer_v=peer_v)
    layers = [_prepare_layer(p, layer) for layer in range(DEPTH)]
    fg = final_g.astype(F32).reshape(1, D_MODEL)
    return (_trunk(x_prompt, layers, rel_bias, fg), _trunk(x_sample, layers, rel_bias, fg))
```

```python
import functools
import math

import jax
import jax.numpy as jnp
from jax import lax
from jax.experimental import pallas as pl
from jax.experimental.pallas import tpu as pltpu

F32 = jnp.float32
BF16 = jnp.bfloat16
I32 = jnp.int32

D_MODEL = 1024
DEPTH = 4
GRID_W = 64
HEAD_DIM = 64
N_HEADS_A = 4
W_A = N_HEADS_A * HEAD_DIM
NA_ROWS = 8
NA_COLS = 16
N_HEADS_B = 8
W_B = N_HEADS_B * HEAD_DIM
DILATIONS = ((128, 1), (512, 4), (2048, 16))
SSM_GROUP = 16
W_C = 256
N_GROUPS_C = W_C // SSM_GROUP
SSM_STATE = 64
NUM_BUCKETS = 32
MAX_DISTANCE = 1024
PEER_HEADS = 8
PEER_KEYS = 128
PEER_EXPERTS = PEER_KEYS * PEER_KEYS
PEER_TOPK = 16
PEER_DKEY = 256
PEER_SEL = PEER_HEADS * PEER_TOPK
EPS = 1e-6
NEG = -1e30

BAND = 64
S5_CHUNK = 64
S5_PAD = 128
HALF = D_MODEL // 2
WORD_ROWS = HALF // 128

MIB = 1024 * 1024


def _params(vmem_mib, n_axes=1):
    return pltpu.CompilerParams(
        vmem_limit_bytes=vmem_mib * MIB,
        dimension_semantics=("arbitrary",) * n_axes,
    )


def _rms(x, g):
    return x * lax.rsqrt(jnp.mean(x * x, axis=-1, keepdims=True) + EPS) * g


def _inproj_kernel(x_ref, g_ref, w_ref, a_ref, b_ref, c_ref):
    h = _rms(x_ref[...], g_ref[...]).astype(BF16)
    p = jnp.dot(h, w_ref[...], preferred_element_type=F32)
    a_ref[...] = p[:, : 3 * W_A].astype(BF16)
    b_ref[...] = p[:, 3 * W_A : 3 * W_A + 3 * W_B].astype(BF16)
    c_ref[...] = p[:, 3 * W_A + 3 * W_B :]


def _inproj(x, g, w):
    ntok = x.shape[0]
    tm = 512
    wcols = w.shape[1]
    return pl.pallas_call(
        _inproj_kernel,
        grid=(ntok // tm,),
        in_specs=[
            pl.BlockSpec((tm, D_MODEL), lambda i: (i, 0)),
            pl.BlockSpec((1, D_MODEL), lambda i: (0, 0)),
            pl.BlockSpec((D_MODEL, wcols), lambda i: (0, 0)),
        ],
        out_specs=[
            pl.BlockSpec((tm, 3 * W_A), lambda i: (i, 0)),
            pl.BlockSpec((tm, 3 * W_B), lambda i: (i, 0)),
            pl.BlockSpec((tm, W_C), lambda i: (i, 0)),
        ],
        out_shape=[
            jax.ShapeDtypeStruct((ntok, 3 * W_A), BF16),
            jax.ShapeDtypeStruct((ntok, 3 * W_B), BF16),
            jax.ShapeDtypeStruct((ntok, W_C), F32),
        ],
        compiler_params=_params(48),
        name="inproj",
    )(x, g, w)


def _na_kernel(q_ref, k_ref, v_ref, b_ref, g_ref, o_ref, *, rows, rows_per_tile):
    i = pl.program_id(1)
    win = NA_ROWS * GRID_W
    for j in range(rows_per_tile):
        r = i * rows_per_tile + j
        rs = jnp.clip(r - NA_ROWS // 2, 0, rows - NA_ROWS)
        base = rs - r + (NA_ROWS - 1)
        start = pl.multiple_of(rs * GRID_W, GRID_W)
        outs = []
        for h in range(N_HEADS_A):
            cols = slice(h * HEAD_DIM, (h + 1) * HEAD_DIM)
            q = q_ref[0, j * GRID_W : (j + 1) * GRID_W, cols]
            k = k_ref[0, pl.ds(start, win), cols]
            v = v_ref[0, pl.ds(start, win), cols]
            s = lax.dot_general(q, k, (((1,), (1,)), ((), ())), preferred_element_type=F32)
            s = s + b_ref[h, base]
            m = jnp.max(s, axis=-1, keepdims=True)
            p = jnp.exp(s - m)
            l = jnp.sum(p, axis=-1, keepdims=True)
            outs.append(jnp.dot(p.astype(BF16), v, preferred_element_type=F32) / l)
        o = _rms(jnp.concatenate(outs, axis=-1), g_ref[...])
        o_ref[0, j * GRID_W : (j + 1) * GRID_W, :] = o.astype(BF16)


def _na_bias_table(rpb):
    c = jnp.arange(GRID_W)
    col_start = jnp.clip(c - NA_COLS // 2, 0, GRID_W - NA_COLS)
    col_ok = (c[None, :] >= col_start[:, None]) & (c[None, :] < col_start[:, None] + NA_COLS)
    col_off = jnp.clip(c[None, :] - c[:, None], -(NA_COLS - 1), NA_COLS - 1) + (NA_COLS - 1)
    t = rpb[:, :, col_off]
    t = jnp.where(col_ok[None, None], t, NEG)
    ro = jnp.arange(NA_ROWS)[:, None] + jnp.arange(NA_ROWS)[None, :]
    t = t[:, ro]
    t = jnp.transpose(t, (0, 1, 3, 2, 4))
    return t.reshape(N_HEADS_A, NA_ROWS, GRID_W, NA_ROWS * GRID_W).astype(F32)


def _na_attention(a3, bias, g):
    n, l, _ = a3.shape
    rows = l // GRID_W
    assert rows >= NA_ROWS and l % GRID_W == 0
    rows_per_tile = 8
    tq = rows_per_tile * GRID_W
    kern = functools.partial(_na_kernel, rows=rows, rows_per_tile=rows_per_tile)
    return pl.pallas_call(
        kern,
        grid=(n, l // tq),
        in_specs=[
            pl.BlockSpec((1, tq, W_A), lambda b, i: (b, i, 0)),
            pl.BlockSpec((1, l, W_A), lambda b, i: (b, 0, 1)),
            pl.BlockSpec((1, l, W_A), lambda b, i: (b, 0, 2)),
            pl.BlockSpec(bias.shape, lambda b, i: (0, 0, 0, 0)),
            pl.BlockSpec((1, W_A), lambda b, i: (0, 0)),
        ],
        out_specs=pl.BlockSpec((1, tq, W_A), lambda b, i: (b, i, 0)),
        out_shape=jax.ShapeDtypeStruct((n, l, W_A), BF16),
        compiler_params=_params(48, 2),
        name="na_attention",
    )(a3, a3, a3, bias, g)


def _band_kernel(q_ref, k_ref, v_ref, kp_ref, vp_ref, kn_ref, vn_ref, b_ref, o_ref, lse_ref, *, n_tiles):
    i = pl.program_id(1)
    tq = q_ref.shape[1]
    kwid = tq + 2 * BAND
    col = lax.broadcasted_iota(I32, (1, kwid), 1)
    lo = jnp.where(i == 0, BAND, 0)
    hi = jnp.where(i == n_tiles - 1, tq + BAND, kwid)
    edge = jnp.where((col >= lo) & (col < hi), 0.0, NEG)
    pair = 2 * HEAD_DIM
    first = lax.broadcasted_iota(I32, (1, pair), 1) < HEAD_DIM
    nt = (((1,), (1,)), ((), ()))
    for p in range(N_HEADS_B // 2):
        cols = slice(p * pair, (p + 1) * pair)
        q2 = q_ref[0, :, cols]
        kw = jnp.concatenate([kp_ref[0, :, cols], k_ref[0, :, cols], kn_ref[0, :, cols]], axis=0)
        vw = jnp.concatenate([vp_ref[0, :, cols], v_ref[0, :, cols], vn_ref[0, :, cols]], axis=0)
        outs, lses = [], []
        for half in range(2):
            qm = jnp.where(first if half == 0 else jnp.logical_not(first), q2, jnp.zeros_like(q2))
            s = lax.dot_general(qm, kw, nt, preferred_element_type=F32) + b_ref[2 * p + half] + edge
            m = jnp.max(s, axis=-1, keepdims=True)
            e = jnp.exp(s - m)
            l = jnp.sum(e, axis=-1, keepdims=True)
            outs.append(jnp.dot(e.astype(BF16), vw, preferred_element_type=F32) / l)
            lses.append(m + jnp.log(l))
        o_ref[0, :, cols] = jnp.where(first, outs[0], outs[1])
        lse_ref[0, :, cols] = jnp.where(first, lses[0], lses[1])


def _t5_bucket(rel):
    nb = NUM_BUCKETS // 2
    ret = jnp.where(rel > 0, nb, 0)
    n = jnp.abs(rel)
    max_exact = nb // 2
    nf = jnp.maximum(n, 1).astype(F32)
    large = max_exact + (jnp.log(nf / max_exact) / math.log(MAX_DISTANCE / max_exact)
                         * (nb - max_exact)).astype(I32)
    large = jnp.minimum(large, nb - 1)
    return ret + jnp.where(n < max_exact, n, large)


BAND_BLOCKS_PER_TILE = 4


def _band_tile(ls):
    assert ls % BAND == 0
    return min(BAND_BLOCKS_PER_TILE, ls // BAND) * BAND


def _band_bias_table(rel_bias, dilation, tq):
    qi = jnp.arange(tq)
    ki = jnp.arange(tq + 2 * BAND) - BAND
    rel = ki[None, :] - qi[:, None]
    inside = jnp.abs(rel) <= BAND
    bias = jnp.transpose(rel_bias[_t5_bucket(jnp.where(inside, rel, 0) * dilation)], (2, 0, 1)).astype(F32)
    return jnp.where(inside[None], bias, NEG)


def _band_attention(qkv, bias):
    n, ls, _ = qkv.shape
    tq = _band_tile(ls)
    blocks_per_tile = tq // BAND
    n_blocks = ls // BAND
    assert ls % tq == 0 and bias.shape == (N_HEADS_B, tq, tq + 2 * BAND)
    kern = functools.partial(_band_kernel, n_tiles=ls // tq)
    cur = lambda part: pl.BlockSpec((1, tq, W_B), lambda b, i: (b, i, part))
    prev = lambda part: pl.BlockSpec(
        (1, BAND, W_B), lambda b, i: (b, jnp.maximum(i * blocks_per_tile - 1, 0), part))
    nxt = lambda part: pl.BlockSpec(
        (1, BAND, W_B), lambda b, i: (b, jnp.minimum((i + 1) * blocks_per_tile, n_blocks - 1), part))
    return pl.pallas_call(
        kern,
        grid=(n, ls // tq),
        in_specs=[cur(0), cur(1), cur(2), prev(1), prev(2), nxt(1), nxt(2),
                  pl.BlockSpec(bias.shape, lambda b, i: (0, 0, 0))],
        out_specs=[cur(0), cur(0)],
        out_shape=[jax.ShapeDtypeStruct((n, ls, W_B), F32), jax.ShapeDtypeStruct((n, ls, W_B), F32)],
        compiler_params=_params(48, 2),
        name="band_attention",
    )(qkv, qkv, qkv, qkv, qkv, qkv, qkv, bias)


def _merge_kernel(o1, o2, o3, l1, l2, l3, g_ref, out_ref):
    a, b, c = l1[...], l2[...], l3[...]
    m = jnp.maximum(jnp.maximum(a, b), c)
    ea, eb, ec = jnp.exp(a - m), jnp.exp(b - m), jnp.exp(c - m)
    den = ea + eb + ec
    o = (ea / den) * o1[...] + (eb / den) * o2[...] + (ec / den) * o3[...]
    out_ref[...] = _rms(o, g_ref[...]).astype(BF16)


def _merge(os_, ls_, g):
    ntok = os_[0].shape[0]
    tm = 512
    blk = pl.BlockSpec((tm, W_B), lambda i: (i, 0))
    return pl.pallas_call(
        _merge_kernel,
        grid=(ntok // tm,),
        in_specs=[blk] * 6 + [pl.BlockSpec((1, W_B), lambda i: (0, 0))],
        out_specs=blk,
        out_shape=jax.ShapeDtypeStruct((ntok, W_B), BF16),
        compiler_params=_params(48),
        name="dilated_merge",
    )(*os_, *ls_, g)


def _dilated_mixture(b3, rel_bias, g):
    n, l, width = b3.shape
    outs, lses = [], []
    for window, d in DILATIONS:
        assert window // (2 * d) == BAND and l % d == 0
        ls = l // d
        bias = _band_bias_table(rel_bias, d, _band_tile(ls))

        def to_sub(a):
            return jnp.transpose(a.reshape(n, ls, d, width), (0, 2, 1, 3)).reshape(n * d, ls, width)

        def from_sub(a):
            return jnp.transpose(a.reshape(n, d, ls, W_B), (0, 2, 1, 3)).reshape(n * l, W_B)

        o, lse = _band_attention(b3 if d == 1 else to_sub(b3), bias)
        outs.append(from_sub(o))
        lses.append(from_sub(lse))
    return _merge(outs, lses, g)


def _s5_kernel(u_ref, wt_ref, e_ref, f_ref, lam_ref, d_ref, y_ref, s_scr, h_scr, *, nseq, cps):
    u = u_ref[0]
    ub = u.astype(BF16)
    s_scr[...] = jnp.dot(ub, e_ref[0], preferred_element_type=F32)
    lam = lam_ref[0]
    lfr, lfi = lam[:, 0:S5_PAD], lam[:, S5_PAD : 2 * S5_PAD]
    lbr, lbi = lam[:, 2 * S5_PAD : 3 * S5_PAD], lam[:, 3 * S5_PAD :]
    zero = jnp.zeros((nseq, S5_PAD), F32)
    fr, fi, br, bi = zero, zero, zero, zero
    for c in range(cps):
        rf = slice(c * nseq, (c + 1) * nseq)
        h_scr[rf, 0:S5_PAD] = fr
        h_scr[rf, S5_PAD : 2 * S5_PAD] = fi
        sr, si = s_scr[rf, 0:S5_PAD], s_scr[rf, S5_PAD : 2 * S5_PAD]
        fr, fi = lfr * fr - lfi * fi + sr, lfr * fi + lfi * fr + si
        cb = cps - 1 - c
        rb = slice(cb * nseq, (cb + 1) * nseq)
        h_scr[rb, 2 * S5_PAD : 3 * S5_PAD] = br
        h_scr[rb, 3 * S5_PAD :] = bi
        sr, si = s_scr[rb, 2 * S5_PAD : 3 * S5_PAD], s_scr[rb, 3 * S5_PAD :]
        br, bi = lbr * br - lbi * bi + sr, lbr * bi + lbi * br + si
    y = jnp.dot(ub, wt_ref[0], preferred_element_type=F32)
    y = y + jnp.dot(h_scr[...].astype(BF16), f_ref[0], preferred_element_type=F32)
    y_ref[0] = y + d_ref[0] * u


def _s5_matrices(a_re, a_im, log_step, b_re, b_im, c_re, c_im, d_skip):
    hp = lax.Precision.HIGHEST
    t, g16, p = S5_CHUNK, SSM_GROUP, SSM_STATE
    step = jnp.exp(log_step.astype(F32))[..., None]
    ar, ai = a_re.astype(F32), a_im.astype(F32)
    decay = jnp.exp(ar * step)
    lb_re = decay * jnp.cos(ai * step)
    lb_im = decay * jnp.sin(ai * step)
    nr = lb_re - 1.0
    den = ar * ar + ai * ai
    z_re = (nr * ar + lb_im * ai) / den
    z_im = (lb_im * ar - nr * ai) / den
    br, bi = b_re.astype(F32), b_im.astype(F32)
    bb_re = z_re[..., None] * br - z_im[..., None] * bi
    bb_im = z_re[..., None] * bi + z_im[..., None] * br
    cr, ci = c_re.astype(F32), c_im.astype(F32)
    def pw_step(carry, _):
        pr, pi = carry
        return (pr * lb_re - pi * lb_im, pr * lb_im + pi * lb_re), (pr, pi)
    _, (pw_re, pw_im) = lax.scan(pw_step, (jnp.ones_like(lb_re), jnp.zeros_like(lb_im)), None, length=t + 1)
    pw_re = jnp.moveaxis(pw_re, 0, 2)
    pw_im = jnp.moveaxis(pw_im, 0, 2)
    lbb_re = pw_re[..., None] * bb_re[:, :, None] - pw_im[..., None] * bb_im[:, :, None]
    lbb_im = pw_re[..., None] * bb_im[:, :, None] + pw_im[..., None] * bb_re[:, :, None]
    taps = (jnp.einsum("dgop,dgkpi->dgkoi", cr, lbb_re[:, :, :t], precision=hp)
            - jnp.einsum("dgop,dgkpi->dgkoi", ci, lbb_im[:, :, :t], precision=hp))
    r_idx = jnp.arange(t)[:, None]
    s_idx = jnp.arange(t)[None, :]
    lag = s_idx - r_idx
    by_lag = jnp.concatenate([jnp.flip(taps[1][:, 1:], axis=1), taps[0][:, :1] + taps[1][:, :1],
                              taps[0][:, 1:]], axis=1)
    wt = jnp.transpose(by_lag[:, lag + (t - 1)], (0, 1, 4, 2, 3)).reshape(N_GROUPS_C, t * g16, t * g16)
    pad = S5_PAD - p

    def e_part(x):
        x = jnp.transpose(x, (0, 1, 3, 2)).reshape(N_GROUPS_C, t * g16, p)
        return jnp.pad(x, ((0, 0), (0, 0), (0, pad)))

    rev = jnp.arange(t - 1, -1, -1)
    e = jnp.concatenate([e_part(lbb_re[0][:, rev]), e_part(lbb_im[0][:, rev]),
                         e_part(lbb_re[1][:, :t]), e_part(lbb_im[1][:, :t])], axis=-1)
    def f_parts(d, pows):
        gr = cr[d][:, None] * pw_re[d][:, pows][:, :, None] - ci[d][:, None] * pw_im[d][:, pows][:, :, None]
        gi = cr[d][:, None] * pw_im[d][:, pows][:, :, None] + ci[d][:, None] * pw_re[d][:, pows][:, :, None]
        def shp(x):
            x = jnp.transpose(x, (0, 3, 1, 2)).reshape(N_GROUPS_C, p, t * g16)
            return jnp.pad(x, ((0, 0), (0, pad), (0, 0)))
        return shp(gr), shp(-gi)

    f_fr, f_fi = f_parts(0, jnp.arange(1, t + 1))
    f_br, f_bi = f_parts(1, jnp.arange(t, 0, -1))
    f = jnp.concatenate([f_fr, f_fi, f_br, f_bi], axis=1)

    def lam_part(x):
        return jnp.pad(x, ((0, 0), (0, pad)))[:, None, :]

    lam = jnp.concatenate([lam_part(pw_re[0][:, t]), lam_part(pw_im[0][:, t]),
                           lam_part(pw_re[1][:, t]), lam_part(pw_im[1][:, t])], axis=-1)
    dvec = jnp.tile(d_skip.astype(F32).reshape(N_GROUPS_C, 1, g16), (1, t, 1)).reshape(N_GROUPS_C, 1, t * g16)
    return wt.astype(BF16), e.astype(BF16), f.astype(BF16), lam, dvec


def _s5(uc, n, l, mats):
    wt, e, f, lam, dvec = mats
    t, g16 = S5_CHUNK, SSM_GROUP
    assert l % t == 0
    cps = l // t
    nc = cps * n
    width = t * g16
    u2 = jnp.transpose(uc.reshape(n, cps, t, N_GROUPS_C, g16), (3, 1, 0, 2, 4)).reshape(N_GROUPS_C, nc, width)
    kern = functools.partial(_s5_kernel, nseq=n, cps=cps)
    per_g = lambda shape: pl.BlockSpec((1,) + shape, lambda g: (g, 0, 0))
    y2 = pl.pallas_call(
        kern,
        grid=(N_GROUPS_C,),
        in_specs=[per_g((nc, width)), per_g((width, width)), per_g((width, 4 * S5_PAD)),
                  per_g((4 * S5_PAD, width)), per_g((1, 4 * S5_PAD)), per_g((1, width))],
        out_specs=per_g((nc, width)),
        out_shape=jax.ShapeDtypeStruct((N_GROUPS_C, nc, width), F32),
        scratch_shapes=[pltpu.VMEM((nc, 4 * S5_PAD), F32), pltpu.VMEM((nc, 4 * S5_PAD), F32)],
        compiler_params=_params(48),
        name="s5_chunked",
    )(u2, wt, e, f, lam, dvec)
    return jnp.transpose(y2.reshape(N_GROUPS_C, cps, n, t, g16), (2, 1, 3, 0, 4)).reshape(n * l, W_C)


def _outproj_kernel(x_ref, oa_ref, ob_ref, yc_ref, w_ref, wg_ref, bg_ref, gc_ref, o_ref):
    y = yc_ref[...]
    g = 0.5 * y * (1.0 + lax.erf(y * (1.0 / math.sqrt(2.0))))
    z = jnp.dot(g.astype(BF16), wg_ref[...], preferred_element_type=F32) + bg_ref[...]
    oc = g * (1.0 / (1.0 + jnp.exp(-z)))
    ocn = _rms(oc, gc_ref[...]).astype(BF16)
    acc = jnp.dot(oa_ref[...], w_ref[0:W_A, :], preferred_element_type=F32)
    acc = acc + jnp.dot(ob_ref[...], w_ref[W_A : W_A + W_B, :], preferred_element_type=F32)
    acc = acc + jnp.dot(ocn, w_ref[W_A + W_B :, :], preferred_element_type=F32)
    o_ref[...] = x_ref[...] + acc


def _outproj(x, oa, ob, yc, w, wg, bg, gc):
    ntok = x.shape[0]
    tm = 512
    row = lambda width: pl.BlockSpec((tm, width), lambda i: (i, 0))
    full = lambda a: pl.BlockSpec(a.shape, lambda i: (0, 0))
    return pl.pallas_call(
        _outproj_kernel,
        grid=(ntok // tm,),
        in_specs=[row(D_MODEL), row(W_A), row(W_B), row(W_C), full(w), full(wg), full(bg), full(gc)],
        out_specs=row(D_MODEL),
        out_shape=jax.ShapeDtypeStruct((ntok, D_MODEL), F32),
        compiler_params=_params(48),
        name="outproj",
    )(x, oa, ob, yc, w, wg, bg, gc)


def _topk_rows(vals, k, iota, sentinel):
    out_v, out_i = [], []
    for _ in range(k):
        m = jnp.max(vals, axis=0, keepdims=True)
        am = jnp.min(jnp.where(vals == m, iota, sentinel), axis=0, keepdims=True)
        out_v.append(m)
        out_i.append(am)
        vals = jnp.where(iota == am, -jnp.inf, vals)
    return jnp.concatenate(out_v, axis=0), jnp.concatenate(out_i, axis=0)


_CAND_BLOCKS = ([(0, 1, 0, PEER_TOPK)] + [(a, a + 1, 0, 8) for a in range(1, 8)] + [(8, PEER_TOPK, 0, 1)])


def _candidate_positions(tm):
    cols = []
    for a0, a1, b0, b1 in _CAND_BLOCKS:
        cols += [a * PEER_TOPK + b for a in range(a0, a1) for b in range(b0, b1)]
    covered = set(cols)
    assert all(a * PEER_TOPK + b in covered for a in range(PEER_TOPK) for b in range(PEER_TOPK)
               if (a + 1) * (b + 1) <= PEER_TOPK)
    return jnp.broadcast_to(jnp.asarray(cols, F32)[:, None], (len(cols), tm))


def _candidate_rows(first, second, combine):
    return jnp.concatenate([combine(first[a0:a1], second[b0:b1]) for a0, a1, b0, b1 in _CAND_BLOCKS], axis=0)


def _route_kernel(x_ref, g_ref, wq_ref, keys_ref, pos_ref, xn_ref, idx_ref, gate_ref):
    tm = x_ref.shape[0]
    xn = _rms(x_ref[...], g_ref[...])
    xn_ref[...] = xn
    q = jnp.dot(xn.astype(BF16), wq_ref[...], preferred_element_type=F32).astype(BF16)
    half = PEER_DKEY // 2
    iota_k = lax.broadcasted_iota(I32, (PEER_KEYS, tm), 0).astype(F32)
    cpos = pos_ref[...]
    for h in range(PEER_HEADS):
        tops = []
        for s in range(2):
            qs = q[:, (2 * h + s) * half : (2 * h + s + 1) * half]
            sc = lax.dot_general(keys_ref[2 * h + s], qs, (((1,), (1,)), ((), ())),
                                 preferred_element_type=F32)
            tops.append(_topk_rows(sc, PEER_TOPK, iota_k, float(PEER_KEYS)))
        (s0, i0), (s1, i1) = tops
        cand = _candidate_rows(s0, s1, lambda a, b: a + b)
        cidx = _candidate_rows(i0.astype(I32), i1.astype(I32), lambda a, b: (a * PEER_KEYS + b) * WORD_ROWS)
        best, pos = _topk_rows(cand, PEER_TOPK, cpos, float(PEER_TOPK * PEER_TOPK))
        experts = [jnp.sum(jnp.where(cpos == pos[r : r + 1], cidx, 0), axis=0, keepdims=True)
                   for r in range(PEER_TOPK)]
        ex = jnp.exp(best - best[0:1])
        rows = slice(h * PEER_TOPK, (h + 1) * PEER_TOPK)
        idx_ref[rows, :] = jnp.concatenate(experts, axis=0)
        gate_ref[rows, :] = ex / jnp.sum(ex, axis=0, keepdims=True)


def _route(x, g, wq, keys):
    ntok = x.shape[0]
    tm = 256
    cpos = _candidate_positions(tm)
    return pl.pallas_call(
        _route_kernel,
        grid=(ntok // tm,),
        in_specs=[
            pl.BlockSpec((tm, D_MODEL), lambda i: (i, 0)),
            pl.BlockSpec((1, D_MODEL), lambda i: (0, 0)),
            pl.BlockSpec(wq.shape, lambda i: (0, 0)),
            pl.BlockSpec(keys.shape, lambda i: (0, 0, 0)),
            pl.BlockSpec(cpos.shape, lambda i: (0, 0)),
        ],
        out_specs=[
            pl.BlockSpec((tm, D_MODEL), lambda i: (i, 0)),
            pl.BlockSpec((PEER_SEL, tm), lambda i: (0, i)),
            pl.BlockSpec((PEER_SEL, tm), lambda i: (0, i)),
        ],
        out_shape=[
            jax.ShapeDtypeStruct((ntok, D_MODEL), F32),
            jax.ShapeDtypeStruct((PEER_SEL, ntok), I32),
            jax.ShapeDtypeStruct((PEER_SEL, ntok), F32),
        ],
        compiler_params=_params(48),
        name="peer_route",
    )(x, g, wq, keys, cpos)


PACK_TILE = 512


def _pack_kernel(t_ref, o_ref):
    tb = t_ref[...].astype(BF16).astype(F32)
    hi = pltpu.bitcast(tb[:, :HALF], I32)
    lo = lax.shift_right_logical(pltpu.bitcast(tb[:, HALF:], I32), jnp.int32(16))
    words = hi | lo
    for r in range(WORD_ROWS):
        o_ref[:, r, :] = words[:, r * 128 : (r + 1) * 128]


def _pack_table(tab):
    e = tab.shape[0]
    return _pack_table_3d(tab).reshape(e * WORD_ROWS, 128)


def _pack_table_3d(tab):
    e = tab.shape[0]
    return pl.pallas_call(
        _pack_kernel,
        grid=(e // PACK_TILE,),
        in_specs=[pl.BlockSpec((PACK_TILE, D_MODEL), lambda i: (i, 0))],
        out_specs=pl.BlockSpec((PACK_TILE, WORD_ROWS, 128), lambda i: (i, 0, 0)),
        out_shape=jax.ShapeDtypeStruct((e, WORD_ROWS, 128), I32),
        compiler_params=_params(48),
        name="pack_table",
    )(tab)


PEER_TILE = 256
GATHER_BUFFERS = 8
GATHER_SPLIT = 2
INDEX_STREAMS = 8
PER_STREAM = PEER_SEL // INDEX_STREAMS
LANE_CHUNKS = D_MODEL // 128


def _gather_rows(idx_refs, tbl_ref, s_ref, t):
    for q in range(PER_STREAM):
        off = t * PER_STREAM + q
        for j in range(INDEX_STREAMS):
            k = q * INDEX_STREAMS + j
            first_row = pl.multiple_of(idx_refs[j][off], WORD_ROWS)
            s_ref[k * WORD_ROWS : (k + 1) * WORD_ROWS, :] = tbl_ref[pl.ds(first_row, WORD_ROWS), :]


def _for_each_token(idx_refs, tbl_ref, bufs, tt, load_group, compute, store_group):
    nb = len(bufs)
    for b in range(nb):
        _gather_rows(idx_refs, tbl_ref, bufs[b], b)

    def trip(p, carry):
        t0 = pl.multiple_of(nb * p, nb)
        rows = load_group(t0)
        outs = []
        for b in range(nb):
            outs.append(compute(bufs[b], rows, b))
            ahead = jnp.minimum(t0 + nb + b, tt - 1)
            _gather_rows(idx_refs, tbl_ref, bufs[b], ahead)
        store_group(t0, outs)
        return carry

    lax.fori_loop(0, tt // nb, trip, 0)


def _unpack_bf16(words):
    hi = pltpu.bitcast(words & jnp.int32(-65536), F32).astype(BF16)
    lo = pltpu.bitcast(words << 16, F32).astype(BF16)
    return hi, lo


def _diag_masks():
    j = lax.broadcasted_iota(I32, (2 * WORD_ROWS, PEER_SEL * WORD_ROWS), 0)
    m = lax.broadcasted_iota(I32, (2 * WORD_ROWS, PEER_SEL * WORD_ROWS), 1)
    r = m % WORD_ROWS
    return (j < WORD_ROWS) & (r == j), (j >= WORD_ROWS) & (r == j - WORD_ROWS)


def _peer_u_kernel(*refs):
    idx_refs = refs[:INDEX_STREAMS]
    xn_ref, gate_ref, tbl_ref, fold_ref, w_ref = refs[INDEX_STREAMS : INDEX_STREAMS + 5]
    scratch = refs[INDEX_STREAMS + 5 :]
    bufs, part_ref = scratch[:-1], scratch[-1]
    tt = xn_ref.shape[0]
    mask_hi, mask_lo = _diag_masks()
    nt = (((1,), (1,)), ((), ()))
    chunk = PEER_SEL * WORD_ROWS // GATHER_SPLIT

    def load_group(t0):
        return xn_ref[pl.ds(t0, len(bufs)), :]

    def compute(s_ref, xrows, b):
        x8 = jnp.concatenate([xrows[b : b + 1, j * 128 : (j + 1) * 128] for j in range(LANE_CHUNKS)],
                             axis=0).astype(BF16)
        parts = []
        for c in range(GATHER_SPLIT):
            rows = slice(c * chunk, (c + 1) * chunk)
            hi, lo = _unpack_bf16(s_ref[rows, :])
            z = (jnp.where(mask_hi[:, rows], lax.dot_general(x8, hi, nt, preferred_element_type=F32), 0.0)
                 + jnp.where(mask_lo[:, rows], lax.dot_general(x8, lo, nt, preferred_element_type=F32), 0.0))
            parts.append(jnp.sum(z, axis=0, keepdims=True))
        return jnp.concatenate(parts, axis=-1)

    def store_group(t0, outs):
        part_ref[pl.ds(t0, len(bufs)), :] = jnp.concatenate(outs, axis=0)

    _for_each_token(idx_refs, tbl_ref, bufs, tt, load_group, compute, store_group)
    v = part_ref[...]
    vh = v.astype(BF16)
    vl = (v - vh.astype(F32)).astype(BF16)
    a = (jnp.dot(vh, fold_ref[...], preferred_element_type=F32)
         + jnp.dot(vl, fold_ref[...], preferred_element_type=F32))
    w_ref[...] = gate_ref[...] * (0.5 * a * (1.0 + lax.erf(a * (1.0 / math.sqrt(2.0)))))


def _peer_v_kernel(*refs):
    idx_refs = refs[:INDEX_STREAMS]
    w_ref, x_ref, tbl_ref, spread_ref, o_ref = refs[INDEX_STREAMS : INDEX_STREAMS + 5]
    scratch = refs[INDEX_STREAMS + 5 :]
    bufs, wrep_ref = scratch[:-1], scratch[-1]
    tt = w_ref.shape[0]
    mask_hi, mask_lo = _diag_masks()
    chunk = PEER_SEL * WORD_ROWS // GATHER_SPLIT
    wrep_ref[...] = jnp.dot(w_ref[...].astype(BF16), spread_ref[...], preferred_element_type=F32)

    def load_group(t0):
        return wrep_ref[pl.ds(t0, len(bufs)), :]

    def compute(s_ref, wrows, b):
        wr = wrows[b : b + 1, :]
        w_hi = jnp.where(mask_hi, wr, 0.0).astype(BF16)
        w_lo = jnp.where(mask_lo, wr, 0.0).astype(BF16)
        o8 = jnp.zeros((LANE_CHUNKS, 128), F32)
        for c in range(GATHER_SPLIT):
            rows = slice(c * chunk, (c + 1) * chunk)
            hi, lo = _unpack_bf16(s_ref[rows, :])
            o8 = (o8 + jnp.dot(w_hi[:, rows], hi, preferred_element_type=F32)
                  + jnp.dot(w_lo[:, rows], lo, preferred_element_type=F32))
        return o8

    def store_group(t0, outs):
        rows = pl.ds(t0, len(bufs))
        for j in range(LANE_CHUNKS):
            cols = slice(j * 128, (j + 1) * 128)
            delta = jnp.concatenate([o8[j : j + 1, :] for o8 in outs], axis=0)
            o_ref[rows, cols] = x_ref[rows, cols] + delta

    _for_each_token(idx_refs, tbl_ref, bufs, tt, load_group, compute, store_group)


def _peer_tables_call(kernel, idx_streams, row_inputs, table, const, out_width, scratch, name):
    ntok = row_inputs[0].shape[0]
    tt = PEER_TILE
    row_spec = lambda a: pl.BlockSpec((tt, a.shape[1]), lambda i: (i, 0))
    return pl.pallas_call(
        kernel,
        grid=(ntok // tt,),
        in_specs=[pl.BlockSpec((tt * PER_STREAM,), lambda i: (i,), memory_space=pltpu.SMEM)] * INDEX_STREAMS
        + [row_spec(a) for a in row_inputs]
        + [pl.BlockSpec(table.shape, lambda i: (0, 0), pipeline_mode=pl.Buffered(1)),
           pl.BlockSpec(const.shape, lambda i: (0, 0))],
        out_specs=pl.BlockSpec((tt, out_width), lambda i: (i, 0)),
        out_shape=jax.ShapeDtypeStruct((ntok, out_width), F32),
        scratch_shapes=scratch,
        compiler_params=_params(56),
        name=name,
    )(*idx_streams, *row_inputs, table, const)


def _peer(x, g, wq, keys, u_packed, v_packed):
    ntok = x.shape[0]
    xn, idx_t, gate_t = _route(x, g, wq, keys)
    idx_streams = [jnp.transpose(idx_t[j::INDEX_STREAMS]).reshape(ntok * PER_STREAM)
                   for j in range(INDEX_STREAMS)]
    gates = jnp.transpose(gate_t)
    nrows = PEER_SEL * WORD_ROWS
    m = jnp.arange(nrows)
    fold = (m[:, None] // WORD_ROWS == jnp.arange(PEER_SEL)[None, :]).astype(BF16)
    scratch = [pltpu.VMEM((nrows, 128), I32)] * GATHER_BUFFERS + [pltpu.VMEM((PEER_TILE, nrows), F32)]
    w = _peer_tables_call(_peer_u_kernel, idx_streams, [xn, gates], u_packed, fold, PEER_SEL,
                          scratch, "peer_gather_u")
    return _peer_tables_call(_peer_v_kernel, idx_streams, [w, x], v_packed, jnp.transpose(fold), D_MODEL,
                             scratch, "peer_gather_v")


def _final_kernel(x_ref, g_ref, o_ref):
    o_ref[...] = _rms(x_ref[...], g_ref[...])


def _final_norm(x, g):
    ntok = x.shape[0]
    tm = 1024
    return pl.pallas_call(
        _final_kernel,
        grid=(ntok // tm,),
        in_specs=[pl.BlockSpec((tm, D_MODEL), lambda i: (i, 0)), pl.BlockSpec((1, D_MODEL), lambda i: (0, 0))],
        out_specs=pl.BlockSpec((tm, D_MODEL), lambda i: (i, 0)),
        out_shape=jax.ShapeDtypeStruct((ntok, D_MODEL), F32),
        compiler_params=_params(48),
        name="final_norm",
    )(x, g)


def _prepare_layer(p, layer):
    scale = HEAD_DIM ** -0.5
    w_in = p["w_in"][layer]
    qcols = jnp.concatenate([
        jnp.full((W_A,), scale, F32), jnp.ones((2 * W_A,), F32),
        jnp.full((W_B,), scale, F32), jnp.ones((2 * W_B + W_C,), F32)])
    row = lambda a: a.astype(F32).reshape(1, -1)
    return dict(
        norm1=row(p["norm1_g"][layer]),
        w_in=(w_in * qcols[None, :]).astype(BF16),
        na_bias=_na_bias_table(p["rpb_a"][layer]),
        gn_a=row(p["out_norm_a"][layer]),
        gn_b=row(p["out_norm_b"][layer]),
        gn_c=row(p["out_norm_c"][layer]),
        s5=_s5_matrices(p["ssm_a_re"][layer], p["ssm_a_im"][layer], p["ssm_log_step"][layer],
                        p["ssm_b_re"][layer], p["ssm_b_im"][layer], p["ssm_c_re"][layer],
                        p["ssm_c_im"][layer], p["ssm_d"][layer]),
        w_glu=p["w_glu"][layer].astype(BF16),
        b_glu=row(p["b_glu"][layer]),
        w_out=p["w_out"][layer].astype(BF16),
        norm2=row(p["norm2_g"][layer]),
        wq=p["peer_wq"][layer].astype(BF16),
        keys=p["peer_keys"][layer].astype(BF16).reshape(PEER_HEADS * 2, PEER_KEYS, PEER_DKEY // 2),
        u=_pack_table(p["peer_u"][layer]),
        v=_pack_table(p["peer_v"][layer]),
    )


def _trunk(x3, layers, rel_bias, final_g):
    n, l, _ = x3.shape
    x = x3.reshape(n * l, D_MODEL)
    for lp in layers:
        a3, b3, uc = _inproj(x, lp["norm1"], lp["w_in"])
        oa = _na_attention(a3.reshape(n, l, 3 * W_A), lp["na_bias"], lp["gn_a"]).reshape(n * l, W_A)
        ob = _dilated_mixture(b3.reshape(n, l, 3 * W_B), rel_bias, lp["gn_b"])
        yc = _s5(uc, n, l, lp["s5"])
        x = _outproj(x, oa, ob, yc, lp["w_out"], lp["w_glu"], lp["b_glu"], lp["gn_c"])
        x = _peer(x, lp["norm2"], lp["wq"], lp["keys"], lp["u"], lp["v"])
    return _final_norm(x, final_g).reshape(n, l, D_MODEL)


def kernel(x_prompt, x_sample, norm1_g, w_in, rpb_a, rel_bias, ssm_a_re, ssm_a_im, ssm_log_step, ssm_b_re, ssm_b_im, ssm_c_re, ssm_c_im, ssm_d, w_glu, b_glu, out_norm_a, out_norm_b, out_norm_c, w_out, norm2_g, peer_wq, peer_keys, peer_u, peer_v, final_g):
    p = dict(norm1_g=norm1_g, w_in=w_in, rpb_a=rpb_a, ssm_a_re=ssm_a_re, ssm_a_im=ssm_a_im,
             ssm_log_step=ssm_log_step, ssm_b_re=ssm_b_re, ssm_b_im=ssm_b_im, ssm_c_re=ssm_c_re,
             ssm_c_im=ssm_c_im, ssm_d=ssm_d, w_glu=w_glu, b_glu=b_glu, out_norm_a=out_norm_a,
             out_norm_b=out_norm_b, out_norm_c=out_norm_c, w_out=w_out, norm2_g=norm2_g,
             peer_wq=peer_wq, peer_keys=peer_keys, peer_u=peer_u, peer_v=peer_v)
    layers = [_prepare_layer(p, layer) for layer in range(DEPTH)]
    fg = final_g.astype(F32).reshape(1, D_MODEL)
    return (_trunk(x_prompt, layers, rel_bias, fg), _trunk(x_sample, layers, rel_bias, fg))
```

```python
import functools
import math

import jax
import jax.numpy as jnp
from jax import lax
from jax.experimental import pallas as pl
from jax.experimental.pallas import tpu as pltpu

F32 = jnp.float32
BF16 = jnp.bfloat16
I32 = jnp.int32

D_MODEL = 1024
DEPTH = 4
GRID_W = 64
HEAD_DIM = 64
N_HEADS_A = 4
W_A = N_HEADS_A * HEAD_DIM
NA_ROWS = 8
NA_COLS = 16
N_HEADS_B = 8
W_B = N_HEADS_B * HEAD_DIM
DILATIONS = ((128, 1), (512, 4), (2048, 16))
SSM_GROUP = 16
W_C = 256
N_GROUPS_C = W_C // SSM_GROUP
SSM_STATE = 64
NUM_BUCKETS = 32
MAX_DISTANCE = 1024
PEER_HEADS = 8
PEER_KEYS = 128
PEER_EXPERTS = PEER_KEYS * PEER_KEYS
PEER_TOPK = 16
PEER_DKEY = 256
PEER_SEL = PEER_HEADS * PEER_TOPK
EPS = 1e-6
NEG = -1e30

BAND = 64
S5_CHUNK = 32
S5_PAD = 128
HALF = D_MODEL // 2
WORD_ROWS = HALF // 128

MIB = 1024 * 1024


def _params(vmem_mib, n_axes=1):
    return pltpu.CompilerParams(
        vmem_limit_bytes=vmem_mib * MIB,
        dimension_semantics=("arbitrary",) * n_axes,
    )


def _rms(x, g):
    return x * lax.rsqrt(jnp.mean(x * x, axis=-1, keepdims=True) + EPS) * g


def _inproj_kernel(x_ref, g_ref, w_ref, a_ref, b_ref, c_ref):
    h = _rms(x_ref[...], g_ref[...]).astype(BF16)
    p = jnp.dot(h, w_ref[...], preferred_element_type=F32)
    a_ref[...] = p[:, : 3 * W_A].astype(BF16)
    b_ref[...] = p[:, 3 * W_A : 3 * W_A + 3 * W_B].astype(BF16)
    c_ref[...] = p[:, 3 * W_A + 3 * W_B :]


def _inproj(x, g, w):
    ntok = x.shape[0]
    tm = 512
    wcols = w.shape[1]
    return pl.pallas_call(
        _inproj_kernel,
        grid=(ntok // tm,),
        in_specs=[
            pl.BlockSpec((tm, D_MODEL), lambda i: (i, 0)),
            pl.BlockSpec((1, D_MODEL), lambda i: (0, 0)),
            pl.BlockSpec((D_MODEL, wcols), lambda i: (0, 0)),
        ],
        out_specs=[
            pl.BlockSpec((tm, 3 * W_A), lambda i: (i, 0)),
            pl.BlockSpec((tm, 3 * W_B), lambda i: (i, 0)),
            pl.BlockSpec((tm, W_C), lambda i: (i, 0)),
        ],
        out_shape=[
            jax.ShapeDtypeStruct((ntok, 3 * W_A), BF16),
            jax.ShapeDtypeStruct((ntok, 3 * W_B), BF16),
            jax.ShapeDtypeStruct((ntok, W_C), F32),
        ],
        compiler_params=_params(48),
        name="inproj",
    )(x, g, w)


NA_TILE_ROWS = NA_ROWS // 2


def _na_kernel(q_ref, kp_ref, k_ref, kn_ref, vp_ref, v_ref, vn_ref, b_ref, g_ref, o_ref):
    pair = 2 * HEAD_DIM
    first = lax.broadcasted_iota(I32, (1, pair), 1) < HEAD_DIM
    nt = (((1,), (1,)), ((), ()))
    outs = []
    for p in range(N_HEADS_A // 2):
        cols = slice(p * pair, (p + 1) * pair)
        q2 = q_ref[0, :, cols]
        kw = jnp.concatenate([kp_ref[0, :, cols], k_ref[0, :, cols], kn_ref[0, :, cols]], axis=0)
        vw = jnp.concatenate([vp_ref[0, :, cols], v_ref[0, :, cols], vn_ref[0, :, cols]], axis=0)
        halves = []
        for half in range(2):
            qm = jnp.where(first if half == 0 else jnp.logical_not(first), q2, jnp.zeros_like(q2))
            s = lax.dot_general(qm, kw, nt, preferred_element_type=F32) + b_ref[0, 2 * p + half]
            m = jnp.max(s, axis=-1, keepdims=True)
            e = jnp.exp(s - m)
            l = jnp.sum(e, axis=-1, keepdims=True)
            halves.append(jnp.dot(e.astype(BF16), vw, preferred_element_type=F32) / l)
        outs.append(jnp.where(first, halves[0], halves[1]))
    o_ref[0] = _rms(jnp.concatenate(outs, axis=-1), g_ref[...]).astype(BF16)


def _na_bias_table(rpb):
    tr = NA_TILE_ROWS
    c = jnp.arange(GRID_W)
    col_start = jnp.clip(c - NA_COLS // 2, 0, GRID_W - NA_COLS)
    col_ok = (c[None, :] >= col_start[:, None]) & (c[None, :] < col_start[:, None] + NA_COLS)
    col_off = jnp.clip(c[None, :] - c[:, None], -(NA_COLS - 1), NA_COLS - 1) + (NA_COLS - 1)
    j = jnp.arange(tr)[:, None]
    kr = jnp.arange(3 * tr)[None, :]
    row_off = kr - j + (NA_ROWS - 1 - tr)
    windows = jnp.stack([(kr >= tr) & (kr < tr + NA_ROWS) & (j >= 0),
                         (kr >= j) & (kr < j + NA_ROWS),
                         (kr < NA_ROWS) & (j >= 0)])
    t = rpb[:, jnp.clip(row_off, 0, 2 * NA_ROWS - 2)][:, :, :, col_off]
    ok = windows[:, None, :, :, None, None] & col_ok[None, None, None, None]
    t = jnp.where(ok, t[None], NEG)
    t = jnp.transpose(t, (0, 1, 2, 4, 3, 5))
    return t.reshape(3, N_HEADS_A, tr * GRID_W, 3 * tr * GRID_W).astype(F32)


def _na_attention(a3, bias, g):
    n, l, _ = a3.shape
    tq = NA_TILE_ROWS * GRID_W
    n_tiles = l // tq
    assert l % tq == 0 and n_tiles >= 3 and 2 * NA_TILE_ROWS == NA_ROWS
    prev = lambda part: pl.BlockSpec((1, tq, W_A), lambda b, i: (b, jnp.maximum(i - 1, 0), part))
    cur = lambda part: pl.BlockSpec((1, tq, W_A), lambda b, i: (b, i, part))
    nxt = lambda part: pl.BlockSpec((1, tq, W_A), lambda b, i: (b, jnp.minimum(i + 1, n_tiles - 1), part))
    variant = lambda b, i: (jnp.where(i == 0, 0, jnp.where(i == n_tiles - 1, 2, 1)), 0, 0, 0)
    return pl.pallas_call(
        _na_kernel,
        grid=(n, n_tiles),
        in_specs=[cur(0), prev(1), cur(1), nxt(1), prev(2), cur(2), nxt(2),
                  pl.BlockSpec((1,) + bias.shape[1:], variant),
                  pl.BlockSpec((1, W_A), lambda b, i: (0, 0))],
        out_specs=cur(0),
        out_shape=jax.ShapeDtypeStruct((n, l, W_A), BF16),
        compiler_params=_params(48, 2),
        name="na_attention",
    )(a3, a3, a3, a3, a3, a3, a3, bias, g)


def _band_kernel(q_ref, k_ref, v_ref, kp_ref, vp_ref, kn_ref, vn_ref, b_ref, o_ref, lse_ref, *, n_tiles):
    i = pl.program_id(1)
    tq = q_ref.shape[1]
    kwid = tq + 2 * BAND
    col = lax.broadcasted_iota(I32, (1, kwid), 1)
    lo = jnp.where(i == 0, BAND, 0)
    hi = jnp.where(i == n_tiles - 1, tq + BAND, kwid)
    edge = jnp.where((col >= lo) & (col < hi), 0.0, NEG)
    pair = 2 * HEAD_DIM
    first = lax.broadcasted_iota(I32, (1, pair), 1) < HEAD_DIM
    nt = (((1,), (1,)), ((), ()))
    for p in range(N_HEADS_B // 2):
        cols = slice(p * pair, (p + 1) * pair)
        q2 = q_ref[0, :, cols]
        kw = jnp.concatenate([kp_ref[0, :, cols], k_ref[0, :, cols], kn_ref[0, :, cols]], axis=0)
        vw = jnp.concatenate([vp_ref[0, :, cols], v_ref[0, :, cols], vn_ref[0, :, cols]], axis=0)
        outs, lses = [], []
        for half in range(2):
            qm = jnp.where(first if half == 0 else jnp.logical_not(first), q2, jnp.zeros_like(q2))
            s = lax.dot_general(qm, kw, nt, preferred_element_type=F32) + b_ref[2 * p + half] + edge
            m = jnp.max(s, axis=-1, keepdims=True)
            e = jnp.exp(s - m)
            l = jnp.sum(e, axis=-1, keepdims=True)
            outs.append(jnp.dot(e.astype(BF16), vw, preferred_element_type=F32) / l)
            lses.append(m + jnp.log(l))
        o_ref[0, :, cols] = jnp.where(first, outs[0], outs[1])
        lse_ref[0, :, cols] = jnp.where(first, lses[0], lses[1])


def _t5_bucket(rel):
    nb = NUM_BUCKETS // 2
    ret = jnp.where(rel > 0, nb, 0)
    n = jnp.abs(rel)
    max_exact = nb // 2
    nf = jnp.maximum(n, 1).astype(F32)
    large = max_exact + (jnp.log(nf / max_exact) / math.log(MAX_DISTANCE / max_exact)
                         * (nb - max_exact)).astype(I32)
    large = jnp.minimum(large, nb - 1)
    return ret + jnp.where(n < max_exact, n, large)


BAND_BLOCKS_PER_TILE = 4


def _band_tile(ls):
    assert ls % BAND == 0
    return min(BAND_BLOCKS_PER_TILE, ls // BAND) * BAND


def _band_bias_table(rel_bias, dilation, tq):
    qi = jnp.arange(tq)
    ki = jnp.arange(tq + 2 * BAND) - BAND
    rel = ki[None, :] - qi[:, None]
    inside = jnp.abs(rel) <= BAND
    bias = jnp.transpose(rel_bias[_t5_bucket(jnp.where(inside, rel, 0) * dilation)], (2, 0, 1)).astype(F32)
    return jnp.where(inside[None], bias, NEG)


def _band_attention(qkv, bias):
    n, ls, _ = qkv.shape
    tq = _band_tile(ls)
    blocks_per_tile = tq // BAND
    n_blocks = ls // BAND
    assert ls % tq == 0 and bias.shape == (N_HEADS_B, tq, tq + 2 * BAND)
    kern = functools.partial(_band_kernel, n_tiles=ls // tq)
    cur = lambda part: pl.BlockSpec((1, tq, W_B), lambda b, i: (b, i, part))
    prev = lambda part: pl.BlockSpec(
        (1, BAND, W_B), lambda b, i: (b, jnp.maximum(i * blocks_per_tile - 1, 0), part))
    nxt = lambda part: pl.BlockSpec(
        (1, BAND, W_B), lambda b, i: (b, jnp.minimum((i + 1) * blocks_per_tile, n_blocks - 1), part))
    return pl.pallas_call(
        kern,
        grid=(n, ls // tq),
        in_specs=[cur(0), cur(1), cur(2), prev(1), prev(2), nxt(1), nxt(2),
                  pl.BlockSpec(bias.shape, lambda b, i: (0, 0, 0))],
        out_specs=[cur(0), cur(0)],
        out_shape=[jax.ShapeDtypeStruct((n, ls, W_B), F32), jax.ShapeDtypeStruct((n, ls, W_B), F32)],
        compiler_params=_params(48, 2),
        name="band_attention",
    )(qkv, qkv, qkv, qkv, qkv, qkv, qkv, bias)


def _merge_kernel(o1, o2, o3, l1, l2, l3, g_ref, out_ref):
    a, b, c = l1[...], l2[...], l3[...]
    m = jnp.maximum(jnp.maximum(a, b), c)
    ea, eb, ec = jnp.exp(a - m), jnp.exp(b - m), jnp.exp(c - m)
    den = ea + eb + ec
    o = (ea / den) * o1[...] + (eb / den) * o2[...] + (ec / den) * o3[...]
    out_ref[...] = _rms(o, g_ref[...]).astype(BF16)


def _merge(os_, ls_, g):
    ntok = os_[0].shape[0]
    tm = 512
    blk = pl.BlockSpec((tm, W_B), lambda i: (i, 0))
    return pl.pallas_call(
        _merge_kernel,
        grid=(ntok // tm,),
        in_specs=[blk] * 6 + [pl.BlockSpec((1, W_B), lambda i: (0, 0))],
        out_specs=blk,
        out_shape=jax.ShapeDtypeStruct((ntok, W_B), BF16),
        compiler_params=_params(48),
        name="dilated_merge",
    )(*os_, *ls_, g)


def _dilated_mixture(b3, rel_bias, g):
    n, l, width = b3.shape
    outs, lses = [], []
    for window, d in DILATIONS:
        assert window // (2 * d) == BAND and l % d == 0
        ls = l // d
        bias = _band_bias_table(rel_bias, d, _band_tile(ls))

        def to_sub(a):
            return jnp.transpose(a.reshape(n, ls, d, width), (0, 2, 1, 3)).reshape(n * d, ls, width)

        def from_sub(a):
            return jnp.transpose(a.reshape(n, d, ls, W_B), (0, 2, 1, 3)).reshape(n * l, W_B)

        o, lse = _band_attention(b3 if d == 1 else to_sub(b3), bias)
        outs.append(from_sub(o))
        lses.append(from_sub(lse))
    return _merge(outs, lses, g)


def _s5_kernel(u_ref, wt_ref, e_ref, f_ref, lam_ref, d_ref, y_ref, s_scr, h_scr, *, nseq, cps):
    u = u_ref[0]
    ub = u.astype(BF16)
    s_scr[...] = jnp.dot(ub, e_ref[0], preferred_element_type=F32)
    lam = lam_ref[0]
    lfr, lfi = lam[:, 0:S5_PAD], lam[:, S5_PAD : 2 * S5_PAD]
    lbr, lbi = lam[:, 2 * S5_PAD : 3 * S5_PAD], lam[:, 3 * S5_PAD :]
    zero = jnp.zeros((nseq, S5_PAD), F32)
    fr, fi, br, bi = zero, zero, zero, zero
    for c in range(cps):
        rf = slice(c * nseq, (c + 1) * nseq)
        h_scr[rf, 0:S5_PAD] = fr
        h_scr[rf, S5_PAD : 2 * S5_PAD] = fi
        sr, si = s_scr[rf, 0:S5_PAD], s_scr[rf, S5_PAD : 2 * S5_PAD]
        fr, fi = lfr * fr - lfi * fi + sr, lfr * fi + lfi * fr + si
        cb = cps - 1 - c
        rb = slice(cb * nseq, (cb + 1) * nseq)
        h_scr[rb, 2 * S5_PAD : 3 * S5_PAD] = br
        h_scr[rb, 3 * S5_PAD :] = bi
        sr, si = s_scr[rb, 2 * S5_PAD : 3 * S5_PAD], s_scr[rb, 3 * S5_PAD :]
        br, bi = lbr * br - lbi * bi + sr, lbr * bi + lbi * br + si
    y = jnp.dot(ub, wt_ref[0], preferred_element_type=F32)
    y = y + jnp.dot(h_scr[...].astype(BF16), f_ref[0], preferred_element_type=F32)
    y_ref[0] = y + d_ref[0] * u


def _s5_matrices(a_re, a_im, log_step, b_re, b_im, c_re, c_im, d_skip):
    hp = lax.Precision.HIGHEST
    t, g16, p = S5_CHUNK, SSM_GROUP, SSM_STATE
    step = jnp.exp(log_step.astype(F32))[..., None]
    ar, ai = a_re.astype(F32), a_im.astype(F32)
    decay = jnp.exp(ar * step)
    lb_re = decay * jnp.cos(ai * step)
    lb_im = decay * jnp.sin(ai * step)
    nr = lb_re - 1.0
    den = ar * ar + ai * ai
    z_re = (nr * ar + lb_im * ai) / den
    z_im = (lb_im * ar - nr * ai) / den
    br, bi = b_re.astype(F32), b_im.astype(F32)
    bb_re = z_re[..., None] * br - z_im[..., None] * bi
    bb_im = z_re[..., None] * bi + z_im[..., None] * br
    cr, ci = c_re.astype(F32), c_im.astype(F32)
    def pw_step(carry, _):
        pr, pi = carry
        return (pr * lb_re - pi * lb_im, pr * lb_im + pi * lb_re), (pr, pi)
    _, (pw_re, pw_im) = lax.scan(pw_step, (jnp.ones_like(lb_re), jnp.zeros_like(lb_im)), None, length=t + 1)
    pw_re = jnp.moveaxis(pw_re, 0, 2)
    pw_im = jnp.moveaxis(pw_im, 0, 2)
    lbb_re = pw_re[..., None] * bb_re[:, :, None] - pw_im[..., None] * bb_im[:, :, None]
    lbb_im = pw_re[..., None] * bb_im[:, :, None] + pw_im[..., None] * bb_re[:, :, None]
    taps = (jnp.einsum("dgop,dgkpi->dgkoi", cr, lbb_re[:, :, :t], precision=hp)
            - jnp.einsum("dgop,dgkpi->dgkoi", ci, lbb_im[:, :, :t], precision=hp))
    r_idx = jnp.arange(t)[:, None]
    s_idx = jnp.arange(t)[None, :]
    lag = s_idx - r_idx
    by_lag = jnp.concatenate([jnp.flip(taps[1][:, 1:], axis=1), taps[0][:, :1] + taps[1][:, :1],
                              taps[0][:, 1:]], axis=1)
    wt = jnp.transpose(by_lag[:, lag + (t - 1)], (0, 1, 4, 2, 3)).reshape(N_GROUPS_C, t * g16, t * g16)
    pad = S5_PAD - p

    def e_part(x):
        x = jnp.transpose(x, (0, 1, 3, 2)).reshape(N_GROUPS_C, t * g16, p)
        return jnp.pad(x, ((0, 0), (0, 0), (0, pad)))

    rev = jnp.arange(t - 1, -1, -1)
    e = jnp.concatenate([e_part(lbb_re[0][:, rev]), e_part(lbb_im[0][:, rev]),
                         e_part(lbb_re[1][:, :t]), e_part(lbb_im[1][:, :t])], axis=-1)
    def f_parts(d, pows):
        gr = cr[d][:, None] * pw_re[d][:, pows][:, :, None] - ci[d][:, None] * pw_im[d][:, pows][:, :, None]
        gi = cr[d][:, None] * pw_im[d][:, pows][:, :, None] + ci[d][:, None] * pw_re[d][:, pows][:, :, None]
        def shp(x):
            x = jnp.transpose(x, (0, 3, 1, 2)).reshape(N_GROUPS_C, p, t * g16)
            return jnp.pad(x, ((0, 0), (0, pad), (0, 0)))
        return shp(gr), shp(-gi)

    f_fr, f_fi = f_parts(0, jnp.arange(1, t + 1))
    f_br, f_bi = f_parts(1, jnp.arange(t, 0, -1))
    f = jnp.concatenate([f_fr, f_fi, f_br, f_bi], axis=1)

    def lam_part(x):
        return jnp.pad(x, ((0, 0), (0, pad)))[:, None, :]

    lam = jnp.concatenate([lam_part(pw_re[0][:, t]), lam_part(pw_im[0][:, t]),
                           lam_part(pw_re[1][:, t]), lam_part(pw_im[1][:, t])], axis=-1)
    dvec = jnp.tile(d_skip.astype(F32).reshape(N_GROUPS_C, 1, g16), (1, t, 1)).reshape(N_GROUPS_C, 1, t * g16)
    return wt.astype(BF16), e.astype(BF16), f.astype(BF16), lam, dvec


def _s5(uc, n, l, mats):
    wt, e, f, lam, dvec = mats
    t, g16 = S5_CHUNK, SSM_GROUP
    assert l % t == 0
    cps = l // t
    nc = cps * n
    width = t * g16
    u2 = jnp.transpose(uc.reshape(n, cps, t, N_GROUPS_C, g16), (3, 1, 0, 2, 4)).reshape(N_GROUPS_C, nc, width)
    kern = functools.partial(_s5_kernel, nseq=n, cps=cps)
    per_g = lambda shape: pl.BlockSpec((1,) + shape, lambda g: (g, 0, 0))
    y2 = pl.pallas_call(
        kern,
        grid=(N_GROUPS_C,),
        in_specs=[per_g((nc, width)), per_g((width, width)), per_g((width, 4 * S5_PAD)),
                  per_g((4 * S5_PAD, width)), per_g((1, 4 * S5_PAD)), per_g((1, width))],
        out_specs=per_g((nc, width)),
        out_shape=jax.ShapeDtypeStruct((N_GROUPS_C, nc, width), F32),
        scratch_shapes=[pltpu.VMEM((nc, 4 * S5_PAD), F32), pltpu.VMEM((nc, 4 * S5_PAD), F32)],
        compiler_params=_params(48),
        name="s5_chunked",
    )(u2, wt, e, f, lam, dvec)
    return jnp.transpose(y2.reshape(N_GROUPS_C, cps, n, t, g16), (2, 1, 3, 0, 4)).reshape(n * l, W_C)


def _outproj_kernel(x_ref, oa_ref, ob_ref, yc_ref, w_ref, wg_ref, bg_ref, gc_ref, o_ref):
    y = yc_ref[...]
    g = 0.5 * y * (1.0 + lax.erf(y * (1.0 / math.sqrt(2.0))))
    z = jnp.dot(g.astype(BF16), wg_ref[...], preferred_element_type=F32) + bg_ref[...]
    oc = g * (1.0 / (1.0 + jnp.exp(-z)))
    ocn = _rms(oc, gc_ref[...]).astype(BF16)
    acc = jnp.dot(oa_ref[...], w_ref[0:W_A, :], preferred_element_type=F32)
    acc = acc + jnp.dot(ob_ref[...], w_ref[W_A : W_A + W_B, :], preferred_element_type=F32)
    acc = acc + jnp.dot(ocn, w_ref[W_A + W_B :, :], preferred_element_type=F32)
    o_ref[...] = x_ref[...] + acc


def _outproj(x, oa, ob, yc, w, wg, bg, gc):
    ntok = x.shape[0]
    tm = 512
    row = lambda width: pl.BlockSpec((tm, width), lambda i: (i, 0))
    full = lambda a: pl.BlockSpec(a.shape, lambda i: (0, 0))
    return pl.pallas_call(
        _outproj_kernel,
        grid=(ntok // tm,),
        in_specs=[row(D_MODEL), row(W_A), row(W_B), row(W_C), full(w), full(wg), full(bg), full(gc)],
        out_specs=row(D_MODEL),
        out_shape=jax.ShapeDtypeStruct((ntok, D_MODEL), F32),
        compiler_params=_params(48),
        name="outproj",
    )(x, oa, ob, yc, w, wg, bg, gc)


def _topk_rows(vals, k, iota, sentinel):
    out_v, out_i = [], []
    for _ in range(k):
        m = jnp.max(vals, axis=0, keepdims=True)
        am = jnp.min(jnp.where(vals == m, iota, sentinel), axis=0, keepdims=True)
        out_v.append(m)
        out_i.append(am)
        vals = jnp.where(iota == am, -jnp.inf, vals)
    return jnp.concatenate(out_v, axis=0), jnp.concatenate(out_i, axis=0)


_CAND_BLOCKS = ([(0, 1, 0, PEER_TOPK)] + [(a, a + 1, 0, 8) for a in range(1, 8)] + [(8, PEER_TOPK, 0, 1)])


def _candidate_positions(tm):
    cols = []
    for a0, a1, b0, b1 in _CAND_BLOCKS:
        cols += [a * PEER_TOPK + b for a in range(a0, a1) for b in range(b0, b1)]
    covered = set(cols)
    assert all(a * PEER_TOPK + b in covered for a in range(PEER_TOPK) for b in range(PEER_TOPK)
               if (a + 1) * (b + 1) <= PEER_TOPK)
    return jnp.broadcast_to(jnp.asarray(cols, F32)[:, None], (len(cols), tm))


def _candidate_rows(first, second, combine):
    return jnp.concatenate([combine(first[a0:a1], second[b0:b1]) for a0, a1, b0, b1 in _CAND_BLOCKS], axis=0)


def _route_kernel(x_ref, g_ref, wq_ref, keys_ref, pos_ref, xn_ref, idx_ref, gate_ref):
    tm = x_ref.shape[0]
    xn = _rms(x_ref[...], g_ref[...])
    xn_ref[...] = xn
    q = jnp.dot(xn.astype(BF16), wq_ref[...], preferred_element_type=F32).astype(BF16)
    half = PEER_DKEY // 2
    iota_k = lax.broadcasted_iota(I32, (PEER_KEYS, tm), 0).astype(F32)
    cpos = pos_ref[...]
    for h in range(PEER_HEADS):
        tops = []
        for s in range(2):
            qs = q[:, (2 * h + s) * half : (2 * h + s + 1) * half]
            sc = lax.dot_general(keys_ref[2 * h + s], qs, (((1,), (1,)), ((), ())),
                                 preferred_element_type=F32)
            tops.append(_topk_rows(sc, PEER_TOPK, iota_k, float(PEER_KEYS)))
        (s0, i0), (s1, i1) = tops
        cand = _candidate_rows(s0, s1, lambda a, b: a + b)
        cidx = _candidate_rows(i0.astype(I32), i1.astype(I32), lambda a, b: (a * PEER_KEYS + b) * WORD_ROWS)
        best, pos = _topk_rows(cand, PEER_TOPK, cpos, float(PEER_TOPK * PEER_TOPK))
        experts = [jnp.sum(jnp.where(cpos == pos[r : r + 1], cidx, 0), axis=0, keepdims=True)
                   for r in range(PEER_TOPK)]
        ex = jnp.exp(best - best[0:1])
        rows = slice(h * PEER_TOPK, (h + 1) * PEER_TOPK)
        idx_ref[rows, :] = jnp.concatenate(experts, axis=0)
        gate_ref[rows, :] = ex / jnp.sum(ex, axis=0, keepdims=True)


def _route(x, g, wq, keys):
    ntok = x.shape[0]
    tm = 256
    cpos = _candidate_positions(tm)
    return pl.pallas_call(
        _route_kernel,
        grid=(ntok // tm,),
        in_specs=[
            pl.BlockSpec((tm, D_MODEL), lambda i: (i, 0)),
            pl.BlockSpec((1, D_MODEL), lambda i: (0, 0)),
            pl.BlockSpec(wq.shape, lambda i: (0, 0)),
            pl.BlockSpec(keys.shape, lambda i: (0, 0, 0)),
            pl.BlockSpec(cpos.shape, lambda i: (0, 0)),
        ],
        out_specs=[
            pl.BlockSpec((tm, D_MODEL), lambda i: (i, 0)),
            pl.BlockSpec((PEER_SEL, tm), lambda i: (0, i)),
            pl.BlockSpec((PEER_SEL, tm), lambda i: (0, i)),
        ],
        out_shape=[
            jax.ShapeDtypeStruct((ntok, D_MODEL), F32),
            jax.ShapeDtypeStruct((PEER_SEL, ntok), I32),
            jax.ShapeDtypeStruct((PEER_SEL, ntok), F32),
        ],
        compiler_params=_params(48),
        name="peer_route",
    )(x, g, wq, keys, cpos)


PACK_TILE = 512


def _pack_kernel(t_ref, o_ref):
    tb = t_ref[...].astype(BF16).astype(F32)
    hi = pltpu.bitcast(tb[:, :HALF], I32)
    lo = lax.shift_right_logical(pltpu.bitcast(tb[:, HALF:], I32), jnp.int32(16))
    words = hi | lo
    for r in range(WORD_ROWS):
        o_ref[:, r, :] = words[:, r * 128 : (r + 1) * 128]


def _pack_table(tab):
    e = tab.shape[0]
    return _pack_table_3d(tab).reshape(e * WORD_ROWS, 128)


def _pack_table_3d(tab):
    e = tab.shape[0]
    return pl.pallas_call(
        _pack_kernel,
        grid=(e // PACK_TILE,),
        in_specs=[pl.BlockSpec((PACK_TILE, D_MODEL), lambda i: (i, 0))],
        out_specs=pl.BlockSpec((PACK_TILE, WORD_ROWS, 128), lambda i: (i, 0, 0)),
        out_shape=jax.ShapeDtypeStruct((e, WORD_ROWS, 128), I32),
        compiler_params=_params(48),
        name="pack_table",
    )(tab)


PEER_TILE = 512
GATHER_BUFFERS = 8
GATHER_SPLIT = 2
INDEX_STREAMS = 8
PER_STREAM = PEER_SEL // INDEX_STREAMS
LANE_CHUNKS = D_MODEL // 128


def _gather_rows(idx_refs, tbl_ref, s_ref, t):
    for q in range(PER_STREAM):
        off = t * PER_STREAM + q
        for j in range(INDEX_STREAMS):
            k = q * INDEX_STREAMS + j
            first_row = pl.multiple_of(idx_refs[j][off], WORD_ROWS)
            s_ref[k * WORD_ROWS : (k + 1) * WORD_ROWS, :] = tbl_ref[pl.ds(first_row, WORD_ROWS), :]


def _for_each_token(idx_refs, tbl_ref, bufs, tt, load_group, compute, store_group):
    nb = len(bufs)
    for b in range(nb):
        _gather_rows(idx_refs, tbl_ref, bufs[b], b)

    def trip(p, carry):
        t0 = pl.multiple_of(nb * p, nb)
        rows = load_group(t0)
        outs = []
        for b in range(nb):
            outs.append(compute(bufs[b], rows, b))
            ahead = jnp.minimum(t0 + nb + b, tt - 1)
            _gather_rows(idx_refs, tbl_ref, bufs[b], ahead)
        store_group(t0, outs)
        return carry

    lax.fori_loop(0, tt // nb, trip, 0)


def _unpack_bf16(words):
    hi = pltpu.bitcast(words & jnp.int32(-65536), F32).astype(BF16)
    lo = pltpu.bitcast(words << 16, F32).astype(BF16)
    return hi, lo


def _diag_masks():
    j = lax.broadcasted_iota(I32, (2 * WORD_ROWS, PEER_SEL * WORD_ROWS), 0)
    m = lax.broadcasted_iota(I32, (2 * WORD_ROWS, PEER_SEL * WORD_ROWS), 1)
    r = m % WORD_ROWS
    return (j < WORD_ROWS) & (r == j), (j >= WORD_ROWS) & (r == j - WORD_ROWS)


def _peer_u_kernel(*refs):
    idx_refs = refs[:INDEX_STREAMS]
    xn_ref, gate_ref, tbl_ref, fold_ref, w_ref = refs[INDEX_STREAMS : INDEX_STREAMS + 5]
    scratch = refs[INDEX_STREAMS + 5 :]
    bufs, part_ref = scratch[:-1], scratch[-1]
    tt = xn_ref.shape[0]
    mask_hi, mask_lo = _diag_masks()
    nt = (((1,), (1,)), ((), ()))
    chunk = PEER_SEL * WORD_ROWS // GATHER_SPLIT

    def load_group(t0):
        return xn_ref[pl.ds(t0, len(bufs)), :]

    def compute(s_ref, xrows, b):
        x8 = jnp.concatenate([xrows[b : b + 1, j * 128 : (j + 1) * 128] for j in range(LANE_CHUNKS)],
                             axis=0).astype(BF16)
        parts = []
        for c in range(GATHER_SPLIT):
            rows = slice(c * chunk, (c + 1) * chunk)
            hi, lo = _unpack_bf16(s_ref[rows, :])
            z = (jnp.where(mask_hi[:, rows], lax.dot_general(x8, hi, nt, preferred_element_type=F32), 0.0)
                 + jnp.where(mask_lo[:, rows], lax.dot_general(x8, lo, nt, preferred_element_type=F32), 0.0))
            parts.append(jnp.sum(z, axis=0, keepdims=True))
        return jnp.concatenate(parts, axis=-1)

    def store_group(t0, outs):
        part_ref[pl.ds(t0, len(bufs)), :] = jnp.concatenate(outs, axis=0)

    _for_each_token(idx_refs, tbl_ref, bufs, tt, load_group, compute, store_group)
    v = part_ref[...]
    vh = v.astype(BF16)
    vl = (v - vh.astype(F32)).astype(BF16)
    a = (jnp.dot(vh, fold_ref[...], preferred_element_type=F32)
         + jnp.dot(vl, fold_ref[...], preferred_element_type=F32))
    w_ref[...] = gate_ref[...] * (0.5 * a * (1.0 + lax.erf(a * (1.0 / math.sqrt(2.0)))))


def _peer_v_kernel(*refs):
    idx_refs = refs[:INDEX_STREAMS]
    w_ref, x_ref, tbl_ref, spread_ref, o_ref = refs[INDEX_STREAMS : INDEX_STREAMS + 5]
    scratch = refs[INDEX_STREAMS + 5 :]
    bufs, wrep_ref = scratch[:-1], scratch[-1]
    tt = w_ref.shape[0]
    mask_hi, mask_lo = _diag_masks()
    chunk = PEER_SEL * WORD_ROWS // GATHER_SPLIT
    wrep_ref[...] = jnp.dot(w_ref[...].astype(BF16), spread_ref[...], preferred_element_type=F32)

    def load_group(t0):
        return wrep_ref[pl.ds(t0, len(bufs)), :]

    def compute(s_ref, wrows, b):
        wr = wrows[b : b + 1, :]
        w_hi = jnp.where(mask_hi, wr, 0.0).astype(BF16)
        w_lo = jnp.where(mask_lo, wr, 0.0).astype(BF16)
        o8 = jnp.zeros((LANE_CHUNKS, 128), F32)
        for c in range(GATHER_SPLIT):
            rows = slice(c * chunk, (c + 1) * chunk)
            hi, lo = _unpack_bf16(s_ref[rows, :])
            o8 = (o8 + jnp.dot(w_hi[:, rows], hi, preferred_element_type=F32)
                  + jnp.dot(w_lo[:, rows], lo, preferred_element_type=F32))
        return o8

    def store_group(t0, outs):
        rows = pl.ds(t0, len(bufs))
        for j in range(LANE_CHUNKS):
            cols = slice(j * 128, (j + 1) * 128)
            delta = jnp.concatenate([o8[j : j + 1, :] for o8 in outs], axis=0)
            o_ref[rows, cols] = x_ref[rows, cols] + delta

    _for_each_token(idx_refs, tbl_ref, bufs, tt, load_group, compute, store_group)


def _peer_tables_call(kernel, idx_streams, row_inputs, table, const, out_width, scratch, name):
    ntok = row_inputs[0].shape[0]
    tt = PEER_TILE
    row_spec = lambda a: pl.BlockSpec((tt, a.shape[1]), lambda i: (i, 0))
    return pl.pallas_call(
        kernel,
        grid=(ntok // tt,),
        in_specs=[pl.BlockSpec((tt * PER_STREAM,), lambda i: (i,), memory_space=pltpu.SMEM)] * INDEX_STREAMS
        + [row_spec(a) for a in row_inputs]
        + [pl.BlockSpec(table.shape, lambda i: (0, 0), pipeline_mode=pl.Buffered(1)),
           pl.BlockSpec(const.shape, lambda i: (0, 0))],
        out_specs=pl.BlockSpec((tt, out_width), lambda i: (i, 0)),
        out_shape=jax.ShapeDtypeStruct((ntok, out_width), F32),
        scratch_shapes=scratch,
        compiler_params=_params(56),
        name=name,
    )(*idx_streams, *row_inputs, table, const)


def _peer(x, g, wq, keys, u_packed, v_packed):
    ntok = x.shape[0]
    xn, idx_t, gate_t = _route(x, g, wq, keys)
    idx_streams = [jnp.transpose(idx_t[j::INDEX_STREAMS]).reshape(ntok * PER_STREAM)
                   for j in range(INDEX_STREAMS)]
    gates = jnp.transpose(gate_t)
    nrows = PEER_SEL * WORD_ROWS
    m = jnp.arange(nrows)
    fold = (m[:, None] // WORD_ROWS == jnp.arange(PEER_SEL)[None, :]).astype(BF16)
    scratch = [pltpu.VMEM((nrows, 128), I32)] * GATHER_BUFFERS + [pltpu.VMEM((PEER_TILE, nrows), F32)]
    w = _peer_tables_call(_peer_u_kernel, idx_streams, [xn, gates], u_packed, fold, PEER_SEL,
                          scratch, "peer_gather_u")
    return _peer_tables_call(_peer_v_kernel, idx_streams, [w, x], v_packed, jnp.transpose(fold), D_MODEL,
                             scratch, "peer_gather_v")


def _final_kernel(x_ref, g_ref, o_ref):
    o_ref[...] = _rms(x_ref[...], g_ref[...])


def _final_norm(x, g):
    ntok = x.shape[0]
    tm = 1024
    return pl.pallas_call(
        _final_kernel,
        grid=(ntok // tm,),
        in_specs=[pl.BlockSpec((tm, D_MODEL), lambda i: (i, 0)), pl.BlockSpec((1, D_MODEL), lambda i: (0, 0))],
        out_specs=pl.BlockSpec((tm, D_MODEL), lambda i: (i, 0)),
        out_shape=jax.ShapeDtypeStruct((ntok, D_MODEL), F32),
        compiler_params=_params(48),
        name="final_norm",
    )(x, g)


def _prepare_layer(p, layer):
    scale = HEAD_DIM ** -0.5
    w_in = p["w_in"][layer]
    qcols = jnp.concatenate([
        jnp.full((W_A,), scale, F32), jnp.ones((2 * W_A,), F32),
        jnp.full((W_B,), scale, F32), jnp.ones((2 * W_B + W_C,), F32)])
    row = lambda a: a.astype(F32).reshape(1, -1)
    return dict(
        norm1=row(p["norm1_g"][layer]),
        w_in=(w_in * qcols[None, :]).astype(BF16),
        na_bias=_na_bias_table(p["rpb_a"][layer]),
        gn_a=row(p["out_norm_a"][layer]),
        gn_b=row(p["out_norm_b"][layer]),
        gn_c=row(p["out_norm_c"][layer]),
        s5=_s5_matrices(p["ssm_a_re"][layer], p["ssm_a_im"][layer], p["ssm_log_step"][layer],
                        p["ssm_b_re"][layer], p["ssm_b_im"][layer], p["ssm_c_re"][layer],
                        p["ssm_c_im"][layer], p["ssm_d"][layer]),
        w_glu=p["w_glu"][layer].astype(BF16),
        b_glu=row(p["b_glu"][layer]),
        w_out=p["w_out"][layer].astype(BF16),
        norm2=row(p["norm2_g"][layer]),
        wq=p["peer_wq"][layer].astype(BF16),
        keys=p["peer_keys"][layer].astype(BF16).reshape(PEER_HEADS * 2, PEER_KEYS, PEER_DKEY // 2),
        u=_pack_table(p["peer_u"][layer]),
        v=_pack_table(p["peer_v"][layer]),
    )


def _trunk(x3, layers, rel_bias, final_g):
    n, l, _ = x3.shape
    x = x3.reshape(n * l, D_MODEL)
    for lp in layers:
        a3, b3, uc = _inproj(x, lp["norm1"], lp["w_in"])
        oa = _na_attention(a3.reshape(n, l, 3 * W_A), lp["na_bias"], lp["gn_a"]).reshape(n * l, W_A)
        ob = _dilated_mixture(b3.reshape(n, l, 3 * W_B), rel_bias, lp["gn_b"])
        yc = _s5(uc, n, l, lp["s5"])
        x = _outproj(x, oa, ob, yc, lp["w_out"], lp["w_glu"], lp["b_glu"], lp["gn_c"])
        x = _peer(x, lp["norm2"], lp["wq"], lp["keys"], lp["u"], lp["v"])
    return _final_norm(x, final_g).reshape(n, l, D_MODEL)


def kernel(x_prompt, x_sample, norm1_g, w_in, rpb_a, rel_bias, ssm_a_re, ssm_a_im, ssm_log_step, ssm_b_re, ssm_b_im, ssm_c_re, ssm_c_im, ssm_d, w_glu, b_glu, out_norm_a, out_norm_b, out_norm_c, w_out, norm2_g, peer_wq, peer_keys, peer_u, peer_v, final_g):
    p = dict(norm1_g=norm1_g, w_in=w_in, rpb_a=rpb_a, ssm_a_re=ssm_a_re, ssm_a_im=ssm_a_im,
             ssm_log_step=ssm_log_step, ssm_b_re=ssm_b_re, ssm_b_im=ssm_b_im, ssm_c_re=ssm_c_re,
             ssm_c_im=ssm_c_im, ssm_d=ssm_d, w_glu=w_glu, b_glu=b_glu, out_norm_a=out_norm_a,
             out_norm_b=out_norm_b, out_norm_c=out_norm_c, w_out=w_out, norm2_g=norm2_g,
             peer_wq=peer_wq, peer_keys=peer_keys, peer_u=peer_u, peer_v=peer_v)
    layers = [_prepare_layer(p, layer) for layer in range(DEPTH)]
    fg = final_g.astype(F32).reshape(1, D_MODEL)
    return (_trunk(x_prompt, layers, rel_bias, fg), _trunk(x_sample, layers, rel_bias, fg))
```

```python
import functools
import math

import jax
import jax.numpy as jnp
from jax import lax
from jax.experimental import pallas as pl
from jax.experimental.pallas import tpu as pltpu

F32 = jnp.float32
BF16 = jnp.bfloat16
I32 = jnp.int32

D_MODEL = 1024
DEPTH = 4
GRID_W = 64
HEAD_DIM = 64
N_HEADS_A = 4
W_A = N_HEADS_A * HEAD_DIM
NA_ROWS = 8
NA_COLS = 16
N_HEADS_B = 8
W_B = N_HEADS_B * HEAD_DIM
DILATIONS = ((128, 1), (512, 4), (2048, 16))
SSM_GROUP = 16
W_C = 256
N_GROUPS_C = W_C // SSM_GROUP
SSM_STATE = 64
NUM_BUCKETS = 32
MAX_DISTANCE = 1024
PEER_HEADS = 8
PEER_KEYS = 128
PEER_EXPERTS = PEER_KEYS * PEER_KEYS
PEER_TOPK = 16
PEER_DKEY = 256
PEER_SEL = PEER_HEADS * PEER_TOPK
EPS = 1e-6
NEG = -1e30

BAND = 64
S5_CHUNK = 32
S5_PAD = 128
HALF = D_MODEL // 2
WORD_ROWS = HALF // 128

MIB = 1024 * 1024


def _params(vmem_mib, n_axes=1):
    return pltpu.CompilerParams(
        vmem_limit_bytes=vmem_mib * MIB,
        dimension_semantics=("arbitrary",) * n_axes,
    )


def _rms(x, g):
    return x * lax.rsqrt(jnp.mean(x * x, axis=-1, keepdims=True) + EPS) * g


def _inproj_kernel(x_ref, g_ref, w_ref, a_ref, b_ref, c_ref):
    h = _rms(x_ref[...], g_ref[...]).astype(BF16)
    p = jnp.dot(h, w_ref[...], preferred_element_type=F32)
    a_ref[...] = p[:, : 3 * W_A].astype(BF16)
    b_ref[...] = p[:, 3 * W_A : 3 * W_A + 3 * W_B].astype(BF16)
    c_ref[...] = p[:, 3 * W_A + 3 * W_B :]


def _inproj(x, g, w):
    ntok = x.shape[0]
    tm = 512
    wcols = w.shape[1]
    return pl.pallas_call(
        _inproj_kernel,
        grid=(ntok // tm,),
        in_specs=[
            pl.BlockSpec((tm, D_MODEL), lambda i: (i, 0)),
            pl.BlockSpec((1, D_MODEL), lambda i: (0, 0)),
            pl.BlockSpec((D_MODEL, wcols), lambda i: (0, 0)),
        ],
        out_specs=[
            pl.BlockSpec((tm, 3 * W_A), lambda i: (i, 0)),
            pl.BlockSpec((tm, 3 * W_B), lambda i: (i, 0)),
            pl.BlockSpec((tm, W_C), lambda i: (i, 0)),
        ],
        out_shape=[
            jax.ShapeDtypeStruct((ntok, 3 * W_A), BF16),
            jax.ShapeDtypeStruct((ntok, 3 * W_B), BF16),
            jax.ShapeDtypeStruct((ntok, W_C), F32),
        ],
        compiler_params=_params(48),
        name="inproj",
    )(x, g, w)


def _paired_attention(q_ref, key_refs, value_refs, pairs, bias_of):
    pair = 2 * HEAD_DIM
    first = lax.broadcasted_iota(I32, (1, pair), 1) < HEAD_DIM
    nt = (((1,), (1,)), ((), ()))
    scores, values = [], []
    for p in pairs:
        cols = slice(p * pair, (p + 1) * pair)
        q2 = q_ref[0, :, cols]
        kw = jnp.concatenate([r[0, :, cols] for r in key_refs], axis=0)
        values.append(jnp.concatenate([r[0, :, cols] for r in value_refs], axis=0))
        for half in range(2):
            qm = jnp.where(first if half == 0 else jnp.logical_not(first), q2, jnp.zeros_like(q2))
            scores.append(lax.dot_general(qm, kw, nt, preferred_element_type=F32) + bias_of(2 * p + half))
    probs, sums, maxes = [], [], []
    for s in scores:
        m = jnp.max(s, axis=-1, keepdims=True)
        e = jnp.exp(s - m)
        probs.append(e.astype(BF16))
        sums.append(jnp.sum(e, axis=-1, keepdims=True))
        maxes.append(m)
    res = []
    for n, vw in enumerate(values):
        o = [jnp.dot(probs[2 * n + h], vw, preferred_element_type=F32) / sums[2 * n + h] for h in range(2)]
        lse = [maxes[2 * n + h] + jnp.log(sums[2 * n + h]) for h in range(2)]
        res.append((jnp.where(first, o[0], o[1]), jnp.where(first, lse[0], lse[1])))
    return res


NA_TILE_ROWS = NA_ROWS // 2


def _na_kernel(q_ref, kp_ref, k_ref, kn_ref, vp_ref, v_ref, vn_ref, b_ref, g_ref, o_ref):
    res = _paired_attention(q_ref, (kp_ref, k_ref, kn_ref), (vp_ref, v_ref, vn_ref),
                            range(N_HEADS_A // 2), lambda h: b_ref[0, h])
    o_ref[0] = _rms(jnp.concatenate([o for o, _ in res], axis=-1), g_ref[...]).astype(BF16)


def _na_bias_table(rpb):
    tr = NA_TILE_ROWS
    c = jnp.arange(GRID_W)
    col_start = jnp.clip(c - NA_COLS // 2, 0, GRID_W - NA_COLS)
    col_ok = (c[None, :] >= col_start[:, None]) & (c[None, :] < col_start[:, None] + NA_COLS)
    col_off = jnp.clip(c[None, :] - c[:, None], -(NA_COLS - 1), NA_COLS - 1) + (NA_COLS - 1)
    j = jnp.arange(tr)[:, None]
    kr = jnp.arange(3 * tr)[None, :]
    row_off = kr - j + (NA_ROWS - 1 - tr)
    windows = jnp.stack([(kr >= tr) & (kr < tr + NA_ROWS) & (j >= 0),
                         (kr >= j) & (kr < j + NA_ROWS),
                         (kr < NA_ROWS) & (j >= 0)])
    t = rpb[:, jnp.clip(row_off, 0, 2 * NA_ROWS - 2)][:, :, :, col_off]
    ok = windows[:, None, :, :, None, None] & col_ok[None, None, None, None]
    t = jnp.where(ok, t[None], NEG)
    t = jnp.transpose(t, (0, 1, 2, 4, 3, 5))
    return t.reshape(3, N_HEADS_A, tr * GRID_W, 3 * tr * GRID_W).astype(F32)


def _na_attention(a3, bias, g):
    n, l, _ = a3.shape
    tq = NA_TILE_ROWS * GRID_W
    n_tiles = l // tq
    assert l % tq == 0 and n_tiles >= 3 and 2 * NA_TILE_ROWS == NA_ROWS
    prev = lambda part: pl.BlockSpec((1, tq, W_A), lambda b, i: (b, jnp.maximum(i - 1, 0), part))
    cur = lambda part: pl.BlockSpec((1, tq, W_A), lambda b, i: (b, i, part))
    nxt = lambda part: pl.BlockSpec((1, tq, W_A), lambda b, i: (b, jnp.minimum(i + 1, n_tiles - 1), part))
    variant = lambda b, i: (jnp.where(i == 0, 0, jnp.where(i == n_tiles - 1, 2, 1)), 0, 0, 0)
    return pl.pallas_call(
        _na_kernel,
        grid=(n, n_tiles),
        in_specs=[cur(0), prev(1), cur(1), nxt(1), prev(2), cur(2), nxt(2),
                  pl.BlockSpec((1,) + bias.shape[1:], variant),
                  pl.BlockSpec((1, W_A), lambda b, i: (0, 0))],
        out_specs=cur(0),
        out_shape=jax.ShapeDtypeStruct((n, l, W_A), BF16),
        compiler_params=_params(48, 2),
        name="na_attention",
    )(a3, a3, a3, a3, a3, a3, a3, bias, g)


BAND_PAIRS_PER_GROUP = 2


def _band_kernel(q_ref, k_ref, v_ref, kp_ref, vp_ref, kn_ref, vn_ref, b_ref, o_ref, lse_ref, *, n_tiles):
    i = pl.program_id(1)
    tq = q_ref.shape[1]
    kwid = tq + 2 * BAND
    col = lax.broadcasted_iota(I32, (1, kwid), 1)
    lo = jnp.where(i == 0, BAND, 0)
    hi = jnp.where(i == n_tiles - 1, tq + BAND, kwid)
    edge = jnp.where((col >= lo) & (col < hi), 0.0, NEG)
    pair = 2 * HEAD_DIM
    for p0 in range(0, N_HEADS_B // 2, BAND_PAIRS_PER_GROUP):
        pairs = range(p0, p0 + BAND_PAIRS_PER_GROUP)
        res = _paired_attention(q_ref, (kp_ref, k_ref, kn_ref), (vp_ref, v_ref, vn_ref), pairs,
                                lambda h: b_ref[h] + edge)
        for p, (o, lse) in zip(pairs, res):
            o_ref[0, :, p * pair : (p + 1) * pair] = o
            lse_ref[0, :, p * pair : (p + 1) * pair] = lse


def _t5_bucket(rel):
    nb = NUM_BUCKETS // 2
    ret = jnp.where(rel > 0, nb, 0)
    n = jnp.abs(rel)
    max_exact = nb // 2
    nf = jnp.maximum(n, 1).astype(F32)
    large = max_exact + (jnp.log(nf / max_exact) / math.log(MAX_DISTANCE / max_exact)
                         * (nb - max_exact)).astype(I32)
    large = jnp.minimum(large, nb - 1)
    return ret + jnp.where(n < max_exact, n, large)


BAND_BLOCKS_PER_TILE = 4


def _band_tile(ls):
    assert ls % BAND == 0
    return min(BAND_BLOCKS_PER_TILE, ls // BAND) * BAND


def _band_bias_table(rel_bias, dilation, tq):
    qi = jnp.arange(tq)
    ki = jnp.arange(tq + 2 * BAND) - BAND
    rel = ki[None, :] - qi[:, None]
    inside = jnp.abs(rel) <= BAND
    bias = jnp.transpose(rel_bias[_t5_bucket(jnp.where(inside, rel, 0) * dilation)], (2, 0, 1)).astype(F32)
    return jnp.where(inside[None], bias, NEG)


def _band_attention(qkv, bias):
    n, ls, _ = qkv.shape
    tq = _band_tile(ls)
    blocks_per_tile = tq // BAND
    n_blocks = ls // BAND
    assert ls % tq == 0 and bias.shape == (N_HEADS_B, tq, tq + 2 * BAND)
    kern = functools.partial(_band_kernel, n_tiles=ls // tq)
    cur = lambda part: pl.BlockSpec((1, tq, W_B), lambda b, i: (b, i, part))
    prev = lambda part: pl.BlockSpec(
        (1, BAND, W_B), lambda b, i: (b, jnp.maximum(i * blocks_per_tile - 1, 0), part))
    nxt = lambda part: pl.BlockSpec(
        (1, BAND, W_B), lambda b, i: (b, jnp.minimum((i + 1) * blocks_per_tile, n_blocks - 1), part))
    return pl.pallas_call(
        kern,
        grid=(n, ls // tq),
        in_specs=[cur(0), cur(1), cur(2), prev(1), prev(2), nxt(1), nxt(2),
                  pl.BlockSpec(bias.shape, lambda b, i: (0, 0, 0))],
        out_specs=[cur(0), cur(0)],
        out_shape=[jax.ShapeDtypeStruct((n, ls, W_B), F32), jax.ShapeDtypeStruct((n, ls, W_B), F32)],
        compiler_params=_params(48, 2),
        name="band_attention",
    )(qkv, qkv, qkv, qkv, qkv, qkv, qkv, bias)


def _merge_kernel(o1, o2, o3, l1, l2, l3, g_ref, out_ref):
    a, b, c = l1[...], l2[...], l3[...]
    m = jnp.maximum(jnp.maximum(a, b), c)
    ea, eb, ec = jnp.exp(a - m), jnp.exp(b - m), jnp.exp(c - m)
    den = ea + eb + ec
    o = (ea / den) * o1[...] + (eb / den) * o2[...] + (ec / den) * o3[...]
    out_ref[...] = _rms(o, g_ref[...]).astype(BF16)


def _merge(os_, ls_, g):
    ntok = os_[0].shape[0]
    tm = 512
    blk = pl.BlockSpec((tm, W_B), lambda i: (i, 0))
    return pl.pallas_call(
        _merge_kernel,
        grid=(ntok // tm,),
        in_specs=[blk] * 6 + [pl.BlockSpec((1, W_B), lambda i: (0, 0))],
        out_specs=blk,
        out_shape=jax.ShapeDtypeStruct((ntok, W_B), BF16),
        compiler_params=_params(48),
        name="dilated_merge",
    )(*os_, *ls_, g)


def _dilated_mixture(b3, rel_bias, g):
    n, l, width = b3.shape
    outs, lses = [], []
    for window, d in DILATIONS:
        assert window // (2 * d) == BAND and l % d == 0
        ls = l // d
        bias = _band_bias_table(rel_bias, d, _band_tile(ls))

        def to_sub(a):
            return jnp.transpose(a.reshape(n, ls, d, width), (0, 2, 1, 3)).reshape(n * d, ls, width)

        def from_sub(a):
            return jnp.transpose(a.reshape(n, d, ls, W_B), (0, 2, 1, 3)).reshape(n * l, W_B)

        o, lse = _band_attention(b3 if d == 1 else to_sub(b3), bias)
        outs.append(from_sub(o))
        lses.append(from_sub(lse))
    return _merge(outs, lses, g)


def _s5_kernel(u_ref, wt_ref, e_ref, f_ref, lam_ref, d_ref, y_ref, s_scr, h_scr, *, nseq, cps):
    u = u_ref[0]
    ub = u.astype(BF16)
    s_scr[...] = jnp.dot(ub, e_ref[0], preferred_element_type=F32)
    lam = lam_ref[0]
    lfr, lfi = lam[:, 0:S5_PAD], lam[:, S5_PAD : 2 * S5_PAD]
    lbr, lbi = lam[:, 2 * S5_PAD : 3 * S5_PAD], lam[:, 3 * S5_PAD :]
    zero = jnp.zeros((nseq, S5_PAD), F32)
    fr, fi, br, bi = zero, zero, zero, zero
    for c in range(cps):
        rf = slice(c * nseq, (c + 1) * nseq)
        h_scr[rf, 0:S5_PAD] = fr
        h_scr[rf, S5_PAD : 2 * S5_PAD] = fi
        sr, si = s_scr[rf, 0:S5_PAD], s_scr[rf, S5_PAD : 2 * S5_PAD]
        fr, fi = lfr * fr - lfi * fi + sr, lfr * fi + lfi * fr + si
        cb = cps - 1 - c
        rb = slice(cb * nseq, (cb + 1) * nseq)
        h_scr[rb, 2 * S5_PAD : 3 * S5_PAD] = br
        h_scr[rb, 3 * S5_PAD :] = bi
        sr, si = s_scr[rb, 2 * S5_PAD : 3 * S5_PAD], s_scr[rb, 3 * S5_PAD :]
        br, bi = lbr * br - lbi * bi + sr, lbr * bi + lbi * br + si
    y = jnp.dot(ub, wt_ref[0], preferred_element_type=F32)
    y = y + jnp.dot(h_scr[...].astype(BF16), f_ref[0], preferred_element_type=F32)
    y_ref[0] = y + d_ref[0] * u


def _s5_matrices(a_re, a_im, log_step, b_re, b_im, c_re, c_im, d_skip):
    hp = lax.Precision.HIGHEST
    t, g16, p = S5_CHUNK, SSM_GROUP, SSM_STATE
    step = jnp.exp(log_step.astype(F32))[..., None]
    ar, ai = a_re.astype(F32), a_im.astype(F32)
    decay = jnp.exp(ar * step)
    lb_re = decay * jnp.cos(ai * step)
    lb_im = decay * jnp.sin(ai * step)
    nr = lb_re - 1.0
    den = ar * ar + ai * ai
    z_re = (nr * ar + lb_im * ai) / den
    z_im = (lb_im * ar - nr * ai) / den
    br, bi = b_re.astype(F32), b_im.astype(F32)
    bb_re = z_re[..., None] * br - z_im[..., None] * bi
    bb_im = z_re[..., None] * bi + z_im[..., None] * br
    cr, ci = c_re.astype(F32), c_im.astype(F32)
    def pw_step(carry, _):
        pr, pi = carry
        return (pr * lb_re - pi * lb_im, pr * lb_im + pi * lb_re), (pr, pi)
    _, (pw_re, pw_im) = lax.scan(pw_step, (jnp.ones_like(lb_re), jnp.zeros_like(lb_im)), None, length=t + 1)
    pw_re = jnp.moveaxis(pw_re, 0, 2)
    pw_im = jnp.moveaxis(pw_im, 0, 2)
    lbb_re = pw_re[..., None] * bb_re[:, :, None] - pw_im[..., None] * bb_im[:, :, None]
    lbb_im = pw_re[..., None] * bb_im[:, :, None] + pw_im[..., None] * bb_re[:, :, None]
    taps = (jnp.einsum("dgop,dgkpi->dgkoi", cr, lbb_re[:, :, :t], precision=hp)
            - jnp.einsum("dgop,dgkpi->dgkoi", ci, lbb_im[:, :, :t], precision=hp))
    r_idx = jnp.arange(t)[:, None]
    s_idx = jnp.arange(t)[None, :]
    lag = s_idx - r_idx
    by_lag = jnp.concatenate([jnp.flip(taps[1][:, 1:], axis=1), taps[0][:, :1] + taps[1][:, :1],
                              taps[0][:, 1:]], axis=1)
    wt = jnp.transpose(by_lag[:, lag + (t - 1)], (0, 1, 4, 2, 3)).reshape(N_GROUPS_C, t * g16, t * g16)
    pad = S5_PAD - p

    def e_part(x):
        x = jnp.transpose(x, (0, 1, 3, 2)).reshape(N_GROUPS_C, t * g16, p)
        return jnp.pad(x, ((0, 0), (0, 0), (0, pad)))

    rev = jnp.arange(t - 1, -1, -1)
    e = jnp.concatenate([e_part(lbb_re[0][:, rev]), e_part(lbb_im[0][:, rev]),
                         e_part(lbb_re[1][:, :t]), e_part(lbb_im[1][:, :t])], axis=-1)
    def f_parts(d, pows):
        gr = cr[d][:, None] * pw_re[d][:, pows][:, :, None] - ci[d][:, None] * pw_im[d][:, pows][:, :, None]
        gi = cr[d][:, None] * pw_im[d][:, pows][:, :, None] + ci[d][:, None] * pw_re[d][:, pows][:, :, None]
        def shp(x):
            x = jnp.transpose(x, (0, 3, 1, 2)).reshape(N_GROUPS_C, p, t * g16)
            return jnp.pad(x, ((0, 0), (0, pad), (0, 0)))
        return shp(gr), shp(-gi)

    f_fr, f_fi = f_parts(0, jnp.arange(1, t + 1))
    f_br, f_bi = f_parts(1, jnp.arange(t, 0, -1))
    f = jnp.concatenate([f_fr, f_fi, f_br, f_bi], axis=1)

    def lam_part(x):
        return jnp.pad(x, ((0, 0), (0, pad)))[:, None, :]

    lam = jnp.concatenate([lam_part(pw_re[0][:, t]), lam_part(pw_im[0][:, t]),
                           lam_part(pw_re[1][:, t]), lam_part(pw_im[1][:, t])], axis=-1)
    dvec = jnp.tile(d_skip.astype(F32).reshape(N_GROUPS_C, 1, g16), (1, t, 1)).reshape(N_GROUPS_C, 1, t * g16)
    return wt.astype(BF16), e.astype(BF16), f.astype(BF16), lam, dvec


def _s5(uc, n, l, mats):
    wt, e, f, lam, dvec = mats
    t, g16 = S5_CHUNK, SSM_GROUP
    assert l % t == 0
    cps = l // t
    nc = cps * n
    width = t * g16
    u2 = jnp.transpose(uc.reshape(n, cps, t, N_GROUPS_C, g16), (3, 1, 0, 2, 4)).reshape(N_GROUPS_C, nc, width)
    kern = functools.partial(_s5_kernel, nseq=n, cps=cps)
    per_g = lambda shape: pl.BlockSpec((1,) + shape, lambda g: (g, 0, 0))
    y2 = pl.pallas_call(
        kern,
        grid=(N_GROUPS_C,),
        in_specs=[per_g((nc, width)), per_g((width, width)), per_g((width, 4 * S5_PAD)),
                  per_g((4 * S5_PAD, width)), per_g((1, 4 * S5_PAD)), per_g((1, width))],
        out_specs=per_g((nc, width)),
        out_shape=jax.ShapeDtypeStruct((N_GROUPS_C, nc, width), F32),
        scratch_shapes=[pltpu.VMEM((nc, 4 * S5_PAD), F32), pltpu.VMEM((nc, 4 * S5_PAD), F32)],
        compiler_params=_params(48),
        name="s5_chunked",
    )(u2, wt, e, f, lam, dvec)
    return jnp.transpose(y2.reshape(N_GROUPS_C, cps, n, t, g16), (2, 1, 3, 0, 4)).reshape(n * l, W_C)


def _outproj_kernel(x_ref, oa_ref, ob_ref, yc_ref, w_ref, wg_ref, bg_ref, gc_ref, o_ref):
    y = yc_ref[...]
    g = 0.5 * y * (1.0 + lax.erf(y * (1.0 / math.sqrt(2.0))))
    z = jnp.dot(g.astype(BF16), wg_ref[...], preferred_element_type=F32) + bg_ref[...]
    oc = g * (1.0 / (1.0 + jnp.exp(-z)))
    ocn = _rms(oc, gc_ref[...]).astype(BF16)
    acc = jnp.dot(oa_ref[...], w_ref[0:W_A, :], preferred_element_type=F32)
    acc = acc + jnp.dot(ob_ref[...], w_ref[W_A : W_A + W_B, :], preferred_element_type=F32)
    acc = acc + jnp.dot(ocn, w_ref[W_A + W_B :, :], preferred_element_type=F32)
    o_ref[...] = x_ref[...] + acc


def _outproj(x, oa, ob, yc, w, wg, bg, gc):
    ntok = x.shape[0]
    tm = 512
    row = lambda width: pl.BlockSpec((tm, width), lambda i: (i, 0))
    full = lambda a: pl.BlockSpec(a.shape, lambda i: (0, 0))
    return pl.pallas_call(
        _outproj_kernel,
        grid=(ntok // tm,),
        in_specs=[row(D_MODEL), row(W_A), row(W_B), row(W_C), full(w), full(wg), full(bg), full(gc)],
        out_specs=row(D_MODEL),
        out_shape=jax.ShapeDtypeStruct((ntok, D_MODEL), F32),
        compiler_params=_params(48),
        name="outproj",
    )(x, oa, ob, yc, w, wg, bg, gc)


def _topk_rows_many(arrays, k, iota, sentinel):
    arrays = list(arrays)
    out_v = [[] for _ in arrays]
    out_i = [[] for _ in arrays]
    for _ in range(k):
        for n, vals in enumerate(arrays):
            m = jnp.max(vals, axis=0, keepdims=True)
            am = jnp.min(jnp.where(vals == m, iota, sentinel), axis=0, keepdims=True)
            out_v[n].append(m)
            out_i[n].append(am)
            arrays[n] = jnp.where(iota == am, -jnp.inf, vals)
    return [(jnp.concatenate(v, axis=0), jnp.concatenate(i, axis=0)) for v, i in zip(out_v, out_i)]


def _topk_rows(vals, k, iota, sentinel):
    return _topk_rows_many([vals], k, iota, sentinel)[0]


_CAND_BLOCKS = ([(0, 1, 0, PEER_TOPK)] + [(a, a + 1, 0, 8) for a in range(1, 8)] + [(8, PEER_TOPK, 0, 1)])


def _candidate_positions(tm):
    cols = []
    for a0, a1, b0, b1 in _CAND_BLOCKS:
        cols += [a * PEER_TOPK + b for a in range(a0, a1) for b in range(b0, b1)]
    covered = set(cols)
    assert all(a * PEER_TOPK + b in covered for a in range(PEER_TOPK) for b in range(PEER_TOPK)
               if (a + 1) * (b + 1) <= PEER_TOPK)
    return jnp.broadcast_to(jnp.asarray(cols, F32)[:, None], (len(cols), tm))


def _candidate_rows(first, second, combine):
    return jnp.concatenate([combine(first[a0:a1], second[b0:b1]) for a0, a1, b0, b1 in _CAND_BLOCKS], axis=0)


def _route_kernel(x_ref, g_ref, wq_ref, keys_ref, pos_ref, xn_ref, idx_ref, gate_ref):
    tm = x_ref.shape[0]
    xn = _rms(x_ref[...], g_ref[...])
    xn_ref[...] = xn
    q = jnp.dot(xn.astype(BF16), wq_ref[...], preferred_element_type=F32).astype(BF16)
    half = PEER_DKEY // 2
    iota_k = lax.broadcasted_iota(I32, (PEER_KEYS, tm), 0).astype(F32)
    cpos = pos_ref[...]
    for h in range(PEER_HEADS):
        scores = []
        for s in range(2):
            qs = q[:, (2 * h + s) * half : (2 * h + s + 1) * half]
            scores.append(lax.dot_general(keys_ref[2 * h + s], qs, (((1,), (1,)), ((), ())),
                                          preferred_element_type=F32))
        (s0, i0), (s1, i1) = _topk_rows_many(scores, PEER_TOPK, iota_k, float(PEER_KEYS))
        cand = _candidate_rows(s0, s1, lambda a, b: a + b)
        cidx = _candidate_rows(i0.astype(I32), i1.astype(I32), lambda a, b: (a * PEER_KEYS + b) * WORD_ROWS)
        best, pos = _topk_rows(cand, PEER_TOPK, cpos, float(PEER_TOPK * PEER_TOPK))
        experts = [jnp.sum(jnp.where(cpos == pos[r : r + 1], cidx, 0), axis=0, keepdims=True)
                   for r in range(PEER_TOPK)]
        ex = jnp.exp(best - best[0:1])
        rows = slice(h * PEER_TOPK, (h + 1) * PEER_TOPK)
        idx_ref[rows, :] = jnp.concatenate(experts, axis=0)
        gate_ref[rows, :] = ex / jnp.sum(ex, axis=0, keepdims=True)


def _route(x, g, wq, keys):
    ntok = x.shape[0]
    tm = 256
    cpos = _candidate_positions(tm)
    return pl.pallas_call(
        _route_kernel,
        grid=(ntok // tm,),
        in_specs=[
            pl.BlockSpec((tm, D_MODEL), lambda i: (i, 0)),
            pl.BlockSpec((1, D_MODEL), lambda i: (0, 0)),
            pl.BlockSpec(wq.shape, lambda i: (0, 0)),
            pl.BlockSpec(keys.shape, lambda i: (0, 0, 0)),
            pl.BlockSpec(cpos.shape, lambda i: (0, 0)),
        ],
        out_specs=[
            pl.BlockSpec((tm, D_MODEL), lambda i: (i, 0)),
            pl.BlockSpec((PEER_SEL, tm), lambda i: (0, i)),
            pl.BlockSpec((PEER_SEL, tm), lambda i: (0, i)),
        ],
        out_shape=[
            jax.ShapeDtypeStruct((ntok, D_MODEL), F32),
            jax.ShapeDtypeStruct((PEER_SEL, ntok), I32),
            jax.ShapeDtypeStruct((PEER_SEL, ntok), F32),
        ],
        compiler_params=_params(48),
        name="peer_route",
    )(x, g, wq, keys, cpos)


PACK_TILE = 512


def _pack_kernel(t_ref, o_ref):
    tb = t_ref[...].astype(BF16).astype(F32)
    hi = pltpu.bitcast(tb[:, :HALF], I32)
    lo = lax.shift_right_logical(pltpu.bitcast(tb[:, HALF:], I32), jnp.int32(16))
    words = hi | lo
    for r in range(WORD_ROWS):
        o_ref[:, r, :] = words[:, r * 128 : (r + 1) * 128]


def _pack_table(tab):
    e = tab.shape[0]
    return _pack_table_3d(tab).reshape(e * WORD_ROWS, 128)


def _pack_table_3d(tab):
    e = tab.shape[0]
    return pl.pallas_call(
        _pack_kernel,
        grid=(e // PACK_TILE,),
        in_specs=[pl.BlockSpec((PACK_TILE, D_MODEL), lambda i: (i, 0))],
        out_specs=pl.BlockSpec((PACK_TILE, WORD_ROWS, 128), lambda i: (i, 0, 0)),
        out_shape=jax.ShapeDtypeStruct((e, WORD_ROWS, 128), I32),
        compiler_params=_params(48),
        name="pack_table",
    )(tab)


PEER_TILE = 512
GATHER_BUFFERS = 8
GATHER_SPLIT = 2
INDEX_STREAMS = 8
PER_STREAM = PEER_SEL // INDEX_STREAMS
LANE_CHUNKS = D_MODEL // 128


def _gather_rows(idx_refs, tbl_ref, s_ref, t):
    for q in range(PER_STREAM):
        off = t * PER_STREAM + q
        for j in range(INDEX_STREAMS):
            k = q * INDEX_STREAMS + j
            first_row = pl.multiple_of(idx_refs[j][off], WORD_ROWS)
            s_ref[k * WORD_ROWS : (k + 1) * WORD_ROWS, :] = tbl_ref[pl.ds(first_row, WORD_ROWS), :]


def _for_each_token(idx_refs, tbl_ref, bufs, tt, load_group, compute, store_group):
    nb = len(bufs)
    for b in range(nb):
        _gather_rows(idx_refs, tbl_ref, bufs[b], b)

    def trip(p, carry):
        t0 = pl.multiple_of(nb * p, nb)
        rows = load_group(t0)
        outs = []
        for b in range(nb):
            outs.append(compute(bufs[b], rows, b))
            ahead = jnp.minimum(t0 + nb + b, tt - 1)
            _gather_rows(idx_refs, tbl_ref, bufs[b], ahead)
        store_group(t0, outs)
        return carry

    lax.fori_loop(0, tt // nb, trip, 0)


def _unpack_bf16(words):
    hi = pltpu.bitcast(words & jnp.int32(-65536), F32).astype(BF16)
    lo = pltpu.bitcast(words << 16, F32).astype(BF16)
    return hi, lo


def _diag_masks():
    j = lax.broadcasted_iota(I32, (2 * WORD_ROWS, PEER_SEL * WORD_ROWS), 0)
    m = lax.broadcasted_iota(I32, (2 * WORD_ROWS, PEER_SEL * WORD_ROWS), 1)
    r = m % WORD_ROWS
    return (j < WORD_ROWS) & (r == j), (j >= WORD_ROWS) & (r == j - WORD_ROWS)


def _peer_u_kernel(*refs):
    idx_refs = refs[:INDEX_STREAMS]
    xn_ref, gate_ref, tbl_ref, fold_ref, w_ref = refs[INDEX_STREAMS : INDEX_STREAMS + 5]
    scratch = refs[INDEX_STREAMS + 5 :]
    bufs, part_ref = scratch[:-1], scratch[-1]
    tt = xn_ref.shape[0]
    mask_hi, mask_lo = _diag_masks()
    nt = (((1,), (1,)), ((), ()))
    chunk = PEER_SEL * WORD_ROWS // GATHER_SPLIT

    def load_group(t0):
        return xn_ref[pl.ds(t0, len(bufs)), :]

    def compute(s_ref, xrows, b):
        x8 = jnp.concatenate([xrows[b : b + 1, j * 128 : (j + 1) * 128] for j in range(LANE_CHUNKS)],
                             axis=0).astype(BF16)
        parts = []
        for c in range(GATHER_SPLIT):
            rows = slice(c * chunk, (c + 1) * chunk)
            hi, lo = _unpack_bf16(s_ref[rows, :])
            z = (jnp.where(mask_hi[:, rows], lax.dot_general(x8, hi, nt, preferred_element_type=F32), 0.0)
                 + jnp.where(mask_lo[:, rows], lax.dot_general(x8, lo, nt, preferred_element_type=F32), 0.0))
            parts.append(jnp.sum(z, axis=0, keepdims=True))
        return jnp.concatenate(parts, axis=-1)

    def store_group(t0, outs):
        part_ref[pl.ds(t0, len(bufs)), :] = jnp.concatenate(outs, axis=0)

    _for_each_token(idx_refs, tbl_ref, bufs, tt, load_group, compute, store_group)
    v = part_ref[...]
    vh = v.astype(BF16)
    vl = (v - vh.astype(F32)).astype(BF16)
    a = (jnp.dot(vh, fold_ref[...], preferred_element_type=F32)
         + jnp.dot(vl, fold_ref[...], preferred_element_type=F32))
    w_ref[...] = gate_ref[...] * (0.5 * a * (1.0 + lax.erf(a * (1.0 / math.sqrt(2.0)))))


def _peer_v_kernel(*refs):
    idx_refs = refs[:INDEX_STREAMS]
    w_ref, x_ref, tbl_ref, spread_ref, o_ref = refs[INDEX_STREAMS : INDEX_STREAMS + 5]
    scratch = refs[INDEX_STREAMS + 5 :]
    bufs, wrep_ref = scratch[:-1], scratch[-1]
    tt = w_ref.shape[0]
    mask_hi, mask_lo = _diag_masks()
    chunk = PEER_SEL * WORD_ROWS // GATHER_SPLIT
    wrep_ref[...] = jnp.dot(w_ref[...].astype(BF16), spread_ref[...], preferred_element_type=F32)

    def load_group(t0):
        return wrep_ref[pl.ds(t0, len(bufs)), :]

    def compute(s_ref, wrows, b):
        wr = wrows[b : b + 1, :]
        w_hi = jnp.where(mask_hi, wr, 0.0).astype(BF16)
        w_lo = jnp.where(mask_lo, wr, 0.0).astype(BF16)
        o8 = jnp.zeros((LANE_CHUNKS, 128), F32)
        for c in range(GATHER_SPLIT):
            rows = slice(c * chunk, (c + 1) * chunk)
            hi, lo = _unpack_bf16(s_ref[rows, :])
            o8 = (o8 + jnp.dot(w_hi[:, rows], hi, preferred_element_type=F32)
                  + jnp.dot(w_lo[:, rows], lo, preferred_element_type=F32))
        return o8

    def store_group(t0, outs):
        rows = pl.ds(t0, len(bufs))
        for j in range(LANE_CHUNKS):
            cols = slice(j * 128, (j + 1) * 128)
            delta = jnp.concatenate([o8[j : j + 1, :] for o8 in outs], axis=0)
            o_ref[rows, cols] = x_ref[rows, cols] + delta

    _for_each_token(idx_refs, tbl_ref, bufs, tt, load_group, compute, store_group)


def _peer_tables_call(kernel, idx_streams, row_inputs, table, const, out_width, scratch, name):
    ntok = row_inputs[0].shape[0]
    tt = PEER_TILE
    row_spec = lambda a: pl.BlockSpec((tt, a.shape[1]), lambda i: (i, 0))
    return pl.pallas_call(
        kernel,
        grid=(ntok // tt,),
        in_specs=[pl.BlockSpec((tt * PER_STREAM,), lambda i: (i,), memory_space=pltpu.SMEM)] * INDEX_STREAMS
        + [row_spec(a) for a in row_inputs]
        + [pl.BlockSpec(table.shape, lambda i: (0, 0), pipeline_mode=pl.Buffered(1)),
           pl.BlockSpec(const.shape, lambda i: (0, 0))],
        out_specs=pl.BlockSpec((tt, out_width), lambda i: (i, 0)),
        out_shape=jax.ShapeDtypeStruct((ntok, out_width), F32),
        scratch_shapes=scratch,
        compiler_params=_params(56),
        name=name,
    )(*idx_streams, *row_inputs, table, const)


def _peer(x, g, wq, keys, u_packed, v_packed):
    ntok = x.shape[0]
    xn, idx_t, gate_t = _route(x, g, wq, keys)
    idx_streams = [jnp.transpose(idx_t[j * PER_STREAM : (j + 1) * PER_STREAM]).reshape(ntok * PER_STREAM)
                   for j in range(INDEX_STREAMS)]
    gates = jnp.transpose(gate_t.reshape(INDEX_STREAMS, PER_STREAM, ntok), (2, 1, 0)).reshape(ntok, PEER_SEL)
    nrows = PEER_SEL * WORD_ROWS
    m = jnp.arange(nrows)
    fold = (m[:, None] // WORD_ROWS == jnp.arange(PEER_SEL)[None, :]).astype(BF16)
    scratch = [pltpu.VMEM((nrows, 128), I32)] * GATHER_BUFFERS + [pltpu.VMEM((PEER_TILE, nrows), F32)]
    w = _peer_tables_call(_peer_u_kernel, idx_streams, [xn, gates], u_packed, fold, PEER_SEL,
                          scratch, "peer_gather_u")
    return _peer_tables_call(_peer_v_kernel, idx_streams, [w, x], v_packed, jnp.transpose(fold), D_MODEL,
                             scratch, "peer_gather_v")


def _final_kernel(x_ref, g_ref, o_ref):
    o_ref[...] = _rms(x_ref[...], g_ref[...])


def _final_norm(x, g):
    ntok = x.shape[0]
    tm = 1024
    return pl.pallas_call(
        _final_kernel,
        grid=(ntok // tm,),
        in_specs=[pl.BlockSpec((tm, D_MODEL), lambda i: (i, 0)), pl.BlockSpec((1, D_MODEL), lambda i: (0, 0))],
        out_specs=pl.BlockSpec((tm, D_MODEL), lambda i: (i, 0)),
        out_shape=jax.ShapeDtypeStruct((ntok, D_MODEL), F32),
        compiler_params=_params(48),
        name="final_norm",
    )(x, g)


def _prepare_layer(p, layer):
    scale = HEAD_DIM ** -0.5
    w_in = p["w_in"][layer]
    qcols = jnp.concatenate([
        jnp.full((W_A,), scale, F32), jnp.ones((2 * W_A,), F32),
        jnp.full((W_B,), scale, F32), jnp.ones((2 * W_B + W_C,), F32)])
    row = lambda a: a.astype(F32).reshape(1, -1)
    return dict(
        norm1=row(p["norm1_g"][layer]),
        w_in=(w_in * qcols[None, :]).astype(BF16),
        na_bias=_na_bias_table(p["rpb_a"][layer]),
        gn_a=row(p["out_norm_a"][layer]),
        gn_b=row(p["out_norm_b"][layer]),
        gn_c=row(p["out_norm_c"][layer]),
        s5=_s5_matrices(p["ssm_a_re"][layer], p["ssm_a_im"][layer], p["ssm_log_step"][layer],
                        p["ssm_b_re"][layer], p["ssm_b_im"][layer], p["ssm_c_re"][layer],
                        p["ssm_c_im"][layer], p["ssm_d"][layer]),
        w_glu=p["w_glu"][layer].astype(BF16),
        b_glu=row(p["b_glu"][layer]),
        w_out=p["w_out"][layer].astype(BF16),
        norm2=row(p["norm2_g"][layer]),
        wq=p["peer_wq"][layer].astype(BF16),
        keys=p["peer_keys"][layer].astype(BF16).reshape(PEER_HEADS * 2, PEER_KEYS, PEER_DKEY // 2),
        u=_pack_table(p["peer_u"][layer]),
        v=_pack_table(p["peer_v"][layer]),
    )


def _trunk(x3, layers, rel_bias, final_g):
    n, l, _ = x3.shape
    x = x3.reshape(n * l, D_MODEL)
    for lp in layers:
        a3, b3, uc = _inproj(x, lp["norm1"], lp["w_in"])
        oa = _na_attention(a3.reshape(n, l, 3 * W_A), lp["na_bias"], lp["gn_a"]).reshape(n * l, W_A)
        ob = _dilated_mixture(b3.reshape(n, l, 3 * W_B), rel_bias, lp["gn_b"])
        yc = _s5(uc, n, l, lp["s5"])
        x = _outproj(x, oa, ob, yc, lp["w_out"], lp["w_glu"], lp["b_glu"], lp["gn_c"])
        x = _peer(x, lp["norm2"], lp["wq"], lp["keys"], lp["u"], lp["v"])
    return _final_norm(x, final_g).reshape(n, l, D_MODEL)


def kernel(x_prompt, x_sample, norm1_g, w_in, rpb_a, rel_bias, ssm_a_re, ssm_a_im, ssm_log_step, ssm_b_re, ssm_b_im, ssm_c_re, ssm_c_im, ssm_d, w_glu, b_glu, out_norm_a, out_norm_b, out_norm_c, w_out, norm2_g, peer_wq, peer_keys, peer_u, peer_v, final_g):
    p = dict(norm1_g=norm1_g, w_in=w_in, rpb_a=rpb_a, ssm_a_re=ssm_a_re, ssm_a_im=ssm_a_im,
             ssm_log_step=ssm_log_step, ssm_b_re=ssm_b_re, ssm_b_im=ssm_b_im, ssm_c_re=ssm_c_re,
             ssm_c_im=ssm_c_im, ssm_d=ssm_d, w_glu=w_glu, b_glu=b_glu, out_norm_a=out_norm_a,
             out_norm_b=out_norm_b, out_norm_c=out_norm_c, w_out=w_out, norm2_g=norm2_g,
             peer_wq=peer_wq, peer_keys=peer_keys, peer_u=peer_u, peer_v=peer_v)
    layers = [_prepare_layer(p, layer) for layer in range(DEPTH)]
    fg = final_g.astype(F32).reshape(1, D_MODEL)
    return (_trunk(x_prompt, layers, rel_bias, fg), _trunk(x_sample, layers, rel_bias, fg))
```

```python
import functools
import math

import jax
import jax.numpy as jnp
from jax import lax
from jax.experimental import pallas as pl
from jax.experimental.pallas import tpu as pltpu

F32 = jnp.float32
BF16 = jnp.bfloat16
I32 = jnp.int32

D_MODEL = 1024
DEPTH = 4
GRID_W = 64
HEAD_DIM = 64
N_HEADS_A = 4
W_A = N_HEADS_A * HEAD_DIM
NA_ROWS = 8
NA_COLS = 16
N_HEADS_B = 8
W_B = N_HEADS_B * HEAD_DIM
DILATIONS = ((128, 1), (512, 4), (2048, 16))
SSM_GROUP = 16
W_C = 256
N_GROUPS_C = W_C // SSM_GROUP
SSM_STATE = 64
NUM_BUCKETS = 32
MAX_DISTANCE = 1024
PEER_HEADS = 8
PEER_KEYS = 128
PEER_EXPERTS = PEER_KEYS * PEER_KEYS
PEER_TOPK = 16
PEER_DKEY = 256
PEER_SEL = PEER_HEADS * PEER_TOPK
EPS = 1e-6
NEG = -1e30

BAND = 64
S5_CHUNK = 32
S5_PAD = 128
HALF = D_MODEL // 2
WORD_ROWS = HALF // 128

MIB = 1024 * 1024


def _params(vmem_mib, n_axes=1):
    return pltpu.CompilerParams(
        vmem_limit_bytes=vmem_mib * MIB,
        dimension_semantics=("arbitrary",) * n_axes,
    )


def _rms(x, g):
    return x * lax.rsqrt(jnp.mean(x * x, axis=-1, keepdims=True) + EPS) * g


def _inproj_kernel(x_ref, g_ref, w_ref, a_ref, b_ref, c_ref):
    h = _rms(x_ref[...], g_ref[...]).astype(BF16)
    p = jnp.dot(h, w_ref[...], preferred_element_type=F32)
    a_ref[...] = p[:, : 3 * W_A].astype(BF16)
    b_ref[...] = p[:, 3 * W_A : 3 * W_A + 3 * W_B].astype(BF16)
    c_ref[...] = p[:, 3 * W_A + 3 * W_B :]


def _inproj(x, g, w):
    ntok = x.shape[0]
    tm = 512
    wcols = w.shape[1]
    return pl.pallas_call(
        _inproj_kernel,
        grid=(ntok // tm,),
        in_specs=[
            pl.BlockSpec((tm, D_MODEL), lambda i: (i, 0)),
            pl.BlockSpec((1, D_MODEL), lambda i: (0, 0)),
            pl.BlockSpec((D_MODEL, wcols), lambda i: (0, 0)),
        ],
        out_specs=[
            pl.BlockSpec((tm, 3 * W_A), lambda i: (i, 0)),
            pl.BlockSpec((tm, 3 * W_B), lambda i: (i, 0)),
            pl.BlockSpec((tm, W_C), lambda i: (i, 0)),
        ],
        out_shape=[
            jax.ShapeDtypeStruct((ntok, 3 * W_A), BF16),
            jax.ShapeDtypeStruct((ntok, 3 * W_B), BF16),
            jax.ShapeDtypeStruct((ntok, W_C), F32),
        ],
        compiler_params=_params(48),
        name="inproj",
    )(x, g, w)


def _paired_attention(q_ref, key_refs, value_refs, pairs, bias_of):
    pair = 2 * HEAD_DIM
    first = lax.broadcasted_iota(I32, (1, pair), 1) < HEAD_DIM
    nt = (((1,), (1,)), ((), ()))
    scores, values = [], []
    for p in pairs:
        cols = slice(p * pair, (p + 1) * pair)
        q2 = q_ref[0, :, cols]
        kw = jnp.concatenate([r[0, :, cols] for r in key_refs], axis=0)
        values.append(jnp.concatenate([r[0, :, cols] for r in value_refs], axis=0))
        for half in range(2):
            qm = jnp.where(first if half == 0 else jnp.logical_not(first), q2, jnp.zeros_like(q2))
            scores.append(lax.dot_general(qm, kw, nt, preferred_element_type=F32) + bias_of(2 * p + half))
    probs, sums, maxes = [], [], []
    for s in scores:
        m = jnp.max(s, axis=-1, keepdims=True)
        e = jnp.exp(s - m)
        probs.append(e.astype(BF16))
        sums.append(jnp.sum(e, axis=-1, keepdims=True))
        maxes.append(m)
    res = []
    for n, vw in enumerate(values):
        o = [jnp.dot(probs[2 * n + h], vw, preferred_element_type=F32) / sums[2 * n + h] for h in range(2)]
        lse = [maxes[2 * n + h] + jnp.log(sums[2 * n + h]) for h in range(2)]
        res.append((jnp.where(first, o[0], o[1]), jnp.where(first, lse[0], lse[1])))
    return res


NA_TILE_ROWS = NA_ROWS // 2


def _na_kernel(q_ref, kp_ref, k_ref, kn_ref, vp_ref, v_ref, vn_ref, b_ref, g_ref, o_ref):
    res = _paired_attention(q_ref, (kp_ref, k_ref, kn_ref), (vp_ref, v_ref, vn_ref),
                            range(N_HEADS_A // 2), lambda h: b_ref[0, h])
    o_ref[0] = _rms(jnp.concatenate([o for o, _ in res], axis=-1), g_ref[...]).astype(BF16)


def _na_bias_table(rpb):
    tr = NA_TILE_ROWS
    c = jnp.arange(GRID_W)
    col_start = jnp.clip(c - NA_COLS // 2, 0, GRID_W - NA_COLS)
    col_ok = (c[None, :] >= col_start[:, None]) & (c[None, :] < col_start[:, None] + NA_COLS)
    col_off = jnp.clip(c[None, :] - c[:, None], -(NA_COLS - 1), NA_COLS - 1) + (NA_COLS - 1)
    j = jnp.arange(tr)[:, None]
    kr = jnp.arange(3 * tr)[None, :]
    row_off = kr - j + (NA_ROWS - 1 - tr)
    windows = jnp.stack([(kr >= tr) & (kr < tr + NA_ROWS) & (j >= 0),
                         (kr >= j) & (kr < j + NA_ROWS),
                         (kr < NA_ROWS) & (j >= 0)])
    t = rpb[:, jnp.clip(row_off, 0, 2 * NA_ROWS - 2)][:, :, :, col_off]
    ok = windows[:, None, :, :, None, None] & col_ok[None, None, None, None]
    t = jnp.where(ok, t[None], NEG)
    t = jnp.transpose(t, (0, 1, 2, 4, 3, 5))
    return t.reshape(3, N_HEADS_A, tr * GRID_W, 3 * tr * GRID_W).astype(F32)


def _na_attention(a3, bias, g):
    n, l, _ = a3.shape
    tq = NA_TILE_ROWS * GRID_W
    n_tiles = l // tq
    assert l % tq == 0 and n_tiles >= 3 and 2 * NA_TILE_ROWS == NA_ROWS
    prev = lambda part: pl.BlockSpec((1, tq, W_A), lambda b, i: (b, jnp.maximum(i - 1, 0), part))
    cur = lambda part: pl.BlockSpec((1, tq, W_A), lambda b, i: (b, i, part))
    nxt = lambda part: pl.BlockSpec((1, tq, W_A), lambda b, i: (b, jnp.minimum(i + 1, n_tiles - 1), part))
    variant = lambda b, i: (jnp.where(i == 0, 0, jnp.where(i == n_tiles - 1, 2, 1)), 0, 0, 0)
    return pl.pallas_call(
        _na_kernel,
        grid=(n, n_tiles),
        in_specs=[cur(0), prev(1), cur(1), nxt(1), prev(2), cur(2), nxt(2),
                  pl.BlockSpec((1,) + bias.shape[1:], variant),
                  pl.BlockSpec((1, W_A), lambda b, i: (0, 0))],
        out_specs=cur(0),
        out_shape=jax.ShapeDtypeStruct((n, l, W_A), BF16),
        compiler_params=_params(48, 2),
        name="na_attention",
    )(a3, a3, a3, a3, a3, a3, a3, bias, g)


BAND_PAIRS_PER_GROUP = 2


def _band_kernel(q_ref, k_ref, v_ref, kp_ref, vp_ref, kn_ref, vn_ref, b_ref, o_ref, lse_ref, *, n_tiles):
    i = pl.program_id(1)
    tq = q_ref.shape[1]
    kwid = tq + 2 * BAND
    col = lax.broadcasted_iota(I32, (1, kwid), 1)
    lo = jnp.where(i == 0, BAND, 0)
    hi = jnp.where(i == n_tiles - 1, tq + BAND, kwid)
    edge = jnp.where((col >= lo) & (col < hi), 0.0, NEG)
    pair = 2 * HEAD_DIM
    for p0 in range(0, N_HEADS_B // 2, BAND_PAIRS_PER_GROUP):
        pairs = range(p0, p0 + BAND_PAIRS_PER_GROUP)
        res = _paired_attention(q_ref, (kp_ref, k_ref, kn_ref), (vp_ref, v_ref, vn_ref), pairs,
                                lambda h: b_ref[h] + edge)
        for p, (o, lse) in zip(pairs, res):
            o_ref[0, :, p * pair : (p + 1) * pair] = o
            lse_ref[0, :, p * pair : (p + 1) * pair] = lse


def _t5_bucket(rel):
    nb = NUM_BUCKETS // 2
    ret = jnp.where(rel > 0, nb, 0)
    n = jnp.abs(rel)
    max_exact = nb // 2
    nf = jnp.maximum(n, 1).astype(F32)
    large = max_exact + (jnp.log(nf / max_exact) / math.log(MAX_DISTANCE / max_exact)
                         * (nb - max_exact)).astype(I32)
    large = jnp.minimum(large, nb - 1)
    return ret + jnp.where(n < max_exact, n, large)


BAND_BLOCKS_PER_TILE = 4


def _band_tile(ls):
    assert ls % BAND == 0
    return min(BAND_BLOCKS_PER_TILE, ls // BAND) * BAND


def _band_bias_table(rel_bias, dilation, tq):
    width = tq + 2 * BAND
    span = tq + width - 1
    rel = jnp.arange(span) - (tq - 1) - BAND
    inside = jnp.abs(rel) <= BAND
    vals = jnp.where(inside[:, None], rel_bias[_t5_bucket(jnp.where(inside, rel, 0) * dilation)], NEG)
    z = jnp.transpose(vals).astype(F32)
    skew = jnp.tile(z, (1, tq + 1))[:, : tq * (span + 1)].reshape(N_HEADS_B, tq, span + 1)
    return skew[:, ::-1, :width]


def _band_attention(qkv, bias):
    n, ls, _ = qkv.shape
    tq = _band_tile(ls)
    blocks_per_tile = tq // BAND
    n_blocks = ls // BAND
    assert ls % tq == 0 and bias.shape == (N_HEADS_B, tq, tq + 2 * BAND)
    kern = functools.partial(_band_kernel, n_tiles=ls // tq)
    cur = lambda part: pl.BlockSpec((1, tq, W_B), lambda b, i: (b, i, part))
    prev = lambda part: pl.BlockSpec(
        (1, BAND, W_B), lambda b, i: (b, jnp.maximum(i * blocks_per_tile - 1, 0), part))
    nxt = lambda part: pl.BlockSpec(
        (1, BAND, W_B), lambda b, i: (b, jnp.minimum((i + 1) * blocks_per_tile, n_blocks - 1), part))
    return pl.pallas_call(
        kern,
        grid=(n, ls // tq),
        in_specs=[cur(0), cur(1), cur(2), prev(1), prev(2), nxt(1), nxt(2),
                  pl.BlockSpec(bias.shape, lambda b, i: (0, 0, 0))],
        out_specs=[cur(0), cur(0)],
        out_shape=[jax.ShapeDtypeStruct((n, ls, W_B), F32), jax.ShapeDtypeStruct((n, ls, W_B), F32)],
        compiler_params=_params(48, 2),
        name="band_attention",
    )(qkv, qkv, qkv, qkv, qkv, qkv, qkv, bias)


def _merge_kernel(o1, o2, o3, l1, l2, l3, g_ref, out_ref):
    a, b, c = l1[...], l2[...], l3[...]
    m = jnp.maximum(jnp.maximum(a, b), c)
    ea, eb, ec = jnp.exp(a - m), jnp.exp(b - m), jnp.exp(c - m)
    den = ea + eb + ec
    o = (ea / den) * o1[...] + (eb / den) * o2[...] + (ec / den) * o3[...]
    out_ref[...] = _rms(o, g_ref[...]).astype(BF16)


def _merge(os_, ls_, g):
    ntok = os_[0].shape[0]
    tm = 512
    blk = pl.BlockSpec((tm, W_B), lambda i: (i, 0))
    return pl.pallas_call(
        _merge_kernel,
        grid=(ntok // tm,),
        in_specs=[blk] * 6 + [pl.BlockSpec((1, W_B), lambda i: (0, 0))],
        out_specs=blk,
        out_shape=jax.ShapeDtypeStruct((ntok, W_B), BF16),
        compiler_params=_params(48),
        name="dilated_merge",
    )(*os_, *ls_, g)


def _dilated_mixture(b3, rel_bias, g):
    n, l, width = b3.shape
    outs, lses = [], []
    for window, d in DILATIONS:
        assert window // (2 * d) == BAND and l % d == 0
        ls = l // d
        bias = _band_bias_table(rel_bias, d, _band_tile(ls))

        def to_sub(a):
            return jnp.transpose(a.reshape(n, ls, d, width), (0, 2, 1, 3)).reshape(n * d, ls, width)

        def from_sub(a):
            return jnp.transpose(a.reshape(n, d, ls, W_B), (0, 2, 1, 3)).reshape(n * l, W_B)

        o, lse = _band_attention(b3 if d == 1 else to_sub(b3), bias)
        outs.append(from_sub(o))
        lses.append(from_sub(lse))
    return _merge(outs, lses, g)


LANES = 128


def _s5_kernel(u_ref, wt_ref, e_ref, f_ref, lam_ref, d_ref, y_ref, *, cps):
    t16, nc = u_ref.shape[0] * u_ref.shape[1], u_ref.shape[2]
    u = u_ref[...].reshape(t16, nc)
    ub = u.astype(BF16)
    s = jnp.dot(e_ref[0], ub, preferred_element_type=F32)
    wide = lambda a: jnp.concatenate([a] * (nc // LANES), axis=1)
    lam = lam_ref[0]
    pad = S5_PAD
    chunk_in_seq = lax.broadcasted_iota(I32, (1, nc), 1) % cps

    def entering_state(sr, si, lr, li, backward):
        lr, li = wide(lr), wide(li)
        hr, hi = sr, si
        shift = 1
        while shift < cps:
            amount = nc - shift if backward else shift
            pr, pi = pltpu.roll(hr, amount, axis=1), pltpu.roll(hi, amount, axis=1)
            ok = (chunk_in_seq < cps - shift) if backward else (chunk_in_seq >= shift)
            hr, hi = (hr + jnp.where(ok, lr * pr - li * pi, 0.0), hi + jnp.where(ok, lr * pi + li * pr, 0.0))
            lr, li = lr * lr - li * li, 2.0 * lr * li
            shift *= 2
        amount = nc - 1 if backward else 1
        ok = (chunk_in_seq < cps - 1) if backward else (chunk_in_seq >= 1)
        return (jnp.where(ok, pltpu.roll(hr, amount, axis=1), 0.0),
                jnp.where(ok, pltpu.roll(hi, amount, axis=1), 0.0))

    fr, fi = entering_state(s[0:pad], s[pad : 2 * pad], lam[0:pad], lam[pad : 2 * pad], False)
    br, bi = entering_state(s[2 * pad : 3 * pad], s[3 * pad :], lam[2 * pad : 3 * pad], lam[3 * pad :], True)
    hin = jnp.concatenate([fr, fi, br, bi], axis=0).astype(BF16)
    y = jnp.dot(wt_ref[0], ub, preferred_element_type=F32)
    y = y + jnp.dot(f_ref[0], hin, preferred_element_type=F32)
    y = y + wide(d_ref[0]) * u
    y_ref[...] = y.reshape(y_ref.shape)


def _s5_matrices(a_re, a_im, log_step, b_re, b_im, c_re, c_im, d_skip):
    hp = lax.Precision.HIGHEST
    t, g16, p = S5_CHUNK, SSM_GROUP, SSM_STATE
    step = jnp.exp(log_step.astype(F32))[..., None]
    ar, ai = a_re.astype(F32), a_im.astype(F32)
    decay = jnp.exp(ar * step)
    lb_re = decay * jnp.cos(ai * step)
    lb_im = decay * jnp.sin(ai * step)
    nr = lb_re - 1.0
    den = ar * ar + ai * ai
    z_re = (nr * ar + lb_im * ai) / den
    z_im = (lb_im * ar - nr * ai) / den
    br, bi = b_re.astype(F32), b_im.astype(F32)
    bb_re = z_re[..., None] * br - z_im[..., None] * bi
    bb_im = z_re[..., None] * bi + z_im[..., None] * br
    cr, ci = c_re.astype(F32), c_im.astype(F32)
    def pw_step(carry, _):
        pr, pi = carry
        return (pr * lb_re - pi * lb_im, pr * lb_im + pi * lb_re), (pr, pi)
    _, (pw_re, pw_im) = lax.scan(pw_step, (jnp.ones_like(lb_re), jnp.zeros_like(lb_im)), None, length=t + 1)
    pw_re = jnp.moveaxis(pw_re, 0, 2)
    pw_im = jnp.moveaxis(pw_im, 0, 2)
    lbb_re = pw_re[..., None] * bb_re[:, :, None] - pw_im[..., None] * bb_im[:, :, None]
    lbb_im = pw_re[..., None] * bb_im[:, :, None] + pw_im[..., None] * bb_re[:, :, None]
    taps = (jnp.einsum("dgop,dgkpi->dgkoi", cr, lbb_re[:, :, :t], precision=hp)
            - jnp.einsum("dgop,dgkpi->dgkoi", ci, lbb_im[:, :, :t], precision=hp))
    r_idx = jnp.arange(t)[:, None]
    s_idx = jnp.arange(t)[None, :]
    lag = s_idx - r_idx
    by_lag = jnp.concatenate([jnp.flip(taps[1][:, 1:], axis=1), taps[0][:, :1] + taps[1][:, :1],
                              taps[0][:, 1:]], axis=1)
    wt = jnp.transpose(by_lag[:, lag + (t - 1)], (0, 1, 4, 2, 3)).reshape(N_GROUPS_C, t * g16, t * g16)
    pad = S5_PAD - p

    def e_part(x):
        x = jnp.transpose(x, (0, 1, 3, 2)).reshape(N_GROUPS_C, t * g16, p)
        return jnp.pad(x, ((0, 0), (0, 0), (0, pad)))

    rev = jnp.arange(t - 1, -1, -1)
    e = jnp.concatenate([e_part(lbb_re[0][:, rev]), e_part(lbb_im[0][:, rev]),
                         e_part(lbb_re[1][:, :t]), e_part(lbb_im[1][:, :t])], axis=-1)
    def f_parts(d, pows):
        gr = cr[d][:, None] * pw_re[d][:, pows][:, :, None] - ci[d][:, None] * pw_im[d][:, pows][:, :, None]
        gi = cr[d][:, None] * pw_im[d][:, pows][:, :, None] + ci[d][:, None] * pw_re[d][:, pows][:, :, None]
        def shp(x):
            x = jnp.transpose(x, (0, 3, 1, 2)).reshape(N_GROUPS_C, p, t * g16)
            return jnp.pad(x, ((0, 0), (0, pad), (0, 0)))
        return shp(gr), shp(-gi)

    f_fr, f_fi = f_parts(0, jnp.arange(1, t + 1))
    f_br, f_bi = f_parts(1, jnp.arange(t, 0, -1))
    f = jnp.concatenate([f_fr, f_fi, f_br, f_bi], axis=1)

    def lam_part(x):
        return jnp.pad(x, ((0, 0), (0, pad)))[:, None, :]

    lam = jnp.concatenate([lam_part(pw_re[0][:, t]), lam_part(pw_im[0][:, t]),
                           lam_part(pw_re[1][:, t]), lam_part(pw_im[1][:, t])], axis=-1)
    dvec = jnp.tile(d_skip.astype(F32).reshape(N_GROUPS_C, 1, g16), (1, t, 1)).reshape(N_GROUPS_C, 1, t * g16)
    tr = lambda a: jnp.transpose(a, (0, 2, 1))
    col = lambda a: jnp.broadcast_to(tr(a), (a.shape[0], a.shape[2], LANES))
    return tr(wt).astype(BF16), tr(e).astype(BF16), tr(f).astype(BF16), col(lam), col(dvec)


def _s5(uc, n, l, mats):
    wt, e, f, lam, dvec = mats
    t, g16 = S5_CHUNK, SSM_GROUP
    assert l % t == 0
    cps = l // t
    nc = cps * n
    assert nc % LANES == 0 and (cps & (cps - 1)) == 0
    ut = jnp.transpose(uc.reshape(nc, t * W_C)).reshape(t, W_C, nc)
    kern = functools.partial(_s5_kernel, cps=cps)
    per_g = lambda a: pl.BlockSpec((1,) + a.shape[1:], lambda g: (g, 0, 0))
    blk = pl.BlockSpec((t, g16, nc), lambda g: (0, g, 0))
    yt = pl.pallas_call(
        kern,
        grid=(N_GROUPS_C,),
        in_specs=[blk, per_g(wt), per_g(e), per_g(f), per_g(lam), per_g(dvec)],
        out_specs=blk,
        out_shape=jax.ShapeDtypeStruct((t, W_C, nc), F32),
        compiler_params=_params(48),
        name="s5_chunked",
    )(ut, wt, e, f, lam, dvec)
    return jnp.transpose(yt.reshape(t * W_C, nc)).reshape(n * l, W_C)


def _outproj_kernel(x_ref, oa_ref, ob_ref, yc_ref, w_ref, wg_ref, bg_ref, gc_ref, o_ref):
    y = yc_ref[...]
    g = 0.5 * y * (1.0 + lax.erf(y * (1.0 / math.sqrt(2.0))))
    z = jnp.dot(g.astype(BF16), wg_ref[...], preferred_element_type=F32) + bg_ref[...]
    oc = g * (1.0 / (1.0 + jnp.exp(-z)))
    ocn = _rms(oc, gc_ref[...]).astype(BF16)
    acc = jnp.dot(oa_ref[...], w_ref[0:W_A, :], preferred_element_type=F32)
    acc = acc + jnp.dot(ob_ref[...], w_ref[W_A : W_A + W_B, :], preferred_element_type=F32)
    acc = acc + jnp.dot(ocn, w_ref[W_A + W_B :, :], preferred_element_type=F32)
    o_ref[...] = x_ref[...] + acc


def _outproj(x, oa, ob, yc, w, wg, bg, gc):
    ntok = x.shape[0]
    tm = 512
    row = lambda width: pl.BlockSpec((tm, width), lambda i: (i, 0))
    full = lambda a: pl.BlockSpec(a.shape, lambda i: (0, 0))
    return pl.pallas_call(
        _outproj_kernel,
        grid=(ntok // tm,),
        in_specs=[row(D_MODEL), row(W_A), row(W_B), row(W_C), full(w), full(wg), full(bg), full(gc)],
        out_specs=row(D_MODEL),
        out_shape=jax.ShapeDtypeStruct((ntok, D_MODEL), F32),
        compiler_params=_params(48),
        name="outproj",
    )(x, oa, ob, yc, w, wg, bg, gc)


def _topk_rows_many(arrays, k, iota, sentinel):
    arrays = list(arrays)
    out_v = [[] for _ in arrays]
    out_i = [[] for _ in arrays]
    for _ in range(k):
        for n, vals in enumerate(arrays):
            m = jnp.max(vals, axis=0, keepdims=True)
            am = jnp.min(jnp.where(vals == m, iota, sentinel), axis=0, keepdims=True)
            out_v[n].append(m)
            out_i[n].append(am)
            arrays[n] = jnp.where(iota == am, -jnp.inf, vals)
    return [(jnp.concatenate(v, axis=0), jnp.concatenate(i, axis=0)) for v, i in zip(out_v, out_i)]


def _topk_rows(vals, k, iota, sentinel):
    return _topk_rows_many([vals], k, iota, sentinel)[0]


_CAND_BLOCKS = ([(0, 1, 0, PEER_TOPK)] + [(a, a + 1, 0, 8) for a in range(1, 8)] + [(8, PEER_TOPK, 0, 1)])


def _candidate_positions(tm):
    cols = []
    for a0, a1, b0, b1 in _CAND_BLOCKS:
        cols += [a * PEER_TOPK + b for a in range(a0, a1) for b in range(b0, b1)]
    covered = set(cols)
    assert all(a * PEER_TOPK + b in covered for a in range(PEER_TOPK) for b in range(PEER_TOPK)
               if (a + 1) * (b + 1) <= PEER_TOPK)
    return jnp.broadcast_to(jnp.asarray(cols, F32)[:, None], (len(cols), tm))


def _candidate_rows(first, second, combine):
    return jnp.concatenate([combine(first[a0:a1], second[b0:b1]) for a0, a1, b0, b1 in _CAND_BLOCKS], axis=0)


def _route_kernel(x_ref, g_ref, wq_ref, keys_ref, pos_ref, xn_ref, idx_ref, gate_ref):
    tm = x_ref.shape[0]
    xn = _rms(x_ref[...], g_ref[...])
    xn_ref[...] = xn
    q = jnp.dot(xn.astype(BF16), wq_ref[...], preferred_element_type=F32).astype(BF16)
    half = PEER_DKEY // 2
    iota_k = lax.broadcasted_iota(I32, (PEER_KEYS, tm), 0).astype(F32)
    cpos = pos_ref[...]
    for h in range(PEER_HEADS):
        scores = []
        for s in range(2):
            qs = q[:, (2 * h + s) * half : (2 * h + s + 1) * half]
            scores.append(lax.dot_general(keys_ref[2 * h + s], qs, (((1,), (1,)), ((), ())),
                                          preferred_element_type=F32))
        (s0, i0), (s1, i1) = _topk_rows_many(scores, PEER_TOPK, iota_k, float(PEER_KEYS))
        cand = _candidate_rows(s0, s1, lambda a, b: a + b)
        cidx = _candidate_rows(i0.astype(I32), i1.astype(I32), lambda a, b: (a * PEER_KEYS + b) * WORD_ROWS)
        best, pos = _topk_rows(cand, PEER_TOPK, cpos, float(PEER_TOPK * PEER_TOPK))
        experts = [jnp.sum(jnp.where(cpos == pos[r : r + 1], cidx, 0), axis=0, keepdims=True)
                   for r in range(PEER_TOPK)]
        ex = jnp.exp(best - best[0:1])
        rows = slice(h * PEER_TOPK, (h + 1) * PEER_TOPK)
        idx_ref[rows, :] = jnp.concatenate(experts, axis=0)
        gate_ref[rows, :] = ex / jnp.sum(ex, axis=0, keepdims=True)


def _route(x, g, wq, keys):
    ntok = x.shape[0]
    tm = 256
    cpos = _candidate_positions(tm)
    return pl.pallas_call(
        _route_kernel,
        grid=(ntok // tm,),
        in_specs=[
            pl.BlockSpec((tm, D_MODEL), lambda i: (i, 0)),
            pl.BlockSpec((1, D_MODEL), lambda i: (0, 0)),
            pl.BlockSpec(wq.shape, lambda i: (0, 0)),
            pl.BlockSpec(keys.shape, lambda i: (0, 0, 0)),
            pl.BlockSpec(cpos.shape, lambda i: (0, 0)),
        ],
        out_specs=[
            pl.BlockSpec((tm, D_MODEL), lambda i: (i, 0)),
            pl.BlockSpec((PEER_SEL, tm), lambda i: (0, i)),
            pl.BlockSpec((PEER_SEL, tm), lambda i: (0, i)),
        ],
        out_shape=[
            jax.ShapeDtypeStruct((ntok, D_MODEL), F32),
            jax.ShapeDtypeStruct((PEER_SEL, ntok), I32),
            jax.ShapeDtypeStruct((PEER_SEL, ntok), F32),
        ],
        compiler_params=_params(48),
        name="peer_route",
    )(x, g, wq, keys, cpos)


PACK_TILE = 512


def _pack_kernel(t_ref, o_ref):
    tb = t_ref[...].astype(BF16).astype(F32)
    hi = pltpu.bitcast(tb[:, :HALF], I32)
    lo = lax.shift_right_logical(pltpu.bitcast(tb[:, HALF:], I32), jnp.int32(16))
    words = hi | lo
    for r in range(WORD_ROWS):
        o_ref[:, r, :] = words[:, r * 128 : (r + 1) * 128]


def _pack_table(tab):
    e = tab.shape[0]
    return _pack_table_3d(tab).reshape(e * WORD_ROWS, 128)


def _pack_table_3d(tab):
    e = tab.shape[0]
    return pl.pallas_call(
        _pack_kernel,
        grid=(e // PACK_TILE,),
        in_specs=[pl.BlockSpec((PACK_TILE, D_MODEL), lambda i: (i, 0))],
        out_specs=pl.BlockSpec((PACK_TILE, WORD_ROWS, 128), lambda i: (i, 0, 0)),
        out_shape=jax.ShapeDtypeStruct((e, WORD_ROWS, 128), I32),
        compiler_params=_params(48),
        name="pack_table",
    )(tab)


PEER_TILE = 512
GATHER_BUFFERS = 8
GATHER_SPLIT = 2
INDEX_STREAMS = 8
PER_STREAM = PEER_SEL // INDEX_STREAMS
LANE_CHUNKS = D_MODEL // 128


def _gather_rows(idx_refs, tbl_ref, s_ref, t):
    for q in range(PER_STREAM):
        off = t * PER_STREAM + q
        for j in range(INDEX_STREAMS):
            k = q * INDEX_STREAMS + j
            first_row = pl.multiple_of(idx_refs[j][off], WORD_ROWS)
            s_ref[k * WORD_ROWS : (k + 1) * WORD_ROWS, :] = tbl_ref[pl.ds(first_row, WORD_ROWS), :]


def _for_each_token(idx_refs, tbl_ref, bufs, tt, load_group, compute, store_group):
    nb = len(bufs)
    for b in range(nb):
        _gather_rows(idx_refs, tbl_ref, bufs[b], b)

    def trip(p, carry):
        t0 = pl.multiple_of(nb * p, nb)
        rows = load_group(t0)
        outs = []
        for b in range(nb):
            outs.append(compute(bufs[b], rows, b))
            ahead = jnp.minimum(t0 + nb + b, tt - 1)
            _gather_rows(idx_refs, tbl_ref, bufs[b], ahead)
        store_group(t0, outs)
        return carry

    lax.fori_loop(0, tt // nb, trip, 0)


def _unpack_bf16(words):
    hi = pltpu.bitcast(words & jnp.int32(-65536), F32).astype(BF16)
    lo = pltpu.bitcast(words << 16, F32).astype(BF16)
    return hi, lo


def _diag_masks():
    j = lax.broadcasted_iota(I32, (2 * WORD_ROWS, PEER_SEL * WORD_ROWS), 0)
    m = lax.broadcasted_iota(I32, (2 * WORD_ROWS, PEER_SEL * WORD_ROWS), 1)
    r = m % WORD_ROWS
    return (j < WORD_ROWS) & (r == j), (j >= WORD_ROWS) & (r == j - WORD_ROWS)


def _peer_u_kernel(*refs):
    idx_refs = refs[:INDEX_STREAMS]
    xn_ref, gate_ref, tbl_ref, fold_ref, w_ref = refs[INDEX_STREAMS : INDEX_STREAMS + 5]
    scratch = refs[INDEX_STREAMS + 5 :]
    bufs, part_ref = scratch[:-1], scratch[-1]
    tt = xn_ref.shape[0]
    mask_hi, mask_lo = _diag_masks()
    nt = (((1,), (1,)), ((), ()))
    chunk = PEER_SEL * WORD_ROWS // GATHER_SPLIT

    def load_group(t0):
        return xn_ref[pl.ds(t0, len(bufs)), :]

    def compute(s_ref, xrows, b):
        x8 = jnp.concatenate([xrows[b : b + 1, j * 128 : (j + 1) * 128] for j in range(LANE_CHUNKS)],
                             axis=0).astype(BF16)
        parts = []
        for c in range(GATHER_SPLIT):
            rows = slice(c * chunk, (c + 1) * chunk)
            hi, lo = _unpack_bf16(s_ref[rows, :])
            z = (jnp.where(mask_hi[:, rows], lax.dot_general(x8, hi, nt, preferred_element_type=F32), 0.0)
                 + jnp.where(mask_lo[:, rows], lax.dot_general(x8, lo, nt, preferred_element_type=F32), 0.0))
            parts.append(jnp.sum(z, axis=0, keepdims=True))
        return jnp.concatenate(parts, axis=-1)

    def store_group(t0, outs):
        part_ref[pl.ds(t0, len(bufs)), :] = jnp.concatenate(outs, axis=0)

    _for_each_token(idx_refs, tbl_ref, bufs, tt, load_group, compute, store_group)
    v = part_ref[...]
    vh = v.astype(BF16)
    vl = (v - vh.astype(F32)).astype(BF16)
    a = (jnp.dot(vh, fold_ref[...], preferred_element_type=F32)
         + jnp.dot(vl, fold_ref[...], preferred_element_type=F32))
    w_ref[...] = gate_ref[...] * (0.5 * a * (1.0 + lax.erf(a * (1.0 / math.sqrt(2.0)))))


def _peer_v_kernel(*refs):
    idx_refs = refs[:INDEX_STREAMS]
    w_ref, x_ref, tbl_ref, spread_ref, o_ref = refs[INDEX_STREAMS : INDEX_STREAMS + 5]
    scratch = refs[INDEX_STREAMS + 5 :]
    bufs, wrep_ref = scratch[:-1], scratch[-1]
    tt = w_ref.shape[0]
    mask_hi, mask_lo = _diag_masks()
    chunk = PEER_SEL * WORD_ROWS // GATHER_SPLIT
    wrep_ref[...] = jnp.dot(w_ref[...].astype(BF16), spread_ref[...], preferred_element_type=F32)

    def load_group(t0):
        return wrep_ref[pl.ds(t0, len(bufs)), :]

    def compute(s_ref, wrows, b):
        wr = wrows[b : b + 1, :]
        w_hi = jnp.where(mask_hi, wr, 0.0).astype(BF16)
        w_lo = jnp.where(mask_lo, wr, 0.0).astype(BF16)
        o8 = jnp.zeros((LANE_CHUNKS, 128), F32)
        for c in range(GATHER_SPLIT):
            rows = slice(c * chunk, (c + 1) * chunk)
            hi, lo = _unpack_bf16(s_ref[rows, :])
            o8 = (o8 + jnp.dot(w_hi[:, rows], hi, preferred_element_type=F32)
                  + jnp.dot(w_lo[:, rows], lo, preferred_element_type=F32))
        return o8

    def store_group(t0, outs):
        rows = pl.ds(t0, len(bufs))
        for j in range(LANE_CHUNKS):
            cols = slice(j * 128, (j + 1) * 128)
            delta = jnp.concatenate([o8[j : j + 1, :] for o8 in outs], axis=0)
            o_ref[rows, cols] = x_ref[rows, cols] + delta

    _for_each_token(idx_refs, tbl_ref, bufs, tt, load_group, compute, store_group)


def _peer_tables_call(kernel, idx_streams, row_inputs, table, const, out_width, scratch, name):
    ntok = row_inputs[0].shape[0]
    tt = PEER_TILE
    row_spec = lambda a: pl.BlockSpec((tt, a.shape[1]), lambda i: (i, 0))
    return pl.pallas_call(
        kernel,
        grid=(ntok // tt,),
        in_specs=[pl.BlockSpec((tt * PER_STREAM,), lambda i: (i,), memory_space=pltpu.SMEM)] * INDEX_STREAMS
        + [row_spec(a) for a in row_inputs]
        + [pl.BlockSpec(table.shape, lambda i: (0, 0), pipeline_mode=pl.Buffered(1)),
           pl.BlockSpec(const.shape, lambda i: (0, 0))],
        out_specs=pl.BlockSpec((tt, out_width), lambda i: (i, 0)),
        out_shape=jax.ShapeDtypeStruct((ntok, out_width), F32),
        scratch_shapes=scratch,
        compiler_params=_params(56),
        name=name,
    )(*idx_streams, *row_inputs, table, const)


def _peer(x, g, wq, keys, u_packed, v_packed):
    ntok = x.shape[0]
    xn, idx_t, gate_t = _route(x, g, wq, keys)
    idx_streams = [jnp.transpose(idx_t[j * PER_STREAM : (j + 1) * PER_STREAM]).reshape(ntok * PER_STREAM)
                   for j in range(INDEX_STREAMS)]
    gates = jnp.transpose(gate_t.reshape(INDEX_STREAMS, PER_STREAM, ntok), (2, 1, 0)).reshape(ntok, PEER_SEL)
    nrows = PEER_SEL * WORD_ROWS
    m = jnp.arange(nrows)
    fold = (m[:, None] // WORD_ROWS == jnp.arange(PEER_SEL)[None, :]).astype(BF16)
    scratch = [pltpu.VMEM((nrows, 128), I32)] * GATHER_BUFFERS + [pltpu.VMEM((PEER_TILE, nrows), F32)]
    w = _peer_tables_call(_peer_u_kernel, idx_streams, [xn, gates], u_packed, fold, PEER_SEL,
                          scratch, "peer_gather_u")
    return _peer_tables_call(_peer_v_kernel, idx_streams, [w, x], v_packed, jnp.transpose(fold), D_MODEL,
                             scratch, "peer_gather_v")


def _final_kernel(x_ref, g_ref, o_ref):
    o_ref[...] = _rms(x_ref[...], g_ref[...])


def _final_norm(x, g):
    ntok = x.shape[0]
    tm = 1024
    return pl.pallas_call(
        _final_kernel,
        grid=(ntok // tm,),
        in_specs=[pl.BlockSpec((tm, D_MODEL), lambda i: (i, 0)), pl.BlockSpec((1, D_MODEL), lambda i: (0, 0))],
        out_specs=pl.BlockSpec((tm, D_MODEL), lambda i: (i, 0)),
        out_shape=jax.ShapeDtypeStruct((ntok, D_MODEL), F32),
        compiler_params=_params(48),
        name="final_norm",
    )(x, g)


def _prepare_layer(p, layer):
    scale = HEAD_DIM ** -0.5
    w_in = p["w_in"][layer]
    qcols = jnp.concatenate([
        jnp.full((W_A,), scale, F32), jnp.ones((2 * W_A,), F32),
        jnp.full((W_B,), scale, F32), jnp.ones((2 * W_B + W_C,), F32)])
    row = lambda a: a.astype(F32).reshape(1, -1)
    return dict(
        norm1=row(p["norm1_g"][layer]),
        w_in=(w_in * qcols[None, :]).astype(BF16),
        na_bias=_na_bias_table(p["rpb_a"][layer]),
        gn_a=row(p["out_norm_a"][layer]),
        gn_b=row(p["out_norm_b"][layer]),
        gn_c=row(p["out_norm_c"][layer]),
        s5=_s5_matrices(p["ssm_a_re"][layer], p["ssm_a_im"][layer], p["ssm_log_step"][layer],
                        p["ssm_b_re"][layer], p["ssm_b_im"][layer], p["ssm_c_re"][layer],
                        p["ssm_c_im"][layer], p["ssm_d"][layer]),
        w_glu=p["w_glu"][layer].astype(BF16),
        b_glu=row(p["b_glu"][layer]),
        w_out=p["w_out"][layer].astype(BF16),
        norm2=row(p["norm2_g"][layer]),
        wq=p["peer_wq"][layer].astype(BF16),
        keys=p["peer_keys"][layer].astype(BF16).reshape(PEER_HEADS * 2, PEER_KEYS, PEER_DKEY // 2),
        u=_pack_table(p["peer_u"][layer]),
        v=_pack_table(p["peer_v"][layer]),
    )


def _trunk(x3, layers, rel_bias, final_g):
    n, l, _ = x3.shape
    x = x3.reshape(n * l, D_MODEL)
    for lp in layers:
        a3, b3, uc = _inproj(x, lp["norm1"], lp["w_in"])
        oa = _na_attention(a3.reshape(n, l, 3 * W_A), lp["na_bias"], lp["gn_a"]).reshape(n * l, W_A)
        ob = _dilated_mixture(b3.reshape(n, l, 3 * W_B), rel_bias, lp["gn_b"])
        yc = _s5(uc, n, l, lp["s5"])
        x = _outproj(x, oa, ob, yc, lp["w_out"], lp["w_glu"], lp["b_glu"], lp["gn_c"])
        x = _peer(x, lp["norm2"], lp["wq"], lp["keys"], lp["u"], lp["v"])
    return _final_norm(x, final_g).reshape(n, l, D_MODEL)


def kernel(x_prompt, x_sample, norm1_g, w_in, rpb_a, rel_bias, ssm_a_re, ssm_a_im, ssm_log_step, ssm_b_re, ssm_b_im, ssm_c_re, ssm_c_im, ssm_d, w_glu, b_glu, out_norm_a, out_norm_b, out_norm_c, w_out, norm2_g, peer_wq, peer_keys, peer_u, peer_v, final_g):
    p = dict(norm1_g=norm1_g, w_in=w_in, rpb_a=rpb_a, ssm_a_re=ssm_a_re, ssm_a_im=ssm_a_im,
             ssm_log_step=ssm_log_step, ssm_b_re=ssm_b_re, ssm_b_im=ssm_b_im, ssm_c_re=ssm_c_re,
             ssm_c_im=ssm_c_im, ssm_d=ssm_d, w_glu=w_glu, b_glu=b_glu, out_norm_a=out_norm_a,
             out_norm_b=out_norm_b, out_norm_c=out_norm_c, w_out=w_out, norm2_g=norm2_g,
             peer_wq=peer_wq, peer_keys=peer_keys, peer_u=peer_u, peer_v=peer_v)
    layers = [_prepare_layer(p, layer) for layer in range(DEPTH)]
    fg = final_g.astype(F32).reshape(1, D_MODEL)
    return (_trunk(x_prompt, layers, rel_bias, fg), _trunk(x_sample, layers, rel_bias, fg))
```

```python
import functools
import math

import jax
import jax.numpy as jnp
from jax import lax
from jax.experimental import pallas as pl
from jax.experimental.pallas import tpu as pltpu

F32 = jnp.float32
BF16 = jnp.bfloat16
I32 = jnp.int32

D_MODEL = 1024
DEPTH = 4
GRID_W = 64
HEAD_DIM = 64
N_HEADS_A = 4
W_A = N_HEADS_A * HEAD_DIM
NA_ROWS = 8
NA_COLS = 16
N_HEADS_B = 8
W_B = N_HEADS_B * HEAD_DIM
DILATIONS = ((128, 1), (512, 4), (2048, 16))
SSM_GROUP = 16
W_C = 256
N_GROUPS_C = W_C // SSM_GROUP
SSM_STATE = 64
NUM_BUCKETS = 32
MAX_DISTANCE = 1024
PEER_HEADS = 8
PEER_KEYS = 128
PEER_EXPERTS = PEER_KEYS * PEER_KEYS
PEER_TOPK = 16
PEER_DKEY = 256
PEER_SEL = PEER_HEADS * PEER_TOPK
EPS = 1e-6
NEG = -1e30

BAND = 64
S5_CHUNK = 32
S5_PAD = 128
HALF = D_MODEL // 2
WORD_ROWS = HALF // 128

MIB = 1024 * 1024


def _params(vmem_mib, n_axes=1):
    return pltpu.CompilerParams(
        vmem_limit_bytes=vmem_mib * MIB,
        dimension_semantics=("arbitrary",) * n_axes,
    )


def _rms(x, g):
    return x * lax.rsqrt(jnp.mean(x * x, axis=-1, keepdims=True) + EPS) * g


def _inproj_kernel(x_ref, g_ref, w_ref, a_ref, b_ref, c_ref):
    h = _rms(x_ref[...], g_ref[...]).astype(BF16)
    p = jnp.dot(h, w_ref[...], preferred_element_type=F32)
    a_ref[...] = p[:, : 3 * W_A].astype(BF16)
    b_ref[...] = p[:, 3 * W_A : 3 * W_A + 3 * W_B].astype(BF16)
    c_ref[...] = p[:, 3 * W_A + 3 * W_B :]


def _inproj(x, g, w):
    ntok = x.shape[0]
    tm = 512
    wcols = w.shape[1]
    return pl.pallas_call(
        _inproj_kernel,
        grid=(ntok // tm,),
        in_specs=[
            pl.BlockSpec((tm, D_MODEL), lambda i: (i, 0)),
            pl.BlockSpec((1, D_MODEL), lambda i: (0, 0)),
            pl.BlockSpec((D_MODEL, wcols), lambda i: (0, 0)),
        ],
        out_specs=[
            pl.BlockSpec((tm, 3 * W_A), lambda i: (i, 0)),
            pl.BlockSpec((tm, 3 * W_B), lambda i: (i, 0)),
            pl.BlockSpec((tm, W_C), lambda i: (i, 0)),
        ],
        out_shape=[
            jax.ShapeDtypeStruct((ntok, 3 * W_A), BF16),
            jax.ShapeDtypeStruct((ntok, 3 * W_B), BF16),
            jax.ShapeDtypeStruct((ntok, W_C), F32),
        ],
        compiler_params=_params(48),
        name="inproj",
    )(x, g, w)


def _paired_attention(q_ref, key_refs, value_refs, pairs, bias_of):
    pair = 2 * HEAD_DIM
    first = lax.broadcasted_iota(I32, (1, pair), 1) < HEAD_DIM
    nt = (((1,), (1,)), ((), ()))
    scores, values = [], []
    for p in pairs:
        cols = slice(p * pair, (p + 1) * pair)
        q2 = q_ref[0, :, cols]
        kw = jnp.concatenate([r[0, :, cols] for r in key_refs], axis=0)
        values.append(jnp.concatenate([r[0, :, cols] for r in value_refs], axis=0))
        for half in range(2):
            qm = jnp.where(first if half == 0 else jnp.logical_not(first), q2, jnp.zeros_like(q2))
            scores.append(lax.dot_general(qm, kw, nt, preferred_element_type=F32) + bias_of(2 * p + half))
    probs, sums, maxes = [], [], []
    for s in scores:
        m = jnp.max(s, axis=-1, keepdims=True)
        e = jnp.exp(s - m)
        probs.append(e.astype(BF16))
        sums.append(jnp.sum(e, axis=-1, keepdims=True))
        maxes.append(m)
    res = []
    for n, vw in enumerate(values):
        o = [jnp.dot(probs[2 * n + h], vw, preferred_element_type=F32) / sums[2 * n + h] for h in range(2)]
        lse = [maxes[2 * n + h] + jnp.log(sums[2 * n + h]) for h in range(2)]
        res.append((jnp.where(first, o[0], o[1]), jnp.where(first, lse[0], lse[1])))
    return res


NA_TILE_ROWS = NA_ROWS // 2


def _na_kernel(q_ref, kp_ref, k_ref, kn_ref, vp_ref, v_ref, vn_ref, b_ref, g_ref, o_ref):
    res = _paired_attention(q_ref, (kp_ref, k_ref, kn_ref), (vp_ref, v_ref, vn_ref),
                            range(N_HEADS_A // 2), lambda h: b_ref[0, h])
    o_ref[0] = _rms(jnp.concatenate([o for o, _ in res], axis=-1), g_ref[...]).astype(BF16)


def _na_bias_table(rpb):
    tr = NA_TILE_ROWS
    c = jnp.arange(GRID_W)
    col_start = jnp.clip(c - NA_COLS // 2, 0, GRID_W - NA_COLS)
    col_ok = (c[None, :] >= col_start[:, None]) & (c[None, :] < col_start[:, None] + NA_COLS)
    col_off = jnp.clip(c[None, :] - c[:, None], -(NA_COLS - 1), NA_COLS - 1) + (NA_COLS - 1)
    j = jnp.arange(tr)[:, None]
    kr = jnp.arange(3 * tr)[None, :]
    row_off = kr - j + (NA_ROWS - 1 - tr)
    windows = jnp.stack([(kr >= tr) & (kr < tr + NA_ROWS) & (j >= 0),
                         (kr >= j) & (kr < j + NA_ROWS),
                         (kr < NA_ROWS) & (j >= 0)])
    t = rpb[:, jnp.clip(row_off, 0, 2 * NA_ROWS - 2)][:, :, :, col_off]
    ok = windows[:, None, :, :, None, None] & col_ok[None, None, None, None]
    t = jnp.where(ok, t[None], NEG)
    t = jnp.transpose(t, (0, 1, 2, 4, 3, 5))
    return t.reshape(3, N_HEADS_A, tr * GRID_W, 3 * tr * GRID_W).astype(F32)


def _na_attention(a3, bias, g):
    n, l, _ = a3.shape
    tq = NA_TILE_ROWS * GRID_W
    n_tiles = l // tq
    assert l % tq == 0 and n_tiles >= 3 and 2 * NA_TILE_ROWS == NA_ROWS
    prev = lambda part: pl.BlockSpec((1, tq, W_A), lambda b, i: (b, jnp.maximum(i - 1, 0), part))
    cur = lambda part: pl.BlockSpec((1, tq, W_A), lambda b, i: (b, i, part))
    nxt = lambda part: pl.BlockSpec((1, tq, W_A), lambda b, i: (b, jnp.minimum(i + 1, n_tiles - 1), part))
    variant = lambda b, i: (jnp.where(i == 0, 0, jnp.where(i == n_tiles - 1, 2, 1)), 0, 0, 0)
    return pl.pallas_call(
        _na_kernel,
        grid=(n, n_tiles),
        in_specs=[cur(0), prev(1), cur(1), nxt(1), prev(2), cur(2), nxt(2),
                  pl.BlockSpec((1,) + bias.shape[1:], variant),
                  pl.BlockSpec((1, W_A), lambda b, i: (0, 0))],
        out_specs=cur(0),
        out_shape=jax.ShapeDtypeStruct((n, l, W_A), BF16),
        compiler_params=_params(48, 2),
        name="na_attention",
    )(a3, a3, a3, a3, a3, a3, a3, bias, g)


BAND_PAIRS_PER_GROUP = 2


def _band_kernel(q_ref, k_ref, v_ref, kp_ref, vp_ref, kn_ref, vn_ref, b_ref, o_ref, lse_ref, *, n_tiles):
    i = pl.program_id(1)
    tq = q_ref.shape[1]
    kwid = tq + 2 * BAND
    col = lax.broadcasted_iota(I32, (1, kwid), 1)
    lo = jnp.where(i == 0, BAND, 0)
    hi = jnp.where(i == n_tiles - 1, tq + BAND, kwid)
    edge = jnp.where((col >= lo) & (col < hi), 0.0, NEG)
    pair = 2 * HEAD_DIM
    for p0 in range(0, N_HEADS_B // 2, BAND_PAIRS_PER_GROUP):
        pairs = range(p0, p0 + BAND_PAIRS_PER_GROUP)
        res = _paired_attention(q_ref, (kp_ref, k_ref, kn_ref), (vp_ref, v_ref, vn_ref), pairs,
                                lambda h: b_ref[h] + edge)
        for p, (o, lse) in zip(pairs, res):
            o_ref[0, :, p * pair : (p + 1) * pair] = o
            lse_ref[0, :, p * pair : (p + 1) * pair] = lse


def _t5_bucket(rel):
    nb = NUM_BUCKETS // 2
    ret = jnp.where(rel > 0, nb, 0)
    n = jnp.abs(rel)
    max_exact = nb // 2
    nf = jnp.maximum(n, 1).astype(F32)
    large = max_exact + (jnp.log(nf / max_exact) / math.log(MAX_DISTANCE / max_exact)
                         * (nb - max_exact)).astype(I32)
    large = jnp.minimum(large, nb - 1)
    return ret + jnp.where(n < max_exact, n, large)


BAND_BLOCKS_PER_TILE = 4


def _band_tile(ls):
    assert ls % BAND == 0
    return min(BAND_BLOCKS_PER_TILE, ls // BAND) * BAND


def _band_bias_table(rel_bias, dilation, tq):
    width = tq + 2 * BAND
    span = tq + width - 1
    rel = jnp.arange(span) - (tq - 1) - BAND
    inside = jnp.abs(rel) <= BAND
    vals = jnp.where(inside[:, None], rel_bias[_t5_bucket(jnp.where(inside, rel, 0) * dilation)], NEG)
    z = jnp.transpose(vals).astype(F32)
    skew = jnp.tile(z, (1, tq + 1))[:, : tq * (span + 1)].reshape(N_HEADS_B, tq, span + 1)
    return skew[:, ::-1, :width]


def _band_attention(qkv, bias):
    n, ls, _ = qkv.shape
    tq = _band_tile(ls)
    blocks_per_tile = tq // BAND
    n_blocks = ls // BAND
    assert ls % tq == 0 and bias.shape == (N_HEADS_B, tq, tq + 2 * BAND)
    kern = functools.partial(_band_kernel, n_tiles=ls // tq)
    cur = lambda part: pl.BlockSpec((1, tq, W_B), lambda b, i: (b, i, part))
    prev = lambda part: pl.BlockSpec(
        (1, BAND, W_B), lambda b, i: (b, jnp.maximum(i * blocks_per_tile - 1, 0), part))
    nxt = lambda part: pl.BlockSpec(
        (1, BAND, W_B), lambda b, i: (b, jnp.minimum((i + 1) * blocks_per_tile, n_blocks - 1), part))
    return pl.pallas_call(
        kern,
        grid=(n, ls // tq),
        in_specs=[cur(0), cur(1), cur(2), prev(1), prev(2), nxt(1), nxt(2),
                  pl.BlockSpec(bias.shape, lambda b, i: (0, 0, 0))],
        out_specs=[cur(0), cur(0)],
        out_shape=[jax.ShapeDtypeStruct((n, ls, W_B), F32), jax.ShapeDtypeStruct((n, ls, W_B), F32)],
        compiler_params=_params(48, 2),
        name="band_attention",
    )(qkv, qkv, qkv, qkv, qkv, qkv, qkv, bias)


def _merge_kernel(o1, o2, o3, l1, l2, l3, g_ref, out_ref):
    a, b, c = l1[...], l2[...], l3[...]
    m = jnp.maximum(jnp.maximum(a, b), c)
    ea, eb, ec = jnp.exp(a - m), jnp.exp(b - m), jnp.exp(c - m)
    den = ea + eb + ec
    o = (ea / den) * o1[...] + (eb / den) * o2[...] + (ec / den) * o3[...]
    out_ref[...] = _rms(o, g_ref[...]).astype(BF16)


def _merge(os_, ls_, g):
    ntok = os_[0].shape[0]
    tm = 512
    blk = pl.BlockSpec((tm, W_B), lambda i: (i, 0))
    return pl.pallas_call(
        _merge_kernel,
        grid=(ntok // tm,),
        in_specs=[blk] * 6 + [pl.BlockSpec((1, W_B), lambda i: (0, 0))],
        out_specs=blk,
        out_shape=jax.ShapeDtypeStruct((ntok, W_B), BF16),
        compiler_params=_params(48),
        name="dilated_merge",
    )(*os_, *ls_, g)


def _dilated_mixture(b3, rel_bias, g):
    n, l, width = b3.shape
    outs, lses = [], []
    for window, d in DILATIONS:
        assert window // (2 * d) == BAND and l % d == 0
        ls = l // d
        bias = _band_bias_table(rel_bias, d, _band_tile(ls))

        def to_sub(a):
            return jnp.transpose(a.reshape(n, ls, d, width), (0, 2, 1, 3)).reshape(n * d, ls, width)

        def from_sub(a):
            return jnp.transpose(a.reshape(n, d, ls, W_B), (0, 2, 1, 3)).reshape(n * l, W_B)

        o, lse = _band_attention(b3 if d == 1 else to_sub(b3), bias)
        outs.append(from_sub(o))
        lses.append(from_sub(lse))
    return _merge(outs, lses, g)


LANES = 128


def _s5_kernel(u_ref, wt_ref, e_ref, f_ref, lam_ref, d_ref, y_ref, *, cps):
    t16, nc = u_ref.shape[0] * u_ref.shape[1], u_ref.shape[2]
    u = u_ref[...].reshape(t16, nc)
    ub = u.astype(BF16)
    s = jnp.dot(e_ref[0], ub, preferred_element_type=F32)
    wide = lambda a: jnp.concatenate([a] * (nc // LANES), axis=1)
    lam = lam_ref[0]
    pad = S5_PAD
    chunk_in_seq = lax.broadcasted_iota(I32, (1, nc), 1) % cps

    def entering_state(sr, si, lr, li, backward):
        lr, li = wide(lr), wide(li)
        hr, hi = sr, si
        shift = 1
        while shift < cps:
            amount = nc - shift if backward else shift
            pr, pi = pltpu.roll(hr, amount, axis=1), pltpu.roll(hi, amount, axis=1)
            ok = (chunk_in_seq < cps - shift) if backward else (chunk_in_seq >= shift)
            hr, hi = (hr + jnp.where(ok, lr * pr - li * pi, 0.0), hi + jnp.where(ok, lr * pi + li * pr, 0.0))
            lr, li = lr * lr - li * li, 2.0 * lr * li
            shift *= 2
        amount = nc - 1 if backward else 1
        ok = (chunk_in_seq < cps - 1) if backward else (chunk_in_seq >= 1)
        return (jnp.where(ok, pltpu.roll(hr, amount, axis=1), 0.0),
                jnp.where(ok, pltpu.roll(hi, amount, axis=1), 0.0))

    fr, fi = entering_state(s[0:pad], s[pad : 2 * pad], lam[0:pad], lam[pad : 2 * pad], False)
    br, bi = entering_state(s[2 * pad : 3 * pad], s[3 * pad :], lam[2 * pad : 3 * pad], lam[3 * pad :], True)
    hin = jnp.concatenate([fr, fi, br, bi], axis=0).astype(BF16)
    y = jnp.dot(wt_ref[0], ub, preferred_element_type=F32)
    y = y + jnp.dot(f_ref[0], hin, preferred_element_type=F32)
    y = y + wide(d_ref[0]) * u
    y_ref[...] = y.reshape(y_ref.shape)


def _s5_matrices(a_re, a_im, log_step, b_re, b_im, c_re, c_im, d_skip):
    hp = lax.Precision.HIGHEST
    t, g16, p = S5_CHUNK, SSM_GROUP, SSM_STATE
    step = jnp.exp(log_step.astype(F32))[..., None]
    ar, ai = a_re.astype(F32), a_im.astype(F32)
    decay = jnp.exp(ar * step)
    lb_re = decay * jnp.cos(ai * step)
    lb_im = decay * jnp.sin(ai * step)
    nr = lb_re - 1.0
    den = ar * ar + ai * ai
    z_re = (nr * ar + lb_im * ai) / den
    z_im = (lb_im * ar - nr * ai) / den
    br, bi = b_re.astype(F32), b_im.astype(F32)
    bb_re = z_re[..., None] * br - z_im[..., None] * bi
    bb_im = z_re[..., None] * bi + z_im[..., None] * br
    cr, ci = c_re.astype(F32), c_im.astype(F32)
    def pw_step(carry, _):
        pr, pi = carry
        return (pr * lb_re - pi * lb_im, pr * lb_im + pi * lb_re), (pr, pi)
    _, (pw_re, pw_im) = lax.scan(pw_step, (jnp.ones_like(lb_re), jnp.zeros_like(lb_im)), None, length=t + 1)
    pw_re = jnp.moveaxis(pw_re, 0, 2)
    pw_im = jnp.moveaxis(pw_im, 0, 2)
    lbb_re = pw_re[..., None] * bb_re[:, :, None] - pw_im[..., None] * bb_im[:, :, None]
    lbb_im = pw_re[..., None] * bb_im[:, :, None] + pw_im[..., None] * bb_re[:, :, None]
    taps = (jnp.einsum("dgop,dgkpi->dgkoi", cr, lbb_re[:, :, :t], precision=hp)
            - jnp.einsum("dgop,dgkpi->dgkoi", ci, lbb_im[:, :, :t], precision=hp))
    r_idx = jnp.arange(t)[:, None]
    s_idx = jnp.arange(t)[None, :]
    lag = s_idx - r_idx
    by_lag = jnp.concatenate([jnp.flip(taps[1][:, 1:], axis=1), taps[0][:, :1] + taps[1][:, :1],
                              taps[0][:, 1:]], axis=1)
    wt = jnp.transpose(by_lag[:, lag + (t - 1)], (0, 1, 4, 2, 3)).reshape(N_GROUPS_C, t * g16, t * g16)
    pad = S5_PAD - p

    def e_part(x):
        x = jnp.transpose(x, (0, 1, 3, 2)).reshape(N_GROUPS_C, t * g16, p)
        return jnp.pad(x, ((0, 0), (0, 0), (0, pad)))

    rev = jnp.arange(t - 1, -1, -1)
    e = jnp.concatenate([e_part(lbb_re[0][:, rev]), e_part(lbb_im[0][:, rev]),
                         e_part(lbb_re[1][:, :t]), e_part(lbb_im[1][:, :t])], axis=-1)
    def f_parts(d, pows):
        gr = cr[d][:, None] * pw_re[d][:, pows][:, :, None] - ci[d][:, None] * pw_im[d][:, pows][:, :, None]
        gi = cr[d][:, None] * pw_im[d][:, pows][:, :, None] + ci[d][:, None] * pw_re[d][:, pows][:, :, None]
        def shp(x):
            x = jnp.transpose(x, (0, 3, 1, 2)).reshape(N_GROUPS_C, p, t * g16)
            return jnp.pad(x, ((0, 0), (0, pad), (0, 0)))
        return shp(gr), shp(-gi)

    f_fr, f_fi = f_parts(0, jnp.arange(1, t + 1))
    f_br, f_bi = f_parts(1, jnp.arange(t, 0, -1))
    f = jnp.concatenate([f_fr, f_fi, f_br, f_bi], axis=1)

    def lam_part(x):
        return jnp.pad(x, ((0, 0), (0, pad)))[:, None, :]

    lam = jnp.concatenate([lam_part(pw_re[0][:, t]), lam_part(pw_im[0][:, t]),
                           lam_part(pw_re[1][:, t]), lam_part(pw_im[1][:, t])], axis=-1)
    dvec = jnp.tile(d_skip.astype(F32).reshape(N_GROUPS_C, 1, g16), (1, t, 1)).reshape(N_GROUPS_C, 1, t * g16)
    tr = lambda a: jnp.transpose(a, (0, 2, 1))
    col = lambda a: jnp.broadcast_to(tr(a), (a.shape[0], a.shape[2], LANES))
    return tr(wt).astype(BF16), tr(e).astype(BF16), tr(f).astype(BF16), col(lam), col(dvec)


def _s5(uc, n, l, mats):
    wt, e, f, lam, dvec = mats
    t, g16 = S5_CHUNK, SSM_GROUP
    assert l % t == 0
    cps = l // t
    nc = cps * n
    assert nc % LANES == 0 and (cps & (cps - 1)) == 0
    ut = jnp.transpose(uc.reshape(nc, t * W_C)).reshape(t, W_C, nc)
    kern = functools.partial(_s5_kernel, cps=cps)
    per_g = lambda a: pl.BlockSpec((1,) + a.shape[1:], lambda g: (g, 0, 0))
    blk = pl.BlockSpec((t, g16, nc), lambda g: (0, g, 0))
    yt = pl.pallas_call(
        kern,
        grid=(N_GROUPS_C,),
        in_specs=[blk, per_g(wt), per_g(e), per_g(f), per_g(lam), per_g(dvec)],
        out_specs=blk,
        out_shape=jax.ShapeDtypeStruct((t, W_C, nc), F32),
        compiler_params=_params(48),
        name="s5_chunked",
    )(ut, wt, e, f, lam, dvec)
    return jnp.transpose(yt.reshape(t * W_C, nc)).reshape(n * l, W_C)


def _outproj_kernel(x_ref, oa_ref, ob_ref, yc_ref, w_ref, wg_ref, bg_ref, gc_ref, o_ref):
    y = yc_ref[...]
    g = 0.5 * y * (1.0 + lax.erf(y * (1.0 / math.sqrt(2.0))))
    z = jnp.dot(g.astype(BF16), wg_ref[...], preferred_element_type=F32) + bg_ref[...]
    oc = g * (1.0 / (1.0 + jnp.exp(-z)))
    ocn = _rms(oc, gc_ref[...]).astype(BF16)
    acc = jnp.dot(oa_ref[...], w_ref[0:W_A, :], preferred_element_type=F32)
    acc = acc + jnp.dot(ob_ref[...], w_ref[W_A : W_A + W_B, :], preferred_element_type=F32)
    acc = acc + jnp.dot(ocn, w_ref[W_A + W_B :, :], preferred_element_type=F32)
    o_ref[...] = x_ref[...] + acc


def _outproj(x, oa, ob, yc, w, wg, bg, gc):
    ntok = x.shape[0]
    tm = 512
    row = lambda width: pl.BlockSpec((tm, width), lambda i: (i, 0))
    full = lambda a: pl.BlockSpec(a.shape, lambda i: (0, 0))
    return pl.pallas_call(
        _outproj_kernel,
        grid=(ntok // tm,),
        in_specs=[row(D_MODEL), row(W_A), row(W_B), row(W_C), full(w), full(wg), full(bg), full(gc)],
        out_specs=row(D_MODEL),
        out_shape=jax.ShapeDtypeStruct((ntok, D_MODEL), F32),
        compiler_params=_params(48),
        name="outproj",
    )(x, oa, ob, yc, w, wg, bg, gc)


def _topk_rows_many(arrays, k, iota, sentinel):
    arrays = list(arrays)
    out_v = [[] for _ in arrays]
    out_i = [[] for _ in arrays]
    for _ in range(k):
        for n, vals in enumerate(arrays):
            m = jnp.max(vals, axis=0, keepdims=True)
            am = jnp.min(jnp.where(vals == m, iota, sentinel), axis=0, keepdims=True)
            out_v[n].append(m)
            out_i[n].append(am)
            arrays[n] = jnp.where(iota == am, -jnp.inf, vals)
    return [(jnp.concatenate(v, axis=0), jnp.concatenate(i, axis=0)) for v, i in zip(out_v, out_i)]


def _topk_rows(vals, k, iota, sentinel):
    return _topk_rows_many([vals], k, iota, sentinel)[0]


_CAND_BLOCKS = ([(0, 1, 0, PEER_TOPK)] + [(a, a + 1, 0, 8) for a in range(1, 8)] + [(8, PEER_TOPK, 0, 1)])


def _candidate_positions(tm):
    cols = []
    for a0, a1, b0, b1 in _CAND_BLOCKS:
        cols += [a * PEER_TOPK + b for a in range(a0, a1) for b in range(b0, b1)]
    covered = set(cols)
    assert all(a * PEER_TOPK + b in covered for a in range(PEER_TOPK) for b in range(PEER_TOPK)
               if (a + 1) * (b + 1) <= PEER_TOPK)
    return jnp.broadcast_to(jnp.asarray(cols, F32)[:, None], (len(cols), tm))


def _candidate_rows(first, second, combine):
    return jnp.concatenate([combine(first[a0:a1], second[b0:b1]) for a0, a1, b0, b1 in _CAND_BLOCKS], axis=0)


def _route_kernel(x_ref, g_ref, wq_ref, keys_ref, pos_ref, xn_ref, idx_ref, gate_ref):
    tm = x_ref.shape[0]
    xn = _rms(x_ref[...], g_ref[...])
    xn_ref[...] = xn
    q = jnp.dot(xn.astype(BF16), wq_ref[...], preferred_element_type=F32).astype(BF16)
    half = PEER_DKEY // 2
    iota_k = lax.broadcasted_iota(I32, (PEER_KEYS, tm), 0).astype(F32)
    cpos = pos_ref[...]
    for h in range(PEER_HEADS):
        scores = []
        for s in range(2):
            qs = q[:, (2 * h + s) * half : (2 * h + s + 1) * half]
            scores.append(lax.dot_general(keys_ref[2 * h + s], qs, (((1,), (1,)), ((), ())),
                                          preferred_element_type=F32))
        (s0, i0), (s1, i1) = _topk_rows_many(scores, PEER_TOPK, iota_k, float(PEER_KEYS))
        cand = _candidate_rows(s0, s1, lambda a, b: a + b)
        cidx = _candidate_rows(i0.astype(I32), i1.astype(I32), lambda a, b: (a * PEER_KEYS + b) * WORD_ROWS)
        best, pos = _topk_rows(cand, PEER_TOPK, cpos, float(PEER_TOPK * PEER_TOPK))
        experts = [jnp.sum(jnp.where(cpos == pos[r : r + 1], cidx, 0), axis=0, keepdims=True)
                   for r in range(PEER_TOPK)]
        ex = jnp.exp(best - best[0:1])
        rows = slice(h * PEER_TOPK, (h + 1) * PEER_TOPK)
        idx_ref[rows, :] = jnp.concatenate(experts, axis=0)
        gate_ref[rows, :] = ex / jnp.sum(ex, axis=0, keepdims=True)


def _route(x, g, wq, keys):
    ntok = x.shape[0]
    tm = 256
    cpos = _candidate_positions(tm)
    return pl.pallas_call(
        _route_kernel,
        grid=(ntok // tm,),
        in_specs=[
            pl.BlockSpec((tm, D_MODEL), lambda i: (i, 0)),
            pl.BlockSpec((1, D_MODEL), lambda i: (0, 0)),
            pl.BlockSpec(wq.shape, lambda i: (0, 0)),
            pl.BlockSpec(keys.shape, lambda i: (0, 0, 0)),
            pl.BlockSpec(cpos.shape, lambda i: (0, 0)),
        ],
        out_specs=[
            pl.BlockSpec((tm, D_MODEL), lambda i: (i, 0)),
            pl.BlockSpec((PEER_SEL, tm), lambda i: (0, i)),
            pl.BlockSpec((PEER_SEL, tm), lambda i: (0, i)),
        ],
        out_shape=[
            jax.ShapeDtypeStruct((ntok, D_MODEL), F32),
            jax.ShapeDtypeStruct((PEER_SEL, ntok), I32),
            jax.ShapeDtypeStruct((PEER_SEL, ntok), F32),
        ],
        compiler_params=_params(48),
        name="peer_route",
    )(x, g, wq, keys, cpos)


PACK_TILE = 512


def _pack_kernel(t_ref, o_ref):
    tb = t_ref[...].astype(BF16).astype(F32)
    hi = pltpu.bitcast(tb[:, :HALF], I32)
    lo = lax.shift_right_logical(pltpu.bitcast(tb[:, HALF:], I32), jnp.int32(16))
    words = hi | lo
    for r in range(WORD_ROWS):
        o_ref[:, r, :] = words[:, r * 128 : (r + 1) * 128]


def _pack_table(tab):
    e = tab.shape[0]
    return _pack_table_3d(tab).reshape(e * WORD_ROWS, 128)


def _pack_table_3d(tab):
    e = tab.shape[0]
    return pl.pallas_call(
        _pack_kernel,
        grid=(e // PACK_TILE,),
        in_specs=[pl.BlockSpec((PACK_TILE, D_MODEL), lambda i: (i, 0))],
        out_specs=pl.BlockSpec((PACK_TILE, WORD_ROWS, 128), lambda i: (i, 0, 0)),
        out_shape=jax.ShapeDtypeStruct((e, WORD_ROWS, 128), I32),
        compiler_params=_params(48),
        name="pack_table",
    )(tab)


PEER_TILE = 512
GATHER_BUFFERS_U = 16
GATHER_BUFFERS_V = 8
GATHER_SPLIT = 2
INDEX_STREAMS = 8
PER_STREAM = PEER_SEL // INDEX_STREAMS
LANE_CHUNKS = D_MODEL // 128


def _gather_rows(idx_refs, tbl_ref, s_ref, t):
    for q in range(PER_STREAM):
        off = t * PER_STREAM + q
        for j in range(INDEX_STREAMS):
            k = q * INDEX_STREAMS + j
            first_row = pl.multiple_of(idx_refs[j][off], WORD_ROWS)
            s_ref[k * WORD_ROWS : (k + 1) * WORD_ROWS, :] = tbl_ref[pl.ds(first_row, WORD_ROWS), :]


def _for_each_token(idx_refs, tbl_ref, bufs, tt, load_group, compute, store_group):
    nb = len(bufs)
    for b in range(nb):
        _gather_rows(idx_refs, tbl_ref, bufs[b], b)

    def trip(p, carry):
        t0 = pl.multiple_of(nb * p, nb)
        rows = load_group(t0)
        outs = []
        for b in range(nb):
            outs.append(compute(bufs[b], rows, b))
            ahead = jnp.minimum(t0 + nb + b, tt - 1)
            _gather_rows(idx_refs, tbl_ref, bufs[b], ahead)
        store_group(t0, outs)
        return carry

    lax.fori_loop(0, tt // nb, trip, 0)


def _unpack_bf16(words):
    hi = pltpu.bitcast(words & jnp.int32(-65536), F32).astype(BF16)
    lo = pltpu.bitcast(words << 16, F32).astype(BF16)
    return hi, lo


def _diag_masks():
    j = lax.broadcasted_iota(I32, (2 * WORD_ROWS, PEER_SEL * WORD_ROWS), 0)
    m = lax.broadcasted_iota(I32, (2 * WORD_ROWS, PEER_SEL * WORD_ROWS), 1)
    r = m % WORD_ROWS
    return (j < WORD_ROWS) & (r == j), (j >= WORD_ROWS) & (r == j - WORD_ROWS)


def _peer_u_kernel(*refs):
    idx_refs = refs[:INDEX_STREAMS]
    xn_ref, gate_ref, tbl_ref, fold_ref, w_ref = refs[INDEX_STREAMS : INDEX_STREAMS + 5]
    scratch = refs[INDEX_STREAMS + 5 :]
    bufs, part_ref = scratch[:-1], scratch[-1]
    tt = xn_ref.shape[0]
    mask_hi, mask_lo = _diag_masks()
    nt = (((1,), (1,)), ((), ()))
    chunk = PEER_SEL * WORD_ROWS // GATHER_SPLIT

    def load_group(t0):
        return xn_ref[pl.ds(t0, len(bufs)), :]

    def compute(s_ref, xrows, b):
        x8 = jnp.concatenate([xrows[b : b + 1, j * 128 : (j + 1) * 128] for j in range(LANE_CHUNKS)],
                             axis=0).astype(BF16)
        parts = []
        for c in range(GATHER_SPLIT):
            rows = slice(c * chunk, (c + 1) * chunk)
            hi, lo = _unpack_bf16(s_ref[rows, :])
            z = (jnp.where(mask_hi[:, rows], lax.dot_general(x8, hi, nt, preferred_element_type=F32), 0.0)
                 + jnp.where(mask_lo[:, rows], lax.dot_general(x8, lo, nt, preferred_element_type=F32), 0.0))
            parts.append(jnp.sum(z, axis=0, keepdims=True))
        return jnp.concatenate(parts, axis=-1)

    def store_group(t0, outs):
        part_ref[pl.ds(t0, len(bufs)), :] = jnp.concatenate(outs, axis=0)

    _for_each_token(idx_refs, tbl_ref, bufs, tt, load_group, compute, store_group)
    v = part_ref[...]
    vh = v.astype(BF16)
    vl = (v - vh.astype(F32)).astype(BF16)
    a = (jnp.dot(vh, fold_ref[...], preferred_element_type=F32)
         + jnp.dot(vl, fold_ref[...], preferred_element_type=F32))
    w_ref[...] = gate_ref[...] * (0.5 * a * (1.0 + lax.erf(a * (1.0 / math.sqrt(2.0)))))


def _peer_v_kernel(*refs):
    idx_refs = refs[:INDEX_STREAMS]
    w_ref, x_ref, tbl_ref, spread_ref, o_ref = refs[INDEX_STREAMS : INDEX_STREAMS + 5]
    scratch = refs[INDEX_STREAMS + 5 :]
    bufs, wrep_ref = scratch[:-1], scratch[-1]
    tt = w_ref.shape[0]
    mask_hi, mask_lo = _diag_masks()
    chunk = PEER_SEL * WORD_ROWS // GATHER_SPLIT
    wrep_ref[...] = jnp.dot(w_ref[...].astype(BF16), spread_ref[...], preferred_element_type=F32)

    def load_group(t0):
        return wrep_ref[pl.ds(t0, len(bufs)), :]

    def compute(s_ref, wrows, b):
        wr = wrows[b : b + 1, :]
        w_hi = jnp.where(mask_hi, wr, 0.0).astype(BF16)
        w_lo = jnp.where(mask_lo, wr, 0.0).astype(BF16)
        o8 = jnp.zeros((LANE_CHUNKS, 128), F32)
        for c in range(GATHER_SPLIT):
            rows = slice(c * chunk, (c + 1) * chunk)
            hi, lo = _unpack_bf16(s_ref[rows, :])
            o8 = (o8 + jnp.dot(w_hi[:, rows], hi, preferred_element_type=F32)
                  + jnp.dot(w_lo[:, rows], lo, preferred_element_type=F32))
        return o8

    def store_group(t0, outs):
        rows = pl.ds(t0, len(bufs))
        for j in range(LANE_CHUNKS):
            cols = slice(j * 128, (j + 1) * 128)
            delta = jnp.concatenate([o8[j : j + 1, :] for o8 in outs], axis=0)
            o_ref[rows, cols] = x_ref[rows, cols] + delta

    _for_each_token(idx_refs, tbl_ref, bufs, tt, load_group, compute, store_group)


def _peer_tables_call(kernel, idx_streams, row_inputs, table, const, out_width, scratch, name):
    ntok = row_inputs[0].shape[0]
    tt = PEER_TILE
    row_spec = lambda a: pl.BlockSpec((tt, a.shape[1]), lambda i: (i, 0))
    return pl.pallas_call(
        kernel,
        grid=(ntok // tt,),
        in_specs=[pl.BlockSpec((tt * PER_STREAM,), lambda i: (i,), memory_space=pltpu.SMEM)] * INDEX_STREAMS
        + [row_spec(a) for a in row_inputs]
        + [pl.BlockSpec(table.shape, lambda i: (0, 0), pipeline_mode=pl.Buffered(1)),
           pl.BlockSpec(const.shape, lambda i: (0, 0))],
        out_specs=pl.BlockSpec((tt, out_width), lambda i: (i, 0)),
        out_shape=jax.ShapeDtypeStruct((ntok, out_width), F32),
        scratch_shapes=scratch,
        compiler_params=_params(56),
        name=name,
    )(*idx_streams, *row_inputs, table, const)


def _peer(x, g, wq, keys, u_packed, v_packed):
    ntok = x.shape[0]
    xn, idx_t, gate_t = _route(x, g, wq, keys)
    idx_streams = [jnp.transpose(idx_t[j * PER_STREAM : (j + 1) * PER_STREAM]).reshape(ntok * PER_STREAM)
                   for j in range(INDEX_STREAMS)]
    gates = jnp.transpose(gate_t.reshape(INDEX_STREAMS, PER_STREAM, ntok), (2, 1, 0)).reshape(ntok, PEER_SEL)
    nrows = PEER_SEL * WORD_ROWS
    m = jnp.arange(nrows)
    fold = (m[:, None] // WORD_ROWS == jnp.arange(PEER_SEL)[None, :]).astype(BF16)
    scratch = lambda buffers: ([pltpu.VMEM((nrows, 128), I32)] * buffers
                               + [pltpu.VMEM((PEER_TILE, nrows), F32)])
    w = _peer_tables_call(_peer_u_kernel, idx_streams, [xn, gates], u_packed, fold, PEER_SEL,
                          scratch(GATHER_BUFFERS_U), "peer_gather_u")
    return _peer_tables_call(_peer_v_kernel, idx_streams, [w, x], v_packed, jnp.transpose(fold), D_MODEL,
                             scratch(GATHER_BUFFERS_V), "peer_gather_v")


def _final_kernel(x_ref, g_ref, o_ref):
    o_ref[...] = _rms(x_ref[...], g_ref[...])


def _final_norm(x, g):
    ntok = x.shape[0]
    tm = 1024
    return pl.pallas_call(
        _final_kernel,
        grid=(ntok // tm,),
        in_specs=[pl.BlockSpec((tm, D_MODEL), lambda i: (i, 0)), pl.BlockSpec((1, D_MODEL), lambda i: (0, 0))],
        out_specs=pl.BlockSpec((tm, D_MODEL), lambda i: (i, 0)),
        out_shape=jax.ShapeDtypeStruct((ntok, D_MODEL), F32),
        compiler_params=_params(48),
        name="final_norm",
    )(x, g)


def _prepare_layer(p, layer):
    scale = HEAD_DIM ** -0.5
    w_in = p["w_in"][layer]
    qcols = jnp.concatenate([
        jnp.full((W_A,), scale, F32), jnp.ones((2 * W_A,), F32),
        jnp.full((W_B,), scale, F32), jnp.ones((2 * W_B + W_C,), F32)])
    row = lambda a: a.astype(F32).reshape(1, -1)
    return dict(
        norm1=row(p["norm1_g"][layer]),
        w_in=(w_in * qcols[None, :]).astype(BF16),
        na_bias=_na_bias_table(p["rpb_a"][layer]),
        gn_a=row(p["out_norm_a"][layer]),
        gn_b=row(p["out_norm_b"][layer]),
        gn_c=row(p["out_norm_c"][layer]),
        s5=_s5_matrices(p["ssm_a_re"][layer], p["ssm_a_im"][layer], p["ssm_log_step"][layer],
                        p["ssm_b_re"][layer], p["ssm_b_im"][layer], p["ssm_c_re"][layer],
                        p["ssm_c_im"][layer], p["ssm_d"][layer]),
        w_glu=p["w_glu"][layer].astype(BF16),
        b_glu=row(p["b_glu"][layer]),
        w_out=p["w_out"][layer].astype(BF16),
        norm2=row(p["norm2_g"][layer]),
        wq=p["peer_wq"][layer].astype(BF16),
        keys=p["peer_keys"][layer].astype(BF16).reshape(PEER_HEADS * 2, PEER_KEYS, PEER_DKEY // 2),
        u=_pack_table(p["peer_u"][layer]),
        v=_pack_table(p["peer_v"][layer]),
    )


def _trunk(x3, layers, rel_bias, final_g):
    n, l, _ = x3.shape
    x = x3.reshape(n * l, D_MODEL)
    for lp in layers:
        a3, b3, uc = _inproj(x, lp["norm1"], lp["w_in"])
        oa = _na_attention(a3.reshape(n, l, 3 * W_A), lp["na_bias"], lp["gn_a"]).reshape(n * l, W_A)
        ob = _dilated_mixture(b3.reshape(n, l, 3 * W_B), rel_bias, lp["gn_b"])
        yc = _s5(uc, n, l, lp["s5"])
        x = _outproj(x, oa, ob, yc, lp["w_out"], lp["w_glu"], lp["b_glu"], lp["gn_c"])
        x = _peer(x, lp["norm2"], lp["wq"], lp["keys"], lp["u"], lp["v"])
    return _final_norm(x, final_g).reshape(n, l, D_MODEL)


def kernel(x_prompt, x_sample, norm1_g, w_in, rpb_a, rel_bias, ssm_a_re, ssm_a_im, ssm_log_step, ssm_b_re, ssm_b_im, ssm_c_re, ssm_c_im, ssm_d, w_glu, b_glu, out_norm_a, out_norm_b, out_norm_c, w_out, norm2_g, peer_wq, peer_keys, peer_u, peer_v, final_g):
    p = dict(norm1_g=norm1_g, w_in=w_in, rpb_a=rpb_a, ssm_a_re=ssm_a_re, ssm_a_im=ssm_a_im,
             ssm_log_step=ssm_log_step, ssm_b_re=ssm_b_re, ssm_b_im=ssm_b_im, ssm_c_re=ssm_c_re,
             ssm_c_im=ssm_c_im, ssm_d=ssm_d, w_glu=w_glu, b_glu=b_glu, out_norm_a=out_norm_a,
             out_norm_b=out_norm_b, out_norm_c=out_norm_c, w_out=w_out, norm2_g=norm2_g,
             peer_wq=peer_wq, peer_keys=peer_keys, peer_u=peer_u, peer_v=peer_v)
    layers = [_prepare_layer(p, layer) for layer in range(DEPTH)]
    fg = final_g.astype(F32).reshape(1, D_MODEL)
    return (_trunk(x_prompt, layers, rel_bias, fg), _trunk(x_sample, layers, rel_bias, fg))
```

```python
import functools
import math

import jax
import jax.numpy as jnp
from jax import lax
from jax.experimental import pallas as pl
from jax.experimental.pallas import tpu as pltpu

F32 = jnp.float32
BF16 = jnp.bfloat16
I32 = jnp.int32

D_MODEL = 1024
DEPTH = 4
GRID_W = 64
HEAD_DIM = 64
N_HEADS_A = 4
W_A = N_HEADS_A * HEAD_DIM
NA_ROWS = 8
NA_COLS = 16
N_HEADS_B = 8
W_B = N_HEADS_B * HEAD_DIM
DILATIONS = ((128, 1), (512, 4), (2048, 16))
SSM_GROUP = 16
W_C = 256
N_GROUPS_C = W_C // SSM_GROUP
SSM_STATE = 64
NUM_BUCKETS = 32
MAX_DISTANCE = 1024
PEER_HEADS = 8
PEER_KEYS = 128
PEER_EXPERTS = PEER_KEYS * PEER_KEYS
PEER_TOPK = 16
PEER_DKEY = 256
PEER_SEL = PEER_HEADS * PEER_TOPK
EPS = 1e-6
NEG = -1e30

BAND = 64
S5_CHUNK = 32
S5_PAD = 128
HALF = D_MODEL // 2
WORD_ROWS = HALF // 128

MIB = 1024 * 1024


def _params(vmem_mib, n_axes=1):
    return pltpu.CompilerParams(
        vmem_limit_bytes=vmem_mib * MIB,
        dimension_semantics=("arbitrary",) * n_axes,
    )


def _rms(x, g):
    return x * lax.rsqrt(jnp.mean(x * x, axis=-1, keepdims=True) + EPS) * g


def _inproj_kernel(x_ref, g_ref, w_ref, a_ref, b_ref, c_ref):
    h = _rms(x_ref[...], g_ref[...]).astype(BF16)
    p = jnp.dot(h, w_ref[...], preferred_element_type=F32)
    a_ref[...] = p[:, : 3 * W_A].astype(BF16)
    b_ref[...] = p[:, 3 * W_A : 3 * W_A + 3 * W_B].astype(BF16)
    c_ref[...] = p[:, 3 * W_A + 3 * W_B :]


def _inproj(x, g, w):
    ntok = x.shape[0]
    tm = 512
    wcols = w.shape[1]
    return pl.pallas_call(
        _inproj_kernel,
        grid=(ntok // tm,),
        in_specs=[
            pl.BlockSpec((tm, D_MODEL), lambda i: (i, 0)),
            pl.BlockSpec((1, D_MODEL), lambda i: (0, 0)),
            pl.BlockSpec((D_MODEL, wcols), lambda i: (0, 0)),
        ],
        out_specs=[
            pl.BlockSpec((tm, 3 * W_A), lambda i: (i, 0)),
            pl.BlockSpec((tm, 3 * W_B), lambda i: (i, 0)),
            pl.BlockSpec((tm, W_C), lambda i: (i, 0)),
        ],
        out_shape=[
            jax.ShapeDtypeStruct((ntok, 3 * W_A), BF16),
            jax.ShapeDtypeStruct((ntok, 3 * W_B), BF16),
            jax.ShapeDtypeStruct((ntok, W_C), F32),
        ],
        compiler_params=_params(48),
        name="inproj",
    )(x, g, w)


def _paired_attention(q_ref, key_refs, value_refs, pairs, bias_of):
    pair = 2 * HEAD_DIM
    first = lax.broadcasted_iota(I32, (1, pair), 1) < HEAD_DIM
    nt = (((1,), (1,)), ((), ()))
    scores, values = [], []
    for p in pairs:
        cols = slice(p * pair, (p + 1) * pair)
        q2 = q_ref[0, :, cols]
        kw = jnp.concatenate([r[0, :, cols] for r in key_refs], axis=0)
        values.append(jnp.concatenate([r[0, :, cols] for r in value_refs], axis=0))
        for half in range(2):
            qm = jnp.where(first if half == 0 else jnp.logical_not(first), q2, jnp.zeros_like(q2))
            scores.append(lax.dot_general(qm, kw, nt, preferred_element_type=F32) + bias_of(2 * p + half))
    probs, sums, maxes = [], [], []
    for s in scores:
        m = jnp.max(s, axis=-1, keepdims=True)
        e = jnp.exp(s - m)
        probs.append(e.astype(BF16))
        sums.append(jnp.sum(e, axis=-1, keepdims=True))
        maxes.append(m)
    res = []
    for n, vw in enumerate(values):
        o = [jnp.dot(probs[2 * n + h], vw, preferred_element_type=F32) / sums[2 * n + h] for h in range(2)]
        lse = [maxes[2 * n + h] + jnp.log(sums[2 * n + h]) for h in range(2)]
        res.append((jnp.where(first, o[0], o[1]), jnp.where(first, lse[0], lse[1])))
    return res


NA_TILE_ROWS = NA_ROWS // 2


def _na_kernel(q_ref, kp_ref, k_ref, kn_ref, vp_ref, v_ref, vn_ref, b_ref, g_ref, o_ref):
    res = _paired_attention(q_ref, (kp_ref, k_ref, kn_ref), (vp_ref, v_ref, vn_ref),
                            range(N_HEADS_A // 2), lambda h: b_ref[0, h])
    o_ref[0] = _rms(jnp.concatenate([o for o, _ in res], axis=-1), g_ref[...]).astype(BF16)


def _na_bias_table(rpb):
    tr = NA_TILE_ROWS
    c = jnp.arange(GRID_W)
    col_start = jnp.clip(c - NA_COLS // 2, 0, GRID_W - NA_COLS)
    col_ok = (c[None, :] >= col_start[:, None]) & (c[None, :] < col_start[:, None] + NA_COLS)
    col_off = jnp.clip(c[None, :] - c[:, None], -(NA_COLS - 1), NA_COLS - 1) + (NA_COLS - 1)
    j = jnp.arange(tr)[:, None]
    kr = jnp.arange(3 * tr)[None, :]
    row_off = kr - j + (NA_ROWS - 1 - tr)
    windows = jnp.stack([(kr >= tr) & (kr < tr + NA_ROWS) & (j >= 0),
                         (kr >= j) & (kr < j + NA_ROWS),
                         (kr < NA_ROWS) & (j >= 0)])
    t = rpb[:, jnp.clip(row_off, 0, 2 * NA_ROWS - 2)][:, :, :, col_off]
    ok = windows[:, None, :, :, None, None] & col_ok[None, None, None, None]
    t = jnp.where(ok, t[None], NEG)
    t = jnp.transpose(t, (0, 1, 2, 4, 3, 5))
    return t.reshape(3, N_HEADS_A, tr * GRID_W, 3 * tr * GRID_W).astype(F32)


def _na_attention(a3, bias, g):
    n, l, _ = a3.shape
    tq = NA_TILE_ROWS * GRID_W
    n_tiles = l // tq
    assert l % tq == 0 and n_tiles >= 3 and 2 * NA_TILE_ROWS == NA_ROWS
    prev = lambda part: pl.BlockSpec((1, tq, W_A), lambda b, i: (b, jnp.maximum(i - 1, 0), part))
    cur = lambda part: pl.BlockSpec((1, tq, W_A), lambda b, i: (b, i, part))
    nxt = lambda part: pl.BlockSpec((1, tq, W_A), lambda b, i: (b, jnp.minimum(i + 1, n_tiles - 1), part))
    variant = lambda b, i: (jnp.where(i == 0, 0, jnp.where(i == n_tiles - 1, 2, 1)), 0, 0, 0)
    return pl.pallas_call(
        _na_kernel,
        grid=(n, n_tiles),
        in_specs=[cur(0), prev(1), cur(1), nxt(1), prev(2), cur(2), nxt(2),
                  pl.BlockSpec((1,) + bias.shape[1:], variant),
                  pl.BlockSpec((1, W_A), lambda b, i: (0, 0))],
        out_specs=cur(0),
        out_shape=jax.ShapeDtypeStruct((n, l, W_A), BF16),
        compiler_params=_params(48, 2),
        name="na_attention",
    )(a3, a3, a3, a3, a3, a3, a3, bias, g)


BAND_PAIRS_PER_GROUP = 4


def _band_kernel(q_ref, k_ref, v_ref, kp_ref, vp_ref, kn_ref, vn_ref, b_ref, o_ref, lse_ref, *, n_tiles):
    i = pl.program_id(1)
    tq = q_ref.shape[1]
    kwid = tq + 2 * BAND
    col = lax.broadcasted_iota(I32, (1, kwid), 1)
    lo = jnp.where(i == 0, BAND, 0)
    hi = jnp.where(i == n_tiles - 1, tq + BAND, kwid)
    edge = jnp.where((col >= lo) & (col < hi), 0.0, NEG)
    pair = 2 * HEAD_DIM
    for p0 in range(0, N_HEADS_B // 2, BAND_PAIRS_PER_GROUP):
        pairs = range(p0, p0 + BAND_PAIRS_PER_GROUP)
        res = _paired_attention(q_ref, (kp_ref, k_ref, kn_ref), (vp_ref, v_ref, vn_ref), pairs,
                                lambda h: b_ref[h] + edge)
        for p, (o, lse) in zip(pairs, res):
            o_ref[0, :, p * pair : (p + 1) * pair] = o
            lse_ref[0, :, p * pair : (p + 1) * pair] = lse


def _t5_bucket(rel):
    nb = NUM_BUCKETS // 2
    ret = jnp.where(rel > 0, nb, 0)
    n = jnp.abs(rel)
    max_exact = nb // 2
    nf = jnp.maximum(n, 1).astype(F32)
    large = max_exact + (jnp.log(nf / max_exact) / math.log(MAX_DISTANCE / max_exact)
                         * (nb - max_exact)).astype(I32)
    large = jnp.minimum(large, nb - 1)
    return ret + jnp.where(n < max_exact, n, large)


BAND_BLOCKS_PER_TILE = 2


def _band_tile(ls):
    assert ls % BAND == 0
    return min(BAND_BLOCKS_PER_TILE, ls // BAND) * BAND


def _band_bias_table(rel_bias, dilation, tq):
    width = tq + 2 * BAND
    span = tq + width - 1
    rel = jnp.arange(span) - (tq - 1) - BAND
    inside = jnp.abs(rel) <= BAND
    vals = jnp.where(inside[:, None], rel_bias[_t5_bucket(jnp.where(inside, rel, 0) * dilation)], NEG)
    z = jnp.transpose(vals).astype(F32)
    skew = jnp.tile(z, (1, tq + 1))[:, : tq * (span + 1)].reshape(N_HEADS_B, tq, span + 1)
    return skew[:, ::-1, :width]


def _band_attention(qkv, bias):
    n, ls, _ = qkv.shape
    tq = _band_tile(ls)
    blocks_per_tile = tq // BAND
    n_blocks = ls // BAND
    assert ls % tq == 0 and bias.shape == (N_HEADS_B, tq, tq + 2 * BAND)
    kern = functools.partial(_band_kernel, n_tiles=ls // tq)
    cur = lambda part: pl.BlockSpec((1, tq, W_B), lambda b, i: (b, i, part))
    prev = lambda part: pl.BlockSpec(
        (1, BAND, W_B), lambda b, i: (b, jnp.maximum(i * blocks_per_tile - 1, 0), part))
    nxt = lambda part: pl.BlockSpec(
        (1, BAND, W_B), lambda b, i: (b, jnp.minimum((i + 1) * blocks_per_tile, n_blocks - 1), part))
    return pl.pallas_call(
        kern,
        grid=(n, ls // tq),
        in_specs=[cur(0), cur(1), cur(2), prev(1), prev(2), nxt(1), nxt(2),
                  pl.BlockSpec(bias.shape, lambda b, i: (0, 0, 0))],
        out_specs=[cur(0), cur(0)],
        out_shape=[jax.ShapeDtypeStruct((n, ls, W_B), F32), jax.ShapeDtypeStruct((n, ls, W_B), F32)],
        compiler_params=_params(48, 2),
        name="band_attention",
    )(qkv, qkv, qkv, qkv, qkv, qkv, qkv, bias)


def _merge_kernel(o1, o2, o3, l1, l2, l3, g_ref, out_ref):
    a, b, c = l1[...], l2[...], l3[...]
    m = jnp.maximum(jnp.maximum(a, b), c)
    ea, eb, ec = jnp.exp(a - m), jnp.exp(b - m), jnp.exp(c - m)
    den = ea + eb + ec
    o = (ea / den) * o1[...] + (eb / den) * o2[...] + (ec / den) * o3[...]
    out_ref[...] = _rms(o, g_ref[...]).astype(BF16)


def _merge(os_, ls_, g):
    ntok = os_[0].shape[0]
    tm = 512
    blk = pl.BlockSpec((tm, W_B), lambda i: (i, 0))
    return pl.pallas_call(
        _merge_kernel,
        grid=(ntok // tm,),
        in_specs=[blk] * 6 + [pl.BlockSpec((1, W_B), lambda i: (0, 0))],
        out_specs=blk,
        out_shape=jax.ShapeDtypeStruct((ntok, W_B), BF16),
        compiler_params=_params(48),
        name="dilated_merge",
    )(*os_, *ls_, g)


def _dilated_mixture(b3, rel_bias, g):
    n, l, width = b3.shape
    outs, lses = [], []
    for window, d in DILATIONS:
        assert window // (2 * d) == BAND and l % d == 0
        ls = l // d
        bias = _band_bias_table(rel_bias, d, _band_tile(ls))

        def to_sub(a):
            return jnp.transpose(a.reshape(n, ls, d, width), (0, 2, 1, 3)).reshape(n * d, ls, width)

        def from_sub(a):
            return jnp.transpose(a.reshape(n, d, ls, W_B), (0, 2, 1, 3)).reshape(n * l, W_B)

        o, lse = _band_attention(b3 if d == 1 else to_sub(b3), bias)
        outs.append(from_sub(o))
        lses.append(from_sub(lse))
    return _merge(outs, lses, g)


LANES = 128


def _s5_kernel(u_ref, wt_ref, e_ref, f_ref, lam_ref, d_ref, y_ref, *, cps):
    t16, nc = u_ref.shape[0] * u_ref.shape[1], u_ref.shape[2]
    u = u_ref[...].reshape(t16, nc)
    ub = u.astype(BF16)
    s = jnp.dot(e_ref[0], ub, preferred_element_type=F32)
    wide = lambda a: jnp.concatenate([a] * (nc // LANES), axis=1)
    lam = lam_ref[0]
    pad = S5_PAD
    chunk_in_seq = lax.broadcasted_iota(I32, (1, nc), 1) % cps

    def entering_state(sr, si, lr, li, backward):
        lr, li = wide(lr), wide(li)
        hr, hi = sr, si
        shift = 1
        while shift < cps:
            amount = nc - shift if backward else shift
            pr, pi = pltpu.roll(hr, amount, axis=1), pltpu.roll(hi, amount, axis=1)
            ok = (chunk_in_seq < cps - shift) if backward else (chunk_in_seq >= shift)
            hr, hi = (hr + jnp.where(ok, lr * pr - li * pi, 0.0), hi + jnp.where(ok, lr * pi + li * pr, 0.0))
            lr, li = lr * lr - li * li, 2.0 * lr * li
            shift *= 2
        amount = nc - 1 if backward else 1
        ok = (chunk_in_seq < cps - 1) if backward else (chunk_in_seq >= 1)
        return (jnp.where(ok, pltpu.roll(hr, amount, axis=1), 0.0),
                jnp.where(ok, pltpu.roll(hi, amount, axis=1), 0.0))

    fr, fi = entering_state(s[0:pad], s[pad : 2 * pad], lam[0:pad], lam[pad : 2 * pad], False)
    br, bi = entering_state(s[2 * pad : 3 * pad], s[3 * pad :], lam[2 * pad : 3 * pad], lam[3 * pad :], True)
    hin = jnp.concatenate([fr, fi, br, bi], axis=0).astype(BF16)
    y = jnp.dot(wt_ref[0], ub, preferred_element_type=F32)
    y = y + jnp.dot(f_ref[0], hin, preferred_element_type=F32)
    y = y + wide(d_ref[0]) * u
    y_ref[...] = y.reshape(y_ref.shape)


def _s5_matrices(a_re, a_im, log_step, b_re, b_im, c_re, c_im, d_skip):
    hp = lax.Precision.HIGHEST
    t, g16, p = S5_CHUNK, SSM_GROUP, SSM_STATE
    step = jnp.exp(log_step.astype(F32))[..., None]
    ar, ai = a_re.astype(F32), a_im.astype(F32)
    decay = jnp.exp(ar * step)
    lb_re = decay * jnp.cos(ai * step)
    lb_im = decay * jnp.sin(ai * step)
    nr = lb_re - 1.0
    den = ar * ar + ai * ai
    z_re = (nr * ar + lb_im * ai) / den
    z_im = (lb_im * ar - nr * ai) / den
    br, bi = b_re.astype(F32), b_im.astype(F32)
    bb_re = z_re[..., None] * br - z_im[..., None] * bi
    bb_im = z_re[..., None] * bi + z_im[..., None] * br
    cr, ci = c_re.astype(F32), c_im.astype(F32)
    def pw_step(carry, _):
        pr, pi = carry
        return (pr * lb_re - pi * lb_im, pr * lb_im + pi * lb_re), (pr, pi)
    _, (pw_re, pw_im) = lax.scan(pw_step, (jnp.ones_like(lb_re), jnp.zeros_like(lb_im)), None, length=t + 1)
    pw_re = jnp.moveaxis(pw_re, 0, 2)
    pw_im = jnp.moveaxis(pw_im, 0, 2)
    lbb_re = pw_re[..., None] * bb_re[:, :, None] - pw_im[..., None] * bb_im[:, :, None]
    lbb_im = pw_re[..., None] * bb_im[:, :, None] + pw_im[..., None] * bb_re[:, :, None]
    taps = (jnp.einsum("dgop,dgkpi->dgkoi", cr, lbb_re[:, :, :t], precision=hp)
            - jnp.einsum("dgop,dgkpi->dgkoi", ci, lbb_im[:, :, :t], precision=hp))
    r_idx = jnp.arange(t)[:, None]
    s_idx = jnp.arange(t)[None, :]
    lag = s_idx - r_idx
    by_lag = jnp.concatenate([jnp.flip(taps[1][:, 1:], axis=1), taps[0][:, :1] + taps[1][:, :1],
                              taps[0][:, 1:]], axis=1)
    wt = jnp.transpose(by_lag[:, lag + (t - 1)], (0, 1, 4, 2, 3)).reshape(N_GROUPS_C, t * g16, t * g16)
    pad = S5_PAD - p

    def e_part(x):
        x = jnp.transpose(x, (0, 1, 3, 2)).reshape(N_GROUPS_C, t * g16, p)
        return jnp.pad(x, ((0, 0), (0, 0), (0, pad)))

    rev = jnp.arange(t - 1, -1, -1)
    e = jnp.concatenate([e_part(lbb_re[0][:, rev]), e_part(lbb_im[0][:, rev]),
                         e_part(lbb_re[1][:, :t]), e_part(lbb_im[1][:, :t])], axis=-1)
    def f_parts(d, pows):
        gr = cr[d][:, None] * pw_re[d][:, pows][:, :, None] - ci[d][:, None] * pw_im[d][:, pows][:, :, None]
        gi = cr[d][:, None] * pw_im[d][:, pows][:, :, None] + ci[d][:, None] * pw_re[d][:, pows][:, :, None]
        def shp(x):
            x = jnp.transpose(x, (0, 3, 1, 2)).reshape(N_GROUPS_C, p, t * g16)
            return jnp.pad(x, ((0, 0), (0, pad), (0, 0)))
        return shp(gr), shp(-gi)

    f_fr, f_fi = f_parts(0, jnp.arange(1, t + 1))
    f_br, f_bi = f_parts(1, jnp.arange(t, 0, -1))
    f = jnp.concatenate([f_fr, f_fi, f_br, f_bi], axis=1)

    def lam_part(x):
        return jnp.pad(x, ((0, 0), (0, pad)))[:, None, :]

    lam = jnp.concatenate([lam_part(pw_re[0][:, t]), lam_part(pw_im[0][:, t]),
                           lam_part(pw_re[1][:, t]), lam_part(pw_im[1][:, t])], axis=-1)
    dvec = jnp.tile(d_skip.astype(F32).reshape(N_GROUPS_C, 1, g16), (1, t, 1)).reshape(N_GROUPS_C, 1, t * g16)
    tr = lambda a: jnp.transpose(a, (0, 2, 1))
    col = lambda a: jnp.broadcast_to(tr(a), (a.shape[0], a.shape[2], LANES))
    return tr(wt).astype(BF16), tr(e).astype(BF16), tr(f).astype(BF16), col(lam), col(dvec)


def _s5(uc, n, l, mats):
    wt, e, f, lam, dvec = mats
    t, g16 = S5_CHUNK, SSM_GROUP
    assert l % t == 0
    cps = l // t
    nc = cps * n
    assert nc % LANES == 0 and (cps & (cps - 1)) == 0
    ut = jnp.transpose(uc.reshape(nc, t * W_C)).reshape(t, W_C, nc)
    kern = functools.partial(_s5_kernel, cps=cps)
    per_g = lambda a: pl.BlockSpec((1,) + a.shape[1:], lambda g: (g, 0, 0))
    blk = pl.BlockSpec((t, g16, nc), lambda g: (0, g, 0))
    yt = pl.pallas_call(
        kern,
        grid=(N_GROUPS_C,),
        in_specs=[blk, per_g(wt), per_g(e), per_g(f), per_g(lam), per_g(dvec)],
        out_specs=blk,
        out_shape=jax.ShapeDtypeStruct((t, W_C, nc), F32),
        compiler_params=_params(48),
        name="s5_chunked",
    )(ut, wt, e, f, lam, dvec)
    return jnp.transpose(yt.reshape(t * W_C, nc)).reshape(n * l, W_C)


def _outproj_kernel(x_ref, oa_ref, ob_ref, yc_ref, w_ref, wg_ref, bg_ref, gc_ref, o_ref):
    y = yc_ref[...]
    g = 0.5 * y * (1.0 + lax.erf(y * (1.0 / math.sqrt(2.0))))
    z = jnp.dot(g.astype(BF16), wg_ref[...], preferred_element_type=F32) + bg_ref[...]
    oc = g * (1.0 / (1.0 + jnp.exp(-z)))
    ocn = _rms(oc, gc_ref[...]).astype(BF16)
    acc = jnp.dot(oa_ref[...], w_ref[0:W_A, :], preferred_element_type=F32)
    acc = acc + jnp.dot(ob_ref[...], w_ref[W_A : W_A + W_B, :], preferred_element_type=F32)
    acc = acc + jnp.dot(ocn, w_ref[W_A + W_B :, :], preferred_element_type=F32)
    o_ref[...] = x_ref[...] + acc


def _outproj(x, oa, ob, yc, w, wg, bg, gc):
    ntok = x.shape[0]
    tm = 512
    row = lambda width: pl.BlockSpec((tm, width), lambda i: (i, 0))
    full = lambda a: pl.BlockSpec(a.shape, lambda i: (0, 0))
    return pl.pallas_call(
        _outproj_kernel,
        grid=(ntok // tm,),
        in_specs=[row(D_MODEL), row(W_A), row(W_B), row(W_C), full(w), full(wg), full(bg), full(gc)],
        out_specs=row(D_MODEL),
        out_shape=jax.ShapeDtypeStruct((ntok, D_MODEL), F32),
        compiler_params=_params(48),
        name="outproj",
    )(x, oa, ob, yc, w, wg, bg, gc)


def _topk_rows_many(arrays, k, iota, sentinel):
    arrays = list(arrays)
    out_v = [[] for _ in arrays]
    out_i = [[] for _ in arrays]
    for _ in range(k):
        for n, vals in enumerate(arrays):
            m = jnp.max(vals, axis=0, keepdims=True)
            am = jnp.min(jnp.where(vals == m, iota, sentinel), axis=0, keepdims=True)
            out_v[n].append(m)
            out_i[n].append(am)
            arrays[n] = jnp.where(iota == am, -jnp.inf, vals)
    return [(jnp.concatenate(v, axis=0), jnp.concatenate(i, axis=0)) for v, i in zip(out_v, out_i)]


def _topk_rows(vals, k, iota, sentinel):
    return _topk_rows_many([vals], k, iota, sentinel)[0]


_CAND_BLOCKS = ([(0, 1, 0, PEER_TOPK)] + [(a, a + 1, 0, 8) for a in range(1, 8)] + [(8, PEER_TOPK, 0, 1)])


def _candidate_positions(tm):
    cols = []
    for a0, a1, b0, b1 in _CAND_BLOCKS:
        cols += [a * PEER_TOPK + b for a in range(a0, a1) for b in range(b0, b1)]
    covered = set(cols)
    assert all(a * PEER_TOPK + b in covered for a in range(PEER_TOPK) for b in range(PEER_TOPK)
               if (a + 1) * (b + 1) <= PEER_TOPK)
    return jnp.broadcast_to(jnp.asarray(cols, F32)[:, None], (len(cols), tm))


def _candidate_rows(first, second, combine):
    return jnp.concatenate([combine(first[a0:a1], second[b0:b1]) for a0, a1, b0, b1 in _CAND_BLOCKS], axis=0)


def _route_kernel(x_ref, g_ref, wq_ref, keys_ref, pos_ref, xn_ref, idx_ref, gate_ref):
    tm = x_ref.shape[0]
    xn = _rms(x_ref[...], g_ref[...])
    xn_ref[...] = xn
    q = jnp.dot(xn.astype(BF16), wq_ref[...], preferred_element_type=F32).astype(BF16)
    half = PEER_DKEY // 2
    iota_k = lax.broadcasted_iota(I32, (PEER_KEYS, tm), 0).astype(F32)
    cpos = pos_ref[...]
    for h in range(PEER_HEADS):
        scores = []
        for s in range(2):
            qs = q[:, (2 * h + s) * half : (2 * h + s + 1) * half]
            scores.append(lax.dot_general(keys_ref[2 * h + s], qs, (((1,), (1,)), ((), ())),
                                          preferred_element_type=F32))
        (s0, i0), (s1, i1) = _topk_rows_many(scores, PEER_TOPK, iota_k, float(PEER_KEYS))
        cand = _candidate_rows(s0, s1, lambda a, b: a + b)
        cidx = _candidate_rows(i0.astype(I32), i1.astype(I32), lambda a, b: (a * PEER_KEYS + b) * WORD_ROWS)
        best, pos = _topk_rows(cand, PEER_TOPK, cpos, float(PEER_TOPK * PEER_TOPK))
        experts = [jnp.sum(jnp.where(cpos == pos[r : r + 1], cidx, 0), axis=0, keepdims=True)
                   for r in range(PEER_TOPK)]
        ex = jnp.exp(best - best[0:1])
        rows = slice(h * PEER_TOPK, (h + 1) * PEER_TOPK)
        idx_ref[rows, :] = jnp.concatenate(experts, axis=0)
        gate_ref[rows, :] = ex / jnp.sum(ex, axis=0, keepdims=True)


def _route(x, g, wq, keys):
    ntok = x.shape[0]
    tm = 256
    cpos = _candidate_positions(tm)
    return pl.pallas_call(
        _route_kernel,
        grid=(ntok // tm,),
        in_specs=[
            pl.BlockSpec((tm, D_MODEL), lambda i: (i, 0)),
            pl.BlockSpec((1, D_MODEL), lambda i: (0, 0)),
            pl.BlockSpec(wq.shape, lambda i: (0, 0)),
            pl.BlockSpec(keys.shape, lambda i: (0, 0, 0)),
            pl.BlockSpec(cpos.shape, lambda i: (0, 0)),
        ],
        out_specs=[
            pl.BlockSpec((tm, D_MODEL), lambda i: (i, 0)),
            pl.BlockSpec((PEER_SEL, tm), lambda i: (0, i)),
            pl.BlockSpec((PEER_SEL, tm), lambda i: (0, i)),
        ],
        out_shape=[
            jax.ShapeDtypeStruct((ntok, D_MODEL), F32),
            jax.ShapeDtypeStruct((PEER_SEL, ntok), I32),
            jax.ShapeDtypeStruct((PEER_SEL, ntok), F32),
        ],
        compiler_params=_params(48),
        name="peer_route",
    )(x, g, wq, keys, cpos)


PACK_TILE = 512


def _pack_kernel(t_ref, o_ref):
    tb = t_ref[...].astype(BF16).astype(F32)
    hi = pltpu.bitcast(tb[:, :HALF], I32)
    lo = lax.shift_right_logical(pltpu.bitcast(tb[:, HALF:], I32), jnp.int32(16))
    words = hi | lo
    for r in range(WORD_ROWS):
        o_ref[:, r, :] = words[:, r * 128 : (r + 1) * 128]


def _pack_table(tab):
    e = tab.shape[0]
    return _pack_table_3d(tab).reshape(e * WORD_ROWS, 128)


def _pack_table_3d(tab):
    e = tab.shape[0]
    return pl.pallas_call(
        _pack_kernel,
        grid=(e // PACK_TILE,),
        in_specs=[pl.BlockSpec((PACK_TILE, D_MODEL), lambda i: (i, 0))],
        out_specs=pl.BlockSpec((PACK_TILE, WORD_ROWS, 128), lambda i: (i, 0, 0)),
        out_shape=jax.ShapeDtypeStruct((e, WORD_ROWS, 128), I32),
        compiler_params=_params(48),
        name="pack_table",
    )(tab)


PEER_TILE = 512
GATHER_BUFFERS_U = 16
GATHER_BUFFERS_V = 8
GATHER_SPLIT = 2
INDEX_STREAMS = 8
PER_STREAM = PEER_SEL // INDEX_STREAMS
LANE_CHUNKS = D_MODEL // 128


def _gather_rows(idx_refs, tbl_ref, s_ref, t):
    for q in range(PER_STREAM):
        off = t * PER_STREAM + q
        for j in range(INDEX_STREAMS):
            k = q * INDEX_STREAMS + j
            first_row = pl.multiple_of(idx_refs[j][off], WORD_ROWS)
            s_ref[k * WORD_ROWS : (k + 1) * WORD_ROWS, :] = tbl_ref[pl.ds(first_row, WORD_ROWS), :]


def _for_each_token(idx_refs, tbl_ref, bufs, tt, load_group, compute, store_group):
    nb = len(bufs)
    for b in range(nb):
        _gather_rows(idx_refs, tbl_ref, bufs[b], b)

    def trip(p, carry):
        t0 = pl.multiple_of(nb * p, nb)
        rows = load_group(t0)
        outs = []
        for b in range(nb):
            outs.append(compute(bufs[b], rows, b))
            ahead = jnp.minimum(t0 + nb + b, tt - 1)
            _gather_rows(idx_refs, tbl_ref, bufs[b], ahead)
        store_group(t0, outs)
        return carry

    lax.fori_loop(0, tt // nb, trip, 0)


def _unpack_bf16(words):
    hi = pltpu.bitcast(words & jnp.int32(-65536), F32).astype(BF16)
    lo = pltpu.bitcast(words << 16, F32).astype(BF16)
    return hi, lo


def _diag_masks():
    j = lax.broadcasted_iota(I32, (2 * WORD_ROWS, PEER_SEL * WORD_ROWS), 0)
    m = lax.broadcasted_iota(I32, (2 * WORD_ROWS, PEER_SEL * WORD_ROWS), 1)
    r = m % WORD_ROWS
    return (j < WORD_ROWS) & (r == j), (j >= WORD_ROWS) & (r == j - WORD_ROWS)


def _peer_u_kernel(*refs):
    idx_refs = refs[:INDEX_STREAMS]
    xn_ref, gate_ref, tbl_ref, fold_ref, w_ref = refs[INDEX_STREAMS : INDEX_STREAMS + 5]
    scratch = refs[INDEX_STREAMS + 5 :]
    bufs, part_ref = scratch[:-1], scratch[-1]
    tt = xn_ref.shape[0]
    mask_hi, mask_lo = _diag_masks()
    nt = (((1,), (1,)), ((), ()))
    chunk = PEER_SEL * WORD_ROWS // GATHER_SPLIT

    def load_group(t0):
        return xn_ref[pl.ds(t0, len(bufs)), :]

    def compute(s_ref, xrows, b):
        x8 = jnp.concatenate([xrows[b : b + 1, j * 128 : (j + 1) * 128] for j in range(LANE_CHUNKS)],
                             axis=0).astype(BF16)
        parts = []
        for c in range(GATHER_SPLIT):
            rows = slice(c * chunk, (c + 1) * chunk)
            hi, lo = _unpack_bf16(s_ref[rows, :])
            z = (jnp.where(mask_hi[:, rows], lax.dot_general(x8, hi, nt, preferred_element_type=F32), 0.0)
                 + jnp.where(mask_lo[:, rows], lax.dot_general(x8, lo, nt, preferred_element_type=F32), 0.0))
            parts.append(jnp.sum(z, axis=0, keepdims=True))
        return jnp.concatenate(parts, axis=-1)

    def store_group(t0, outs):
        part_ref[pl.ds(t0, len(bufs)), :] = jnp.concatenate(outs, axis=0)

    _for_each_token(idx_refs, tbl_ref, bufs, tt, load_group, compute, store_group)
    v = part_ref[...]
    vh = v.astype(BF16)
    vl = (v - vh.astype(F32)).astype(BF16)
    a = (jnp.dot(vh, fold_ref[...], preferred_element_type=F32)
         + jnp.dot(vl, fold_ref[...], preferred_element_type=F32))
    w_ref[...] = gate_ref[...] * (0.5 * a * (1.0 + lax.erf(a * (1.0 / math.sqrt(2.0)))))


def _peer_v_kernel(*refs):
    idx_refs = refs[:INDEX_STREAMS]
    w_ref, x_ref, tbl_ref, spread_ref, o_ref = refs[INDEX_STREAMS : INDEX_STREAMS + 5]
    scratch = refs[INDEX_STREAMS + 5 :]
    bufs, wrep_ref = scratch[:-1], scratch[-1]
    tt = w_ref.shape[0]
    mask_hi, mask_lo = _diag_masks()
    chunk = PEER_SEL * WORD_ROWS // GATHER_SPLIT
    wrep_ref[...] = jnp.dot(w_ref[...].astype(BF16), spread_ref[...], preferred_element_type=F32)

    def load_group(t0):
        return wrep_ref[pl.ds(t0, len(bufs)), :]

    def compute(s_ref, wrows, b):
        wr = wrows[b : b + 1, :]
        w_hi = jnp.where(mask_hi, wr, 0.0).astype(BF16)
        w_lo = jnp.where(mask_lo, wr, 0.0).astype(BF16)
        o8 = jnp.zeros((LANE_CHUNKS, 128), F32)
        for c in range(GATHER_SPLIT):
            rows = slice(c * chunk, (c + 1) * chunk)
            hi, lo = _unpack_bf16(s_ref[rows, :])
            o8 = (o8 + jnp.dot(w_hi[:, rows], hi, preferred_element_type=F32)
                  + jnp.dot(w_lo[:, rows], lo, preferred_element_type=F32))
        return o8

    def store_group(t0, outs):
        rows = pl.ds(t0, len(bufs))
        for j in range(LANE_CHUNKS):
            cols = slice(j * 128, (j + 1) * 128)
            delta = jnp.concatenate([o8[j : j + 1, :] for o8 in outs], axis=0)
            o_ref[rows, cols] = x_ref[rows, cols] + delta

    _for_each_token(idx_refs, tbl_ref, bufs, tt, load_group, compute, store_group)


def _peer_tables_call(kernel, idx_streams, row_inputs, table, const, out_width, scratch, name):
    ntok = row_inputs[0].shape[0]
    tt = PEER_TILE
    row_spec = lambda a: pl.BlockSpec((tt, a.shape[1]), lambda i: (i, 0))
    return pl.pallas_call(
        kernel,
        grid=(ntok // tt,),
        in_specs=[pl.BlockSpec((tt * PER_STREAM,), lambda i: (i,), memory_space=pltpu.SMEM)] * INDEX_STREAMS
        + [row_spec(a) for a in row_inputs]
        + [pl.BlockSpec(table.shape, lambda i: (0, 0), pipeline_mode=pl.Buffered(1)),
           pl.BlockSpec(const.shape, lambda i: (0, 0))],
        out_specs=pl.BlockSpec((tt, out_width), lambda i: (i, 0)),
        out_shape=jax.ShapeDtypeStruct((ntok, out_width), F32),
        scratch_shapes=scratch,
        compiler_params=_params(56),
        name=name,
    )(*idx_streams, *row_inputs, table, const)


def _peer(x, g, wq, keys, u_packed, v_packed):
    ntok = x.shape[0]
    xn, idx_t, gate_t = _route(x, g, wq, keys)
    idx_streams = [jnp.transpose(idx_t[j * PER_STREAM : (j + 1) * PER_STREAM]).reshape(ntok * PER_STREAM)
                   for j in range(INDEX_STREAMS)]
    gates = jnp.transpose(gate_t.reshape(INDEX_STREAMS, PER_STREAM, ntok), (2, 1, 0)).reshape(ntok, PEER_SEL)
    nrows = PEER_SEL * WORD_ROWS
    m = jnp.arange(nrows)
    fold = (m[:, None] // WORD_ROWS == jnp.arange(PEER_SEL)[None, :]).astype(BF16)
    scratch = lambda buffers: ([pltpu.VMEM((nrows, 128), I32)] * buffers
                               + [pltpu.VMEM((PEER_TILE, nrows), F32)])
    w = _peer_tables_call(_peer_u_kernel, idx_streams, [xn, gates], u_packed, fold, PEER_SEL,
                          scratch(GATHER_BUFFERS_U), "peer_gather_u")
    return _peer_tables_call(_peer_v_kernel, idx_streams, [w, x], v_packed, jnp.transpose(fold), D_MODEL,
                             scratch(GATHER_BUFFERS_V), "peer_gather_v")


def _final_kernel(x_ref, g_ref, o_ref):
    o_ref[...] = _rms(x_ref[...], g_ref[...])


def _final_norm(x, g):
    ntok = x.shape[0]
    tm = 1024
    return pl.pallas_call(
        _final_kernel,
        grid=(ntok // tm,),
        in_specs=[pl.BlockSpec((tm, D_MODEL), lambda i: (i, 0)), pl.BlockSpec((1, D_MODEL), lambda i: (0, 0))],
        out_specs=pl.BlockSpec((tm, D_MODEL), lambda i: (i, 0)),
        out_shape=jax.ShapeDtypeStruct((ntok, D_MODEL), F32),
        compiler_params=_params(48),
        name="final_norm",
    )(x, g)


def _prepare_layer(p, layer):
    scale = HEAD_DIM ** -0.5
    w_in = p["w_in"][layer]
    qcols = jnp.concatenate([
        jnp.full((W_A,), scale, F32), jnp.ones((2 * W_A,), F32),
        jnp.full((W_B,), scale, F32), jnp.ones((2 * W_B + W_C,), F32)])
    row = lambda a: a.astype(F32).reshape(1, -1)
    return dict(
        norm1=row(p["norm1_g"][layer]),
        w_in=(w_in * qcols[None, :]).astype(BF16),
        na_bias=_na_bias_table(p["rpb_a"][layer]),
        gn_a=row(p["out_norm_a"][layer]),
        gn_b=row(p["out_norm_b"][layer]),
        gn_c=row(p["out_norm_c"][layer]),
        s5=_s5_matrices(p["ssm_a_re"][layer], p["ssm_a_im"][layer], p["ssm_log_step"][layer],
                        p["ssm_b_re"][layer], p["ssm_b_im"][layer], p["ssm_c_re"][layer],
                        p["ssm_c_im"][layer], p["ssm_d"][layer]),
        w_glu=p["w_glu"][layer].astype(BF16),
        b_glu=row(p["b_glu"][layer]),
        w_out=p["w_out"][layer].astype(BF16),
        norm2=row(p["norm2_g"][layer]),
        wq=p["peer_wq"][layer].astype(BF16),
        keys=p["peer_keys"][layer].astype(BF16).reshape(PEER_HEADS * 2, PEER_KEYS, PEER_DKEY // 2),
        u=_pack_table(p["peer_u"][layer]),
        v=_pack_table(p["peer_v"][layer]),
    )


def _trunk(x3, layers, rel_bias, final_g):
    n, l, _ = x3.shape
    x = x3.reshape(n * l, D_MODEL)
    for lp in layers:
        a3, b3, uc = _inproj(x, lp["norm1"], lp["w_in"])
        oa = _na_attention(a3.reshape(n, l, 3 * W_A), lp["na_bias"], lp["gn_a"]).reshape(n * l, W_A)
        ob = _dilated_mixture(b3.reshape(n, l, 3 * W_B), rel_bias, lp["gn_b"])
        yc = _s5(uc, n, l, lp["s5"])
        x = _outproj(x, oa, ob, yc, lp["w_out"], lp["w_glu"], lp["b_glu"], lp["gn_c"])
        x = _peer(x, lp["norm2"], lp["wq"], lp["keys"], lp["u"], lp["v"])
    return _final_norm(x, final_g).reshape(n, l, D_MODEL)


def kernel(x_prompt, x_sample, norm1_g, w_in, rpb_a, rel_bias, ssm_a_re, ssm_a_im, ssm_log_step, ssm_b_re, ssm_b_im, ssm_c_re, ssm_c_im, ssm_d, w_glu, b_glu, out_norm_a, out_norm_b, out_norm_c, w_out, norm2_g, peer_wq, peer_keys, peer_u, peer_v, final_g):
    p = dict(norm1_g=norm1_g, w_in=w_in, rpb_a=rpb_a, ssm_a_re=ssm_a_re, ssm_a_im=ssm_a_im,
             ssm_log_step=ssm_log_step, ssm_b_re=ssm_b_re, ssm_b_im=ssm_b_im, ssm_c_re=ssm_c_re,
             ssm_c_im=ssm_c_im, ssm_d=ssm_d, w_glu=w_glu, b_glu=b_glu, out_norm_a=out_norm_a,
             out_norm_b=out_norm_b, out_norm_c=out_norm_c, w_out=w_out, norm2_g=norm2_g,
             peer_wq=peer_wq, peer_keys=peer_keys, peer_u=peer_u, peer_v=peer_v)
    layers = [_prepare_layer(p, layer) for layer in range(DEPTH)]
    fg = final_g.astype(F32).reshape(1, D_MODEL)
    return (_trunk(x_prompt, layers, rel_bias, fg), _trunk(x_sample, layers, rel_bias, fg))
```
